```python
import math
import jax, jax.numpy as jnp
from jax import lax
import numpy as np

D_MODEL = 1024
BATCH = 1
SEQ = 16384
DEPTH = 2

ATT_HEADS = 4
ATT_QK_DIM = D_MODEL // 16
ATT_V_DIM = 2 * ATT_QK_DIM
ATT_WIDTH = ATT_HEADS * ATT_V_DIM
Q_COLS = ATT_HEADS * 2 * ATT_QK_DIM
Q_BLOCK = 128
LRU_WIDTH = D_MODEL // 4
LRU_BLOCKS = 4
LRU_BLOCK_DIM = LRU_WIDTH // LRU_BLOCKS
LRU_CONV = 4
LRU_C = 8.0
CONV_WIDTH = D_MODEL // 4
CONV_KERNEL = 31
COL_WIDTHS = (Q_COLS, Q_COLS, ATT_WIDTH, LRU_WIDTH, LRU_WIDTH, CONV_WIDTH, CONV_WIDTH)
IN_COLS = sum(COL_WIDTHS)
SPLIT_POINTS = tuple(sum(COL_WIDTHS[:i + 1]) for i in range(len(COL_WIDTHS) - 1))
V_START = 2 * Q_COLS
V_END = V_START + ATT_WIDTH
D_MIX = ATT_WIDTH + LRU_WIDTH + CONV_WIDTH
N_EXPERTS = 32
TOP_K = 4
D_EXPERT = D_MODEL
SWIGLU_LIMIT = 7.0
SWIGLU_ALPHA = 1.702
MOE_BLOCK = 128
LN_EPS = 1e-5

kernel_name = "hymba_style_diffattn_rglru_conformer_moe_deepnorm"


def layer_norm(x, g, b):
    xf = x.astype(jnp.float32)
    mu = jnp.mean(xf, axis=-1, keepdims=True)
    var = jnp.mean(jnp.square(xf - mu), axis=-1, keepdims=True)
    y = (xf - mu) * lax.rsqrt(var + LN_EPS)
    return (y * g.astype(jnp.float32) + b.astype(jnp.float32)).astype(x.dtype)


def causal_depthwise_conv(x, w, b):
    k = w.shape[0]
    y = lax.conv_general_dilated(
        x, w.astype(x.dtype)[:, None, :], window_strides=(1,),
        padding=[(k - 1, 0)], dimension_numbers=('NWC', 'WIO', 'NWC'),
        feature_group_count=x.shape[-1])
    return y + b.astype(x.dtype)


def diff_attention(q, k, v, lam, slopes):
    B, S, H, _, DK = q.shape
    nqb = S // Q_BLOCK
    scale = DK ** -0.5
    qb = q.reshape(B, nqb, Q_BLOCK, H, 2, DK).transpose(1, 0, 2, 3, 4, 5)
    k_pos = jnp.arange(S)

    def block(args):
        q_blk, i = args
        q_pos = i * Q_BLOCK + jnp.arange(Q_BLOCK)
        s = jnp.einsum('bqhcd,bkhcd->bhcqk', q_blk, k).astype(jnp.float32) * scale
        dist = (q_pos[:, None] - k_pos[None, :]).astype(jnp.float32)
        s = s - (slopes[:, None, None] * dist[None])[None, :, None]
        s = jnp.where((dist >= 0)[None, None, None], s, -jnp.inf)
        p = jax.nn.softmax(s, axis=-1)
        attn = p[:, :, 0] - lam * p[:, :, 1]
        return jnp.einsum('bhqk,bkhd->bqhd', attn.astype(v.dtype), v)

    o = lax.map(block, (qb, jnp.arange(nqb)))
    return o.transpose(1, 0, 2, 3, 4).reshape(B, S, H, v.shape[-1])


def rg_lru(xc, wa, ba, wx, bx, lam_param):
    B, S, C = xc.shape
    xr = xc.reshape(B, S, LRU_BLOCKS, LRU_BLOCK_DIM)
    gate_a = jax.nn.sigmoid(jnp.einsum('bsnc,ncd->bsnd', xr, wa).reshape(B, S, C) + ba).astype(jnp.float32)
    gate_x = jax.nn.sigmoid(jnp.einsum('bsnc,ncd->bsnd', xr, wx).reshape(B, S, C) + bx).astype(jnp.float32)
    log_a = -LRU_C * gate_a * jax.nn.softplus(-lam_param.astype(jnp.float32))
    a = jnp.exp(log_a)
    mult = jnp.sqrt(1.0 - jnp.exp(2.0 * log_a))
    bterm = mult * gate_x * xc.astype(jnp.float32)

    def combine(left, right):
        a_l, b_l = left
        a_r, b_r = right
        return a_l * a_r, a_r * b_l + b_r

    _, h = lax.associative_scan(combine, (a, bterm), axis=1)
    return h.astype(xc.dtype)


def moe(x, router_w, router_b, w1, b1, w2, b2):
    B, S, D = x.shape
    T = B * S
    xf = x.reshape(T, D)
    logits = (xf @ router_w + router_b).astype(jnp.float32)
    top_vals, top_idx = lax.top_k(logits, TOP_K)
    gates = jax.nn.softmax(top_vals, axis=-1)
    n_assign = T * TOP_K
    expert_flat = top_idx.reshape(-1)
    token_flat = jnp.arange(n_assign) // TOP_K
    order = jnp.argsort(expert_flat)
    s_expert = expert_flat[order]
    s_token = token_flat[order]
    s_gate = gates.reshape(-1)[order]
    counts = jnp.bincount(expert_flat, length=N_EXPERTS)
    starts = jnp.cumsum(counts) - counts
    padded = (counts + MOE_BLOCK - 1) // MOE_BLOCK * MOE_BLOCK
    padded_end = jnp.cumsum(padded)
    padded_start = padded_end - padded
    dest = padded_start[s_expert] + jnp.arange(n_assign) - starts[s_expert]
    n_blocks = -(-(n_assign + N_EXPERTS * (MOE_BLOCK - 1)) // MOE_BLOCK)
    rows = n_blocks * MOE_BLOCK
    x_disp = jnp.zeros((rows, D), x.dtype).at[dest].set(xf[s_token])
    block_expert = jnp.minimum(
        jnp.searchsorted(padded_end, jnp.arange(n_blocks) * MOE_BLOCK, side='right'), N_EXPERTS - 1)

    def expert_block(args):
        xb, e = args
        hdn = xb @ w1[e] + b1[e]
        glu, lin = jnp.split(hdn, 2, axis=-1)
        glu = jnp.minimum(glu, SWIGLU_LIMIT)
        lin = jnp.clip(lin, -SWIGLU_LIMIT, SWIGLU_LIMIT)
        act = glu * jax.nn.sigmoid(SWIGLU_ALPHA * glu) * (lin + 1.0)
        return act @ w2[e] + b2[e]

    y_disp = lax.map(expert_block, (x_disp.reshape(n_blocks, MOE_BLOCK, D), block_expert)).reshape(rows, D)
    y = y_disp[dest] * s_gate[:, None].astype(x.dtype)
    out = jax.ops.segment_sum(y, s_token, num_segments=T)
    return out.reshape(B, S, D)


def setup_inputs(seed: int = 0) -> dict:
    key = jax.random.key(seed)
    ks = jax.random.split(key, 32)
    L, D = DEPTH, D_MODEL
    beta = (8.0 * DEPTH) ** -0.25

    def nrm(k, shape, scale):
        return jax.random.normal(k, shape, jnp.float32) * scale

    x = nrm(ks[0], (BATCH, SEQ, D), 1.0)
    ln_in_g = 1.0 + nrm(ks[1], (D,), 0.02)
    ln_in_b = nrm(ks[2], (D,), 0.02)
    col_scale = jnp.ones((IN_COLS,), jnp.float32).at[V_START:V_END].set(beta)
    w_in = nrm(ks[3], (L, D, IN_COLS), D ** -0.5) * col_scale
    lam_q1 = nrm(ks[4], (L, ATT_QK_DIM), 0.1)
    lam_k1 = nrm(ks[5], (L, ATT_QK_DIM), 0.1)
    lam_q2 = nrm(ks[6], (L, ATT_QK_DIM), 0.1)
    lam_k2 = nrm(ks[7], (L, ATT_QK_DIM), 0.1)
    subln_g = 1.0 + nrm(ks[8], (L, ATT_V_DIM), 0.02)
    lru_conv_w = nrm(ks[9], (L, LRU_CONV, LRU_WIDTH), LRU_CONV ** -0.5)
    lru_conv_b = nrm(ks[10], (L, LRU_WIDTH), 0.01)
    lru_wa = nrm(ks[11], (L, LRU_BLOCKS, LRU_BLOCK_DIM, LRU_BLOCK_DIM), LRU_BLOCK_DIM ** -0.5)
    lru_ba = nrm(ks[12], (L, LRU_WIDTH), 0.01)
    lru_wx = nrm(ks[13], (L, LRU_BLOCKS, LRU_BLOCK_DIM, LRU_BLOCK_DIM), LRU_BLOCK_DIM ** -0.5)
    lru_bx = nrm(ks[14], (L, LRU_WIDTH), 0.01)
    u = jax.random.uniform(ks[15], (L, LRU_WIDTH), jnp.float32, minval=0.9, maxval=0.999)
    sig = u ** (1.0 / LRU_C)
    lru_lambda = jnp.log(sig) - jnp.log1p(-sig)
    cf_conv_w = nrm(ks[16], (L, CONV_KERNEL, CONV_WIDTH), CONV_KERNEL ** -0.5)
    cf_conv_b = nrm(ks[17], (L, CONV_WIDTH), 0.01)
    cf_ln_g = 1.0 + nrm(ks[18], (L, CONV_WIDTH), 0.02)
    cf_ln_b = nrm(ks[19], (L, CONV_WIDTH), 0.02)
    w_out = nrm(ks[20], (L, D_MIX, D), D_MIX ** -0.5 * beta)
    ln1_g = 1.0 + nrm(ks[21], (L, D), 0.02)
    ln1_b = nrm(ks[22], (L, D), 0.02)
    router_w = nrm(ks[23], (L, D, N_EXPERTS), D ** -0.5)
    router_b = nrm(ks[24], (L, N_EXPERTS), 0.01)
    moe_w1 = nrm(ks[25], (L, N_EXPERTS, D, 2 * D_EXPERT), D ** -0.5)
    moe_b1 = nrm(ks[26], (L, N_EXPERTS, 2 * D_EXPERT), 0.01)
    moe_w2 = nrm(ks[27], (L, N_EXPERTS, D_EXPERT, D), D_EXPERT ** -0.5 * beta)
    moe_b2 = nrm(ks[28], (L, N_EXPERTS, D), 0.01)
    ln2_g = 1.0 + nrm(ks[29], (L, D), 0.02)
    ln2_b = nrm(ks[30], (L, D), 0.02)
    return {"x": x, "ln_in_g": ln_in_g, "ln_in_b": ln_in_b, "w_in": w_in,
            "lam_q1": lam_q1, "lam_k1": lam_k1, "lam_q2": lam_q2, "lam_k2": lam_k2,
            "subln_g": subln_g, "lru_conv_w": lru_conv_w, "lru_conv_b": lru_conv_b,
            "lru_wa": lru_wa, "lru_ba": lru_ba, "lru_wx": lru_wx, "lru_bx": lru_bx,
            "lru_lambda": lru_lambda, "cf_conv_w": cf_conv_w, "cf_conv_b": cf_conv_b,
            "cf_ln_g": cf_ln_g, "cf_ln_b": cf_ln_b, "w_out": w_out,
            "ln1_g": ln1_g, "ln1_b": ln1_b, "router_w": router_w, "router_b": router_b,
            "moe_w1": moe_w1, "moe_b1": moe_b1, "moe_w2": moe_w2, "moe_b2": moe_b2,
            "ln2_g": ln2_g, "ln2_b": ln2_b}


def reference(x, ln_in_g, ln_in_b, w_in, lam_q1, lam_k1, lam_q2, lam_k2, subln_g,
              lru_conv_w, lru_conv_b, lru_wa, lru_ba, lru_wx, lru_bx, lru_lambda,
              cf_conv_w, cf_conv_b, cf_ln_g, cf_ln_b, w_out, ln1_g, ln1_b,
              router_w, router_b, moe_w1, moe_b1, moe_w2, moe_b2, ln2_g, ln2_b):
    B, S, D = x.shape
    alpha = (2.0 * DEPTH) ** 0.25
    slopes = jnp.exp2(-8.0 * (jnp.arange(ATT_HEADS, dtype=jnp.float32) + 1.0) / ATT_HEADS)
    h = layer_norm(x, ln_in_g, ln_in_b)
    for l in range(DEPTH):
        u = h @ w_in[l]
        q, k, v, lru_x, lru_y, cf_a, cf_b = jnp.split(u, SPLIT_POINTS, axis=-1)
        lam_init = 0.8 - 0.6 * math.exp(-0.3 * l)
        lam = (jnp.exp(jnp.sum(lam_q1[l].astype(jnp.float32) * lam_k1[l].astype(jnp.float32)))
               - jnp.exp(jnp.sum(lam_q2[l].astype(jnp.float32) * lam_k2[l].astype(jnp.float32)))
               + lam_init)
        att = diff_attention(q.reshape(B, S, ATT_HEADS, 2, ATT_QK_DIM),
                             k.reshape(B, S, ATT_HEADS, 2, ATT_QK_DIM),
                             v.reshape(B, S, ATT_HEADS, ATT_V_DIM), lam, slopes)
        af = att.astype(jnp.float32)
        af = af * lax.rsqrt(jnp.mean(jnp.square(af), axis=-1, keepdims=True) + LN_EPS)
        att = (af * subln_g[l].astype(jnp.float32) * (1.0 - lam_init)).astype(h.dtype).reshape(B, S, ATT_WIDTH)
        xc = causal_depthwise_conv(lru_x, lru_conv_w[l], lru_conv_b[l])
        rec = rg_lru(xc, lru_wa[l], lru_ba[l], lru_wx[l], lru_bx[l], lru_lambda[l]) * jax.nn.gelu(lru_y)
        c = cf_a * jax.nn.sigmoid(cf_b)
        c = causal_depthwise_conv(c, cf_conv_w[l], cf_conv_b[l])
        c = jax.nn.silu(layer_norm(c, cf_ln_g[l], cf_ln_b[l]))
        mix = jnp.concatenate([att, rec, c], axis=-1) @ w_out[l]
        h = layer_norm(alpha * h + mix, ln1_g[l], ln1_b[l])
        ff = moe(h, router_w[l], router_b[l], moe_w1[l], moe_b1[l], moe_w2[l], moe_b2[l])
        h = layer_norm(alpha * h + ff, ln2_g[l], ln2_b[l])
    return h
```

```python
import functools
import math

import jax
import jax.numpy as jnp
from jax import lax
from jax.experimental import pallas as pl
from jax.experimental.pallas import tpu as pltpu

F32 = jnp.float32
BF16 = jnp.bfloat16

DEPTH = 2
ATT_HEADS = 4
ATT_QK_DIM = 64
ATT_V_DIM = 128
LRU_BLOCKS = 4
LRU_CONV = 4
LRU_C = 8.0
CONV_KERNEL = 31
N_EXPERTS = 32
TOP_K = 4
SWIGLU_LIMIT = 7.0
SWIGLU_ALPHA = 1.702
LN_EPS = 1e-5
RESID_ALPHA = (2.0 * DEPTH) ** 0.25

VMEM_LIMIT_BYTES = 56 * 1024 * 1024
HALO = 32
LANES = 128
SUBLANES = 8


def _load_rows(ref, n_rows, d, lead=()):
    nc = d // LANES
    return jnp.concatenate([ref[lead + (pl.ds(c, n_rows, stride=nc), slice(None))]
                            for c in range(nc)], axis=-1)


def _store_rows(ref, val):
    n_rows, d = val.shape
    nc = d // LANES
    for c in range(nc):
        ref[pl.ds(c, n_rows, stride=nc), :] = val[:, c * LANES:(c + 1) * LANES]


def _params(n_axes=1):
    return pltpu.CompilerParams(dimension_semantics=("arbitrary",) * n_axes,
                                vmem_limit_bytes=VMEM_LIMIT_BYTES)


def _layer_norm(x, g, b):
    mu = jnp.mean(x, axis=-1, keepdims=True)
    xc = x - mu
    var = jnp.mean(xc * xc, axis=-1, keepdims=True)
    return xc * lax.rsqrt(var + LN_EPS) * g + b


def _sigmoid(x):
    return 1.0 / (1.0 + jnp.exp(-x))


def _ln_kernel(x_ref, g_ref, b_ref, o_ref):
    o_ref[...] = _layer_norm(x_ref[...], g_ref[...], b_ref[...])


def _input_ln(x, g, b, tm):
    s, d = x.shape
    return pl.pallas_call(
        _ln_kernel,
        grid=(s // tm,),
        in_specs=[pl.BlockSpec((tm, d), lambda i: (i, 0)),
                  pl.BlockSpec((1, d), lambda i: (0, 0)),
                  pl.BlockSpec((1, d), lambda i: (0, 0))],
        out_specs=pl.BlockSpec((tm, d), lambda i: (i, 0)),
        out_shape=jax.ShapeDtypeStruct((s, d), F32),
        compiler_params=_params(),
        name="input_ln",
    )(x, g.reshape(1, d), b.reshape(1, d))


def _inproj_kernel(h_ref, w_ref, qkv_ref, rest_ref, *, n_att, q_cols, scale):
    hb = h_ref[...].astype(BF16)
    qkv = jnp.dot(hb, w_ref[:, :n_att], preferred_element_type=F32)
    col = lax.broadcasted_iota(jnp.int32, qkv.shape, 1)
    qkv = jnp.where(col < q_cols, qkv * scale, qkv)
    qkv_ref[...] = qkv.astype(BF16)
    rest_ref[...] = jnp.dot(hb, w_ref[:, n_att:], preferred_element_type=F32)


def _inproj(h, w_bf16, tm):
    s, d = h.shape
    n = w_bf16.shape[1]
    n_att = 3 * ATT_HEADS * ATT_V_DIM
    kern = functools.partial(_inproj_kernel, n_att=n_att, q_cols=ATT_HEADS * 2 * ATT_QK_DIM,
                             scale=ATT_QK_DIM ** -0.5)
    return pl.pallas_call(
        kern,
        grid=(s // tm,),
        in_specs=[pl.BlockSpec((tm, d), lambda i: (i, 0)),
                  pl.BlockSpec((d, n), lambda i: (0, 0))],
        out_specs=[pl.BlockSpec((tm, n_att), lambda i: (i, 0)),
                   pl.BlockSpec((tm, n - n_att), lambda i: (i, 0))],
        out_shape=[jax.ShapeDtypeStruct((s, n_att), BF16),
                   jax.ShapeDtypeStruct((s, n - n_att), F32)],
        compiler_params=_params(),
        name="inproj",
    )(h, w_bf16)


def _attn_kernel(slopes_ref, q_ref, k_ref, v_ref, lq1_ref, lk1_ref, lq2_ref, lk2_ref, g_ref,
                 o_ref, qs_ref, bias_ref, m_ref, l_ref, acc_ref, *, tq, lam_init):
    h = pl.program_id(0)
    qi = pl.program_id(1)
    slope = slopes_ref[h]
    dk = ATT_QK_DIM

    @pl.when(qi == 0)
    def _build_bias():
        r = lax.broadcasted_iota(jnp.int32, (tq, tq), 0)
        c = lax.broadcasted_iota(jnp.int32, (tq, tq), 1)
        rel = slope * (c - r).astype(F32)
        bias_ref[0] = rel
        bias_ref[1] = jnp.where(c <= r, rel, -jnp.inf)

    q = q_ref[...]
    lane = lax.broadcasted_iota(jnp.int32, q.shape, 1)
    zero = jnp.zeros_like(q)
    qs_ref[0:tq, :] = jnp.where(lane < dk, q, zero)
    qs_ref[tq:2 * tq, :] = jnp.where(lane >= dk, q, zero)
    m_ref[...] = jnp.full(m_ref.shape, -jnp.inf, F32)
    l_ref[...] = jnp.zeros(l_ref.shape, F32)
    acc_ref[...] = jnp.zeros(acc_ref.shape, F32)

    def block(j, bias, off):
        start = pl.multiple_of(j * tq, tq)
        kb = k_ref[pl.ds(start, tq), :]
        vb = v_ref[pl.ds(start, tq), :]
        s = lax.dot_general(qs_ref[...], kb, (((1,), (1,)), ((), ())),
                            preferred_element_type=F32)
        s = s.reshape(2, tq, tq) + bias[None]
        m_old = m_ref[...]
        m_new = jnp.maximum(m_old, jnp.max(s, axis=-1, keepdims=True) + off)
        p = jnp.exp(s - (m_new - off))
        alpha = jnp.exp(m_old - m_new)
        l_ref[...] = alpha * l_ref[...] + jnp.sum(p, axis=-1, keepdims=True)
        pv = jnp.dot(p.reshape(2 * tq, tq).astype(BF16), vb, preferred_element_type=F32)
        acc_ref[...] = alpha.reshape(2 * tq, 1) * acc_ref[...] + pv
        m_ref[...] = m_new

    def body(j, carry):
        block(j, bias_ref[0], slope * ((j - qi) * tq).astype(F32))
        return carry

    lax.fori_loop(0, qi, body, 0)
    block(qi, bias_ref[1], jnp.float32(0.0))

    lam = (jnp.exp(jnp.sum(lq1_ref[...] * lk1_ref[...], axis=-1, keepdims=True))
           - jnp.exp(jnp.sum(lq2_ref[...] * lk2_ref[...], axis=-1, keepdims=True)) + lam_init)
    o = acc_ref[...] / l_ref[...].reshape(2 * tq, 1)
    o = o[0:tq] - lam * o[tq:2 * tq]
    o = o * lax.rsqrt(jnp.mean(o * o, axis=-1, keepdims=True) + LN_EPS)
    o_ref[...] = (o * g_ref[...] * (1.0 - lam_init)).astype(o_ref.dtype)


def _attention(qkv, lq1, lk1, lq2, lk2, subln_g, lam_init, tq):
    s = qkv.shape[0]
    hd = ATT_V_DIM
    nh = ATT_HEADS
    slopes = jnp.exp2(-8.0 * (jnp.arange(nh, dtype=F32) + 1.0) / nh)
    vec = lambda a: a.reshape(1, -1).astype(F32)
    small = lambda n: pl.BlockSpec((1, n), lambda h, i, *_: (0, 0))
    kern = functools.partial(_attn_kernel, tq=tq, lam_init=lam_init)
    return pl.pallas_call(
        kern,
        grid_spec=pltpu.PrefetchScalarGridSpec(
            num_scalar_prefetch=1,
            grid=(nh, s // tq),
            in_specs=[pl.BlockSpec((tq, hd), lambda h, i, *_: (i, h)),
                      pl.BlockSpec((s, hd), lambda h, i, *_: (0, nh + h)),
                      pl.BlockSpec((s, hd), lambda h, i, *_: (0, 2 * nh + h)),
                      small(ATT_QK_DIM), small(ATT_QK_DIM), small(ATT_QK_DIM), small(ATT_QK_DIM),
                      small(hd)],
            out_specs=pl.BlockSpec((tq, hd), lambda h, i, *_: (i, h)),
            scratch_shapes=[pltpu.VMEM((2 * tq, hd), BF16),
                            pltpu.VMEM((2, tq, tq), F32),
                            pltpu.VMEM((2, tq, 1), F32),
                            pltpu.VMEM((2, tq, 1), F32),
                            pltpu.VMEM((2 * tq, hd), F32)]),
        out_shape=jax.ShapeDtypeStruct((s, nh * hd), BF16),
        compiler_params=_params(2),
        name="diff_attention",
    )(slopes, qkv, qkv, qkv, vec(lq1), vec(lk1), vec(lq2), vec(lk2), vec(subln_g))


def _seqmix_kernel(u_ref, lcw_ref, lcb_ref, wa_ref, ba_ref, wx_ref, bx_ref, lam_ref,
                   ccw_ref, ccb_ref, cg_ref, cb_ref, o_ref,
                   xbuf_ref, cbuf_ref, a_ref, b_ref, hs_ref, hc_ref, *, tt, w):
    i = pl.program_id(0)

    @pl.when(i == 0)
    def _init():
        xbuf_ref[0:HALO, :] = jnp.zeros((HALO, w), F32)
        cbuf_ref[0:HALO, :] = jnp.zeros((HALO, w), F32)
        hc_ref[...] = jnp.zeros(hc_ref.shape, F32)

    xbuf_ref[HALO:HALO + tt, :] = u_ref[:, 0:w]
    xc = jnp.zeros((tt, w), F32)
    for j in range(LRU_CONV):
        off = HALO - (LRU_CONV - 1) + j
        xc = xc + lcw_ref[j:j + 1, :] * xbuf_ref[off:off + tt, :]
    xc = xc + lcb_ref[...]
    xcb = xc.astype(BF16)
    gate_a = _sigmoid(jnp.dot(xcb, wa_ref[...], preferred_element_type=F32) + ba_ref[...])
    gate_x = _sigmoid(jnp.dot(xcb, wx_ref[...], preferred_element_type=F32) + bx_ref[...])
    nl = -lam_ref[...]
    softplus = jnp.maximum(nl, 0.0) + jnp.log(1.0 + jnp.exp(-jnp.abs(nl)))
    log_a = -LRU_C * gate_a * softplus
    a_ref[...] = jnp.exp(log_a)
    b_ref[...] = jnp.sqrt(1.0 - jnp.exp(2.0 * log_a)) * gate_x * xc

    def step(t, hprev):
        hnew = a_ref[pl.ds(t, 1), :] * hprev + b_ref[pl.ds(t, 1), :]
        hs_ref[pl.ds(t, 1), :] = hnew
        return hnew

    hc_ref[...] = lax.fori_loop(0, tt, step, hc_ref[...], unroll=8)
    o_ref[:, 0:w] = (hs_ref[...] * jax.nn.gelu(u_ref[:, w:2 * w], approximate=True)).astype(o_ref.dtype)
    xbuf_ref[0:HALO, :] = xbuf_ref[tt:tt + HALO, :]

    cbuf_ref[HALO:HALO + tt, :] = u_ref[:, 2 * w:3 * w] * _sigmoid(u_ref[:, 3 * w:4 * w])
    y = jnp.zeros((tt, w), F32)
    for j in range(CONV_KERNEL):
        off = HALO - (CONV_KERNEL - 1) + j
        y = y + ccw_ref[j:j + 1, :] * cbuf_ref[off:off + tt, :]
    y = _layer_norm(y + ccb_ref[...], cg_ref[...], cb_ref[...])
    o_ref[:, w:2 * w] = (y * _sigmoid(y)).astype(o_ref.dtype)
    cbuf_ref[0:HALO, :] = cbuf_ref[tt:tt + HALO, :]


def _block_diag(wb):
    nb, bd, _ = wb.shape
    out = jnp.zeros((nb * bd, nb * bd), wb.dtype)
    for n in range(nb):
        out = out.at[n * bd:(n + 1) * bd, n * bd:(n + 1) * bd].set(wb[n])
    return out


def _seqmix(rest, lcw, lcb, wa, ba, wx, bx, lam, ccw, ccb, cg, cb, tt):
    s, n = rest.shape
    w = n // 4
    row = lambda a: a.reshape(1, w).astype(F32)
    full = lambda r, c: pl.BlockSpec((r, c), lambda i: (0, 0))
    kern = functools.partial(_seqmix_kernel, tt=tt, w=w)
    return pl.pallas_call(
        kern,
        grid=(s // tt,),
        in_specs=[pl.BlockSpec((tt, n), lambda i: (i, 0)),
                  full(LRU_CONV, w), full(1, w), full(w, w), full(1, w), full(w, w), full(1, w),
                  full(1, w), full(CONV_KERNEL, w), full(1, w), full(1, w), full(1, w)],
        out_specs=pl.BlockSpec((tt, 2 * w), lambda i: (i, 0)),
        out_shape=jax.ShapeDtypeStruct((s, 2 * w), BF16),
        scratch_shapes=[pltpu.VMEM((tt + HALO, w), F32), pltpu.VMEM((tt + HALO, w), F32),
                        pltpu.VMEM((tt, w), F32), pltpu.VMEM((tt, w), F32), pltpu.VMEM((tt, w), F32),
                        pltpu.VMEM((1, w), F32)],
        compiler_params=_params(),
        name="seqmix",
    )(rest, lcw, row(lcb), _block_diag(wa).astype(BF16), row(ba), _block_diag(wx).astype(BF16),
      row(bx), row(lam), ccw, row(ccb), row(cg), row(cb))


def _outproj_kernel(att_ref, rc_ref, h_ref, wo_ref, g_ref, b_ref, rw_ref, rb_ref,
                    h1_ref, h1c_ref, idx_ref, gate_ref, rank_ref, cnt_ref, tri_ref, carry_ref,
                    *, tm, n_att):
    i = pl.program_id(0)
    ne = N_EXPERTS

    @pl.when(i == 0)
    def _init():
        r = lax.broadcasted_iota(jnp.int32, (tm, tm), 0)
        c = lax.broadcasted_iota(jnp.int32, (tm, tm), 1)
        tri_ref[...] = jnp.where(c < r, 1.0, 0.0).astype(BF16)
        carry_ref[...] = jnp.zeros(carry_ref.shape, F32)

    mix = (jnp.dot(att_ref[...], wo_ref[0:n_att, :], preferred_element_type=F32)
           + jnp.dot(rc_ref[...], wo_ref[n_att:, :], preferred_element_type=F32))
    h1 = _layer_norm(RESID_ALPHA * h_ref[...] + mix, g_ref[...], b_ref[...])
    h1_ref[...] = h1
    _store_rows(h1c_ref, h1)

    logits = jnp.dot(h1, rw_ref[...], preferred_element_type=F32,
                     precision=lax.Precision.HIGHEST) + rb_ref[...]
    lane = lax.broadcasted_iota(jnp.int32, (tm, ne), 1).astype(F32)
    onehot = jnp.zeros((tm, ne), F32)
    vals, sels = [], []
    for _ in range(TOP_K):
        mx = jnp.max(logits, axis=-1, keepdims=True)
        sel = jnp.min(jnp.where(logits == mx, lane, float(ne)), axis=-1, keepdims=True)
        hit = lane == sel
        onehot = onehot + jnp.where(hit, 1.0, 0.0)
        logits = jnp.where(hit, -jnp.inf, logits)
        vals.append(mx)
        sels.append(sel)
    ex = [jnp.exp(v - vals[0]) for v in vals]
    den = ex[0] + ex[1] + ex[2] + ex[3]
    before = jnp.dot(tri_ref[...], onehot.astype(BF16), preferred_element_type=F32) + carry_ref[...]
    for k in range(TOP_K):
        idx_ref[:, k:k + 1] = sels[k].astype(jnp.int32)
        gate_ref[:, k:k + 1] = ex[k] / den
        rank_ref[:, k:k + 1] = jnp.sum(jnp.where(lane == sels[k], before, 0.0), axis=-1,
                                       keepdims=True).astype(jnp.int32)
    carry_ref[...] = carry_ref[...] + jnp.sum(onehot, axis=0, keepdims=True)
    cnt_ref[...] = carry_ref[...].astype(jnp.int32)


def _outproj_router(att, rc, h, wo_bf16, g, b, rw, rb, tm):
    s, d = h.shape
    n_att = att.shape[1]
    ne = N_EXPERTS
    row = lambda a: a.reshape(1, -1).astype(F32)
    full = lambda r, c: pl.BlockSpec((r, c), lambda i: (0, 0))
    tile = lambda c: pl.BlockSpec((tm, c), lambda i: (i, 0))
    kern = functools.partial(_outproj_kernel, tm=tm, n_att=n_att)
    return pl.pallas_call(
        kern,
        grid=(s // tm,),
        in_specs=[tile(n_att), tile(rc.shape[1]), tile(d), full(wo_bf16.shape[0], d),
                  full(1, d), full(1, d), full(d, ne), full(1, ne)],
        out_specs=[tile(d), pl.BlockSpec((tm * (d // LANES), LANES), lambda i: (i, 0)),
                   tile(TOP_K), tile(TOP_K), tile(TOP_K), full(1, ne)],
        out_shape=[jax.ShapeDtypeStruct((s, d), F32),
                   jax.ShapeDtypeStruct((s * (d // LANES), LANES), F32),
                   jax.ShapeDtypeStruct((s, TOP_K), jnp.int32),
                   jax.ShapeDtypeStruct((s, TOP_K), F32),
                   jax.ShapeDtypeStruct((s, TOP_K), jnp.int32),
                   jax.ShapeDtypeStruct((1, ne), jnp.int32)],
        scratch_shapes=[pltpu.VMEM((tm, tm), BF16), pltpu.VMEM((1, ne), F32)],
        compiler_params=_params(),
        name="outproj_router",
    )(att, rc, h, wo_bf16, row(g), row(b), rw, row(rb))


def _row_copy(src_ref, src_row, dst_ref, dst_row, sem, nc):
    src = src_ref.at[pl.ds(pl.multiple_of(src_row * nc, nc), nc)]
    dst = dst_ref.at[pl.ds(pl.multiple_of(dst_row * nc, nc), nc)]
    return pltpu.make_async_copy(src, dst, sem)


def _dispatch_kernel(pstart_ref, pend_ref, idx_ref, rank_ref, h_ref, xd_ref, zero_ref, sem, zsem,
                     *, tt, tm, nc, n_blocks):
    i = pl.program_id(0)

    def zero_copy(e):
        start = pl.multiple_of((pend_ref[e] - tm) * nc, tm * nc)
        return pltpu.make_async_copy(zero_ref, xd_ref.at[pl.ds(start, tm * nc)], zsem)

    def tail_copy(b):
        start = pl.multiple_of(b * (tm * nc), tm * nc)
        return pltpu.make_async_copy(zero_ref, xd_ref.at[pl.ds(start, tm * nc)], zsem)

    @pl.when(i == 0)
    def _clear_padding():
        zero_ref[...] = jnp.zeros(zero_ref.shape, zero_ref.dtype)
        n_used = pend_ref[N_EXPERTS - 1] // tm

        def start_tail(b, carry):
            tail_copy(b).start()
            return carry

        def wait_tail(b, carry):
            tail_copy(b).wait()
            return carry

        lax.fori_loop(n_used, n_blocks, start_tail, 0)
        lax.fori_loop(n_used, n_blocks, wait_tail, 0)
        for e in range(N_EXPERTS):
            @pl.when(pend_ref[e] > pstart_ref[e])
            def _():
                zero_copy(e).start()
        for e in range(N_EXPERTS):
            @pl.when(pend_ref[e] > pstart_ref[e])
            def _():
                zero_copy(e).wait()

    def issue(t, carry):
        for k in range(TOP_K):
            a = t * TOP_K + k
            dest = pstart_ref[idx_ref[a]] + rank_ref[a]
            _row_copy(h_ref, t, xd_ref, dest, sem, nc).start()
        return carry

    lax.fori_loop(0, tt, issue, 0)

    def drain(t, carry):
        for k in range(TOP_K):
            _row_copy(h_ref, 0, xd_ref, 0, sem, nc).wait()
        return carry

    lax.fori_loop(0, tt, drain, 0)


def _dispatch(h1c, idx_flat, rank_flat, pstart, pend, rows, s, tt, tm):
    nc = h1c.shape[0] // s
    smem = lambda n: pl.BlockSpec((n,), lambda i, *_: (i,), memory_space=pltpu.SMEM)
    kern = functools.partial(_dispatch_kernel, tt=tt, tm=tm, nc=nc, n_blocks=rows // tm)
    return pl.pallas_call(
        kern,
        grid_spec=pltpu.PrefetchScalarGridSpec(
            num_scalar_prefetch=2,
            grid=(s // tt,),
            in_specs=[smem(tt * TOP_K), smem(tt * TOP_K),
                      pl.BlockSpec((tt * nc, LANES), lambda i, *_: (i, 0))],
            out_specs=pl.BlockSpec(memory_space=pl.ANY),
            scratch_shapes=[pltpu.VMEM((tm * nc, LANES), F32), pltpu.SemaphoreType.DMA(()),
                            pltpu.SemaphoreType.DMA(())]),
        out_shape=jax.ShapeDtypeStruct((rows * nc, LANES), F32),
        compiler_params=_params(),
        name="moe_dispatch",
    )(pstart, pend, idx_flat, rank_flat, h1c)


def _expert_kernel(bexp_ref, nused_ref, x_ref, w1_ref, b1_ref, w2_ref, b2_ref, y_ref,
                   w1b_ref, w2b_ref, *, f, chunk, tm):
    b = pl.program_id(0)
    last = nused_ref[0] - 1
    e = bexp_ref[jnp.minimum(b, last)]
    e_prev = bexp_ref[jnp.maximum(jnp.minimum(b, last) - 1, 0)]

    @pl.when((b == 0) | (e != e_prev))
    def _cast_weights():
        def cast(c, carry):
            r = pl.multiple_of(c * chunk, chunk)
            w1b_ref[pl.ds(r, chunk), :] = w1_ref[pl.ds(r, chunk), :].astype(BF16)
            w2b_ref[pl.ds(r, chunk), :] = w2_ref[pl.ds(r, chunk), :].astype(BF16)
            return carry
        lax.fori_loop(0, w1_ref.shape[0] // chunk, cast, 0)

    @pl.when(b > last)
    def _unused_block():
        y_ref[...] = jnp.zeros(y_ref.shape, y_ref.dtype)

    @pl.when(b <= last)
    def _mlp():
        x = _load_rows(x_ref, tm, w1_ref.shape[0]).astype(BF16)
        hdn = jnp.dot(x, w1b_ref[...], preferred_element_type=F32) + b1_ref[...]
        glu = jnp.minimum(hdn[:, 0:f], SWIGLU_LIMIT)
        lin = jnp.clip(hdn[:, f:2 * f], -SWIGLU_LIMIT, SWIGLU_LIMIT)
        act = glu * _sigmoid(SWIGLU_ALPHA * glu) * (lin + 1.0)
        _store_rows(y_ref, jnp.dot(act.astype(BF16), w2b_ref[...], preferred_element_type=F32)
                    + b2_ref[...])


def _experts(x_disp, bexp, nused, w1, b1, w2, b2, layer, tm):
    d = w1.shape[2]
    f = w2.shape[2]
    nc = d // LANES
    rows = x_disp.shape[0] // nc
    assert f == d, "the weight-cast loop assumes d_expert == d_model"

    def blk(b, bexp_ref, nused_ref):
        return jnp.minimum(b, nused_ref[0] - 1)

    def wmap(b, bexp_ref, nused_ref):
        return (layer, bexp_ref[blk(b, bexp_ref, nused_ref)], 0, 0)

    def bmap(b, bexp_ref, nused_ref):
        return (layer, bexp_ref[blk(b, bexp_ref, nused_ref)], 0, 0)

    kern = functools.partial(_expert_kernel, f=f, chunk=128, tm=tm)
    return pl.pallas_call(
        kern,
        grid_spec=pltpu.PrefetchScalarGridSpec(
            num_scalar_prefetch=2,
            grid=(rows // tm,),
            in_specs=[pl.BlockSpec((tm * nc, LANES), lambda b, be, nu: (b, 0)),
                      pl.BlockSpec((None, None, d, 2 * f), wmap),
                      pl.BlockSpec((None, None, 1, 2 * f), bmap),
                      pl.BlockSpec((None, None, f, d), wmap),
                      pl.BlockSpec((None, None, 1, d), bmap)],
            out_specs=pl.BlockSpec((tm * nc, LANES), lambda b, be, nu: (b, 0)),
            scratch_shapes=[pltpu.VMEM((d, 2 * f), BF16), pltpu.VMEM((f, d), BF16)]),
        out_shape=jax.ShapeDtypeStruct((rows * nc, LANES), F32),
        compiler_params=_params(),
        name="moe_experts",
    )(bexp, nused, x_disp, w1, b1.reshape(b1.shape[0], b1.shape[1], 1, -1), w2,
      b2.reshape(b2.shape[0], b2.shape[1], 1, -1))


def _combine_kernel(pstart_ref, idx_ref, rank_ref, h_ref, gate_ref, g_ref, b_ref, yd_ref, o_ref,
                    ybuf_ref, sem, *, tt, nc):
    def issue(t, carry):
        for k in range(TOP_K):
            a = t * TOP_K + k
            src = pstart_ref[idx_ref[a]] + rank_ref[a]
            _row_copy(yd_ref, src, ybuf_ref.at[k], t, sem, nc).start()
        return carry

    lax.fori_loop(0, tt, issue, 0)

    def drain(t, carry):
        for k in range(TOP_K):
            _row_copy(yd_ref, 0, ybuf_ref.at[k], 0, sem, nc).wait()
        return carry

    lax.fori_loop(0, tt, drain, 0)

    d = h_ref.shape[1]
    ff = gate_ref[:, 0:1] * _load_rows(ybuf_ref, tt, d, (0,))
    for k in range(1, TOP_K):
        ff = ff + gate_ref[:, k:k + 1] * _load_rows(ybuf_ref, tt, d, (k,))
    o_ref[...] = _layer_norm(RESID_ALPHA * h_ref[...] + ff, g_ref[...], b_ref[...])


def _combine(y_disp, h1, gates, idx_flat, rank_flat, pstart, g, b, tt):
    s, d = h1.shape
    nc = d // LANES
    smem = lambda n: pl.BlockSpec((n,), lambda i, *_: (i,), memory_space=pltpu.SMEM)
    row = lambda a: a.reshape(1, -1).astype(F32)
    kern = functools.partial(_combine_kernel, tt=tt, nc=nc)
    return pl.pallas_call(
        kern,
        grid_spec=pltpu.PrefetchScalarGridSpec(
            num_scalar_prefetch=1,
            grid=(s // tt,),
            in_specs=[smem(tt * TOP_K), smem(tt * TOP_K),
                      pl.BlockSpec((tt, d), lambda i, *_: (i, 0)),
                      pl.BlockSpec((tt, TOP_K), lambda i, *_: (i, 0)),
                      pl.BlockSpec((1, d), lambda i, *_: (0, 0)),
                      pl.BlockSpec((1, d), lambda i, *_: (0, 0)),
                      pl.BlockSpec(memory_space=pl.ANY)],
            out_specs=pl.BlockSpec((tt, d), lambda i, *_: (i, 0)),
            scratch_shapes=[pltpu.VMEM((TOP_K, tt * nc, LANES), F32), pltpu.SemaphoreType.DMA(())]),
        out_shape=jax.ShapeDtypeStruct((s, d), F32),
        compiler_params=_params(),
        name="moe_combine",
    )(pstart, idx_flat, rank_flat, h1, gates, row(g), row(b), y_disp)


def _tiles(s):
    t = lambda want: math.gcd(s, want)
    return dict(dense=t(512), attn=t(512), seq=t(512), route=t(512), expert=t(256), combine=t(256))


def kernel(x, ln_in_g, ln_in_b, w_in, lam_q1, lam_k1, lam_q2, lam_k2, subln_g, lru_conv_w, lru_conv_b, lru_wa, lru_ba, lru_wx, lru_bx, lru_lambda, cf_conv_w, cf_conv_b, cf_ln_g, cf_ln_b, w_out, ln1_g, ln1_b, router_w, router_b, moe_w1, moe_b1, moe_w2, moe_b2, ln2_g, ln2_b):
    bsz, s, d = x.shape
    assert bsz == 1
    tl = _tiles(s)
    tm = tl["expert"]
    n_blocks = (s * TOP_K) // tm + N_EXPERTS
    rows = n_blocks * tm

    h = _input_ln(x.reshape(s, d), ln_in_g, ln_in_b, tl["dense"])
    for l in range(DEPTH):
        lam_init = 0.8 - 0.6 * math.exp(-0.3 * l)
        qkv, rest = _inproj(h, w_in[l].astype(BF16), tl["dense"])
        att = _attention(qkv, lam_q1[l], lam_k1[l], lam_q2[l], lam_k2[l], subln_g[l], lam_init,
                         tl["attn"])
        rc = _seqmix(rest, lru_conv_w[l], lru_conv_b[l], lru_wa[l], lru_ba[l], lru_wx[l],
                     lru_bx[l], lru_lambda[l], cf_conv_w[l], cf_conv_b[l], cf_ln_g[l], cf_ln_b[l],
                     tl["seq"])
        h1, h1c, idx, gates, rank, counts = _outproj_router(
            att, rc, h, w_out[l].astype(BF16), ln1_g[l], ln1_b[l], router_w[l], router_b[l],
            tl["route"])
        counts = counts.reshape(N_EXPERTS)
        padded = (counts + tm - 1) // tm * tm
        pend = jnp.cumsum(padded).astype(jnp.int32)
        pstart = pend - padded
        bexp = jnp.minimum(jnp.searchsorted(pend, jnp.arange(n_blocks, dtype=jnp.int32) * tm,
                                            side="right"), N_EXPERTS - 1).astype(jnp.int32)
        nused = (pend[-1:] // tm).astype(jnp.int32)
        idx_flat = idx.reshape(s * TOP_K)
        rank_flat = rank.reshape(s * TOP_K)
        x_disp = _dispatch(h1c, idx_flat, rank_flat, pstart, pend, rows, s, tl["route"], tm)
        y_disp = _experts(x_disp, bexp, nused, moe_w1, moe_b1, moe_w2, moe_b2, l, tm)
        h = _combine(y_disp, h1, gates, idx_flat, rank_flat, pstart, ln2_g[l], ln2_b[l],
                     tl["combine"])
    return h.reshape(bsz, s, d)
```

```python
import functools
import math

import jax
import jax.numpy as jnp
from jax import lax
from jax.experimental import pallas as pl
from jax.experimental.pallas import tpu as pltpu

F32 = jnp.float32
BF16 = jnp.bfloat16

DEPTH = 2
ATT_HEADS = 4
ATT_QK_DIM = 64
ATT_V_DIM = 128
LRU_BLOCKS = 4
LRU_CONV = 4
LRU_C = 8.0
CONV_KERNEL = 31
N_EXPERTS = 32
TOP_K = 4
SWIGLU_LIMIT = 7.0
SWIGLU_ALPHA = 1.702
LN_EPS = 1e-5
RESID_ALPHA = (2.0 * DEPTH) ** 0.25

VMEM_LIMIT_BYTES = 56 * 1024 * 1024
HALO = 32
LANES = 128
SUBLANES = 8
MXU_DIM = 256
LOG2E = math.log2(math.e)
V_PAD_ROWS = 16
Q_GROUP = 256


def _load_rows(ref, n_rows, d, lead=()):
    nc = d // LANES
    return jnp.concatenate([ref[lead + (pl.ds(c, n_rows, stride=nc), slice(None))]
                            for c in range(nc)], axis=-1)


def _store_rows(ref, val):
    n_rows, d = val.shape
    nc = d // LANES
    for c in range(nc):
        ref[pl.ds(c, n_rows, stride=nc), :] = val[:, c * LANES:(c + 1) * LANES]


def _params(n_axes=1):
    return pltpu.CompilerParams(dimension_semantics=("arbitrary",) * n_axes,
                                vmem_limit_bytes=VMEM_LIMIT_BYTES)


def _layer_norm(x, g, b):
    mu = jnp.mean(x, axis=-1, keepdims=True)
    xc = x - mu
    var = jnp.mean(xc * xc, axis=-1, keepdims=True)
    return xc * lax.rsqrt(var + LN_EPS) * g + b


def _sigmoid(x):
    return 1.0 / (1.0 + jnp.exp(-x))


def _ln_kernel(x_ref, g_ref, b_ref, o_ref):
    o_ref[...] = _layer_norm(x_ref[...], g_ref[...], b_ref[...])


def _input_ln(x, g, b, tm):
    s, d = x.shape
    return pl.pallas_call(
        _ln_kernel,
        grid=(s // tm,),
        in_specs=[pl.BlockSpec((tm, d), lambda i: (i, 0)),
                  pl.BlockSpec((1, d), lambda i: (0, 0)),
                  pl.BlockSpec((1, d), lambda i: (0, 0))],
        out_specs=pl.BlockSpec((tm, d), lambda i: (i, 0)),
        out_shape=jax.ShapeDtypeStruct((s, d), F32),
        compiler_params=_params(),
        name="input_ln",
    )(x, g.reshape(1, d), b.reshape(1, d))


def _inproj_kernel(h_ref, w_ref, qt_ref, k_ref, vt_ref, rest_ref, *, scale):
    nh, hd = ATT_HEADS, ATT_V_DIM
    w_att = nh * hd
    tm = h_ref.shape[0]
    hb = h_ref[...].astype(BF16)
    q = jnp.dot(hb, w_ref[:, 0:w_att], preferred_element_type=F32) * scale
    k_ref[...] = jnp.dot(hb, w_ref[:, w_att:2 * w_att], preferred_element_type=F32).astype(BF16)
    v = jnp.dot(hb, w_ref[:, 2 * w_att:3 * w_att], preferred_element_type=F32)
    ones = jnp.ones((V_PAD_ROWS, tm), BF16)
    for h in range(nh):
        qt_ref[h] = q[:, h * hd:(h + 1) * hd].T.astype(BF16)
        vt_ref[h, 0:hd, :] = v[:, h * hd:(h + 1) * hd].T.astype(BF16)
        vt_ref[h, hd:hd + V_PAD_ROWS, :] = ones
    rest_ref[...] = jnp.dot(hb, w_ref[:, 3 * w_att:], preferred_element_type=F32)


def _inproj(h, w_bf16, tm):
    s, d = h.shape
    n = w_bf16.shape[1]
    nh, hd = ATT_HEADS, ATT_V_DIM
    n_att = 3 * nh * hd
    kern = functools.partial(_inproj_kernel, scale=ATT_QK_DIM ** -0.5 * LOG2E)
    tposed = lambda r: pl.BlockSpec((nh, None, r, tm), lambda i: (0, i, 0, 0))
    return pl.pallas_call(
        kern,
        grid=(s // tm,),
        in_specs=[pl.BlockSpec((tm, d), lambda i: (i, 0)),
                  pl.BlockSpec((d, n), lambda i: (0, 0))],
        out_specs=[tposed(hd),
                   pl.BlockSpec((tm, nh * hd), lambda i: (i, 0)),
                   tposed(hd + V_PAD_ROWS),
                   pl.BlockSpec((tm, n - n_att), lambda i: (i, 0))],
        out_shape=[jax.ShapeDtypeStruct((nh, s // tm, hd, tm), BF16),
                   jax.ShapeDtypeStruct((s, nh * hd), BF16),
                   jax.ShapeDtypeStruct((nh, s // tm, hd + V_PAD_ROWS, tm), BF16),
                   jax.ShapeDtypeStruct((s, n - n_att), F32)],
        compiler_params=_params(),
        name="inproj",
    )(h, w_bf16)


def _split3(x):
    hi = x.astype(BF16)
    rem = x - hi.astype(F32)
    mid = rem.astype(BF16)
    lo = (rem - mid.astype(F32)).astype(BF16)
    return hi, mid, lo


def _attn_kernel(slopes_ref, qt_ref, k_ref, vt_ref, lq1_ref, lk1_ref, lq2_ref, lk2_ref, g_ref,
                 o_ref, qs_ref, kc_ref, mask_ref, *stat_refs, tq, lam_init):
    h = pl.program_id(0)
    qi = pl.program_id(1)
    slope2 = slopes_ref[h]
    dk, hd = ATT_QK_DIM, ATT_V_DIM
    n_groups = (2 * tq) // Q_GROUP
    m_refs, acc_refs = stat_refs[:n_groups], stat_refs[n_groups:]

    @pl.when(qi == 0)
    def _build_constants():
        c = lax.broadcasted_iota(jnp.int32, (tq, hd), 0)
        col = lax.broadcasted_iota(jnp.int32, (tq, hd), 1)
        c_lo = jnp.bitwise_and(c, MXU_DIM - 1)
        c_hi = c - c_lo
        kc = jnp.where(col < 3, c_hi, jnp.where(col < 6, c_lo, jnp.where(col < 9, 1, 0)))
        kc_ref[...] = kc.astype(F32).astype(BF16)
        r = lax.broadcasted_iota(jnp.int32, (hd, 2 * tq), 1)
        r = jnp.where(r >= tq, r - tq, r)
        row = lax.broadcasted_iota(jnp.int32, (hd, 2 * tq), 0)
        sl = jnp.full((hd, 2 * tq), slope2, F32)
        s_hi, s_mid, s_lo = _split3(sl)
        t_hi, t_mid, t_lo = _split3(-(sl * r.astype(F32)))
        aug = jnp.zeros((hd, 2 * tq), F32)
        for i, piece in enumerate((s_hi, s_mid, s_lo, s_hi, s_mid, s_lo, t_hi, t_mid, t_lo)):
            aug = jnp.where(row == i, piece.astype(F32), aug)
        qs_ref[hd:2 * hd, :] = aug.astype(BF16)
        ck = lax.broadcasted_iota(jnp.int32, (tq, 2 * tq), 0)
        rq = lax.broadcasted_iota(jnp.int32, (tq, 2 * tq), 1)
        rq = jnp.where(rq >= tq, rq - tq, rq)
        mask_ref[...] = jnp.where(ck <= rq, 0.0, -jnp.inf)

    qt = qt_ref[...]
    dim = lax.broadcasted_iota(jnp.int32, qt.shape, 0)
    zero_q = jnp.zeros_like(qt)
    qs_ref[0:hd, 0:tq] = jnp.where(dim < dk, qt, zero_q)
    qs_ref[0:hd, tq:2 * tq] = jnp.where(dim >= dk, qt, zero_q)
    for m_ref, acc_ref in zip(m_refs, acc_refs):
        m_ref[...] = jnp.full(m_ref.shape, -jnp.inf, F32)
        acc_ref[...] = jnp.zeros(acc_ref.shape, F32)

    def block(j, off, diagonal):
        start = pl.multiple_of(j * tq, tq)
        ka = jnp.concatenate([k_ref[pl.ds(start, tq), :], kc_ref[...]], axis=1)
        vt = vt_ref[j]
        scores = {}

        def qk(g):
            cols = slice(g * Q_GROUP, (g + 1) * Q_GROUP)
            st = jnp.dot(ka, qs_ref[:, cols], preferred_element_type=F32)
            scores[g] = st + mask_ref[:, cols] if diagonal else st

        def softmax_pv(g):
            st = scores.pop(g)
            m_old = m_refs[g][...]
            m_new = jnp.maximum(m_old, jnp.max(st, axis=0, keepdims=True) + off)
            p = jnp.exp2(st - (m_new - off)).astype(BF16)
            alpha = jnp.exp2(m_old - m_new)
            acc_refs[g][...] = alpha * acc_refs[g][...] + jnp.dot(vt, p, preferred_element_type=F32)
            m_refs[g][...] = m_new

        qk(0)
        for g in range(n_groups):
            if g + 1 < n_groups:
                qk(g + 1)
            softmax_pv(g)

    def body(j, carry):
        block(j, slope2 * ((j - qi) * tq).astype(F32), False)
        return carry

    lax.fori_loop(0, qi, body, 0)
    block(qi, jnp.float32(0.0), True)

    lam = (jnp.exp(jnp.sum(lq1_ref[...] * lk1_ref[...], axis=-1, keepdims=True))
           - jnp.exp(jnp.sum(lq2_ref[...] * lk2_ref[...], axis=-1, keepdims=True)) + lam_init)
    ot = jnp.concatenate([a[0:hd, :] / a[hd:hd + 1, :] for a in acc_refs], axis=1)
    o = ot[:, 0:tq] - lam * ot[:, tq:2 * tq]
    o = o * lax.rsqrt(jnp.mean(o * o, axis=0, keepdims=True) + LN_EPS)
    o_ref[...] = (o * g_ref[...] * (1.0 - lam_init)).T.astype(o_ref.dtype)


def _attention(qt, k, vt, lq1, lk1, lq2, lk2, subln_g, lam_init):
    nh, nblk, hd, tq = qt.shape
    vrows = vt.shape[2]
    s = k.shape[0]
    assert (2 * tq) % Q_GROUP == 0 and hd + 9 <= MXU_DIM
    n_groups = (2 * tq) // Q_GROUP
    slopes = jnp.exp2(-8.0 * (jnp.arange(nh, dtype=F32) + 1.0) / nh) * LOG2E
    vec = lambda a: a.reshape(1, -1).astype(F32)
    small = lambda n: pl.BlockSpec((1, n), lambda h, i, *_: (0, 0))
    kern = functools.partial(_attn_kernel, tq=tq, lam_init=lam_init)
    return pl.pallas_call(
        kern,
        grid_spec=pltpu.PrefetchScalarGridSpec(
            num_scalar_prefetch=1,
            grid=(nh, nblk),
            in_specs=[pl.BlockSpec((None, None, hd, tq), lambda h, i, *_: (h, i, 0, 0)),
                      pl.BlockSpec((s, hd), lambda h, i, *_: (0, h)),
                      pl.BlockSpec((None, nblk, vrows, tq), lambda h, i, *_: (h, 0, 0, 0)),
                      small(ATT_QK_DIM), small(ATT_QK_DIM), small(ATT_QK_DIM), small(ATT_QK_DIM),
                      pl.BlockSpec((hd, 1), lambda h, i, *_: (0, 0))],
            out_specs=pl.BlockSpec((tq, hd), lambda h, i, *_: (i, h)),
            scratch_shapes=[pltpu.VMEM((2 * hd, 2 * tq), BF16),
                            pltpu.VMEM((tq, hd), BF16),
                            pltpu.VMEM((tq, 2 * tq), F32)]
            + [pltpu.VMEM((1, Q_GROUP), F32)] * n_groups
            + [pltpu.VMEM((vrows, Q_GROUP), F32)] * n_groups),
        out_shape=jax.ShapeDtypeStruct((s, nh * hd), BF16),
        compiler_params=_params(2),
        name="diff_attention",
    )(slopes, qt, k, vt, vec(lq1), vec(lk1), vec(lq2), vec(lk2),
      subln_g.reshape(hd, 1).astype(F32))


def _seqmix_kernel(u_ref, lcw_ref, lcb_ref, wa_ref, ba_ref, wx_ref, bx_ref, lam_ref,
                   ccw_ref, ccb_ref, cg_ref, cb_ref, o_ref,
                   xbuf_ref, cbuf_ref, a_ref, b_ref, hs_ref, hc_ref, *, tt, w):
    i = pl.program_id(0)

    @pl.when(i == 0)
    def _init():
        xbuf_ref[0:HALO, :] = jnp.zeros((HALO, w), F32)
        cbuf_ref[0:HALO, :] = jnp.zeros((HALO, w), F32)
        hc_ref[...] = jnp.zeros(hc_ref.shape, F32)

    xbuf_ref[HALO:HALO + tt, :] = u_ref[:, 0:w]
    xc = jnp.zeros((tt, w), F32)
    for j in range(LRU_CONV):
        off = HALO - (LRU_CONV - 1) + j
        xc = xc + lcw_ref[j:j + 1, :] * xbuf_ref[off:off + tt, :]
    xc = xc + lcb_ref[...]
    xcb = xc.astype(BF16)
    gate_a = _sigmoid(jnp.dot(xcb, wa_ref[...], preferred_element_type=F32) + ba_ref[...])
    gate_x = _sigmoid(jnp.dot(xcb, wx_ref[...], preferred_element_type=F32) + bx_ref[...])
    nl = -lam_ref[...]
    softplus = jnp.maximum(nl, 0.0) + jnp.log(1.0 + jnp.exp(-jnp.abs(nl)))
    log_a = -LRU_C * gate_a * softplus
    a_ref[...] = jnp.exp(log_a)
    b_ref[...] = jnp.sqrt(1.0 - jnp.exp(2.0 * log_a)) * gate_x * xc

    def step(t, hprev):
        hnew = a_ref[pl.ds(t, 1), :] * hprev + b_ref[pl.ds(t, 1), :]
        hs_ref[pl.ds(t, 1), :] = hnew
        return hnew

    hc_ref[...] = lax.fori_loop(0, tt, step, hc_ref[...], unroll=8)
    o_ref[:, 0:w] = (hs_ref[...] * jax.nn.gelu(u_ref[:, w:2 * w], approximate=True)).astype(o_ref.dtype)
    xbuf_ref[0:HALO, :] = xbuf_ref[tt:tt + HALO, :]

    cbuf_ref[HALO:HALO + tt, :] = u_ref[:, 2 * w:3 * w] * _sigmoid(u_ref[:, 3 * w:4 * w])
    y = jnp.zeros((tt, w), F32)
    for j in range(CONV_KERNEL):
        off = HALO - (CONV_KERNEL - 1) + j
        y = y + ccw_ref[j:j + 1, :] * cbuf_ref[off:off + tt, :]
    y = _layer_norm(y + ccb_ref[...], cg_ref[...], cb_ref[...])
    o_ref[:, w:2 * w] = (y * _sigmoid(y)).astype(o_ref.dtype)
    cbuf_ref[0:HALO, :] = cbuf_ref[tt:tt + HALO, :]


def _block_diag(wb):
    nb, bd, _ = wb.shape
    eye = jnp.eye(nb, dtype=jnp.bool_)
    return jnp.where(eye[:, None, :, None], wb[:, :, None, :], 0.0).reshape(nb * bd, nb * bd)


def _seqmix(rest, lcw, lcb, wa, ba, wx, bx, lam, ccw, ccb, cg, cb, tt):
    s, n = rest.shape
    w = n // 4
    row = lambda a: a.reshape(1, w).astype(F32)
    full = lambda r, c: pl.BlockSpec((r, c), lambda i: (0, 0))
    kern = functools.partial(_seqmix_kernel, tt=tt, w=w)
    return pl.pallas_call(
        kern,
        grid=(s // tt,),
        in_specs=[pl.BlockSpec((tt, n), lambda i: (i, 0)),
                  full(LRU_CONV, w), full(1, w), full(w, w), full(1, w), full(w, w), full(1, w),
                  full(1, w), full(CONV_KERNEL, w), full(1, w), full(1, w), full(1, w)],
        out_specs=pl.BlockSpec((tt, 2 * w), lambda i: (i, 0)),
        out_shape=jax.ShapeDtypeStruct((s, 2 * w), BF16),
        scratch_shapes=[pltpu.VMEM((tt + HALO, w), F32), pltpu.VMEM((tt + HALO, w), F32),
                        pltpu.VMEM((tt, w), F32), pltpu.VMEM((tt, w), F32), pltpu.VMEM((tt, w), F32),
                        pltpu.VMEM((1, w), F32)],
        compiler_params=_params(),
        name="seqmix",
    )(rest, lcw, row(lcb), _block_diag(wa).astype(BF16), row(ba), _block_diag(wx).astype(BF16),
      row(bx), row(lam), ccw, row(ccb), row(cg), row(cb))


def _outproj_kernel(att_ref, rc_ref, h_ref, wo_ref, g_ref, b_ref, rw_ref, rb_ref,
                    h1_ref, h1c_ref, idx_ref, gate_ref, rank_ref, cnt_ref, tri_ref, carry_ref,
                    *, tm, n_att):
    i = pl.program_id(0)
    ne = N_EXPERTS

    @pl.when(i == 0)
    def _init():
        r = lax.broadcasted_iota(jnp.int32, (tm, tm), 0)
        c = lax.broadcasted_iota(jnp.int32, (tm, tm), 1)
        tri_ref[...] = jnp.where(c < r, 1.0, 0.0).astype(BF16)
        carry_ref[...] = jnp.zeros(carry_ref.shape, F32)

    mix = (jnp.dot(att_ref[...], wo_ref[0:n_att, :], preferred_element_type=F32)
           + jnp.dot(rc_ref[...], wo_ref[n_att:, :], preferred_element_type=F32))
    h1 = _layer_norm(RESID_ALPHA * h_ref[...] + mix, g_ref[...], b_ref[...])
    h1_ref[...] = h1
    _store_rows(h1c_ref, h1)

    logits = jnp.dot(h1, rw_ref[...], preferred_element_type=F32,
                     precision=lax.Precision.HIGHEST) + rb_ref[...]
    lane = lax.broadcasted_iota(jnp.int32, (tm, ne), 1).astype(F32)
    onehot = jnp.zeros((tm, ne), F32)
    vals, sels = [], []
    for _ in range(TOP_K):
        mx = jnp.max(logits, axis=-1, keepdims=True)
        sel = jnp.min(jnp.where(logits == mx, lane, float(ne)), axis=-1, keepdims=True)
        hit = lane == sel
        onehot = onehot + jnp.where(hit, 1.0, 0.0)
        logits = jnp.where(hit, -jnp.inf, logits)
        vals.append(mx)
        sels.append(sel)
    ex = [jnp.exp(v - vals[0]) for v in vals]
    den = ex[0] + ex[1] + ex[2] + ex[3]
    before = jnp.dot(tri_ref[...], onehot.astype(BF16), preferred_element_type=F32) + carry_ref[...]
    for k in range(TOP_K):
        idx_ref[:, k:k + 1] = sels[k].astype(jnp.int32)
        gate_ref[:, k:k + 1] = ex[k] / den
        rank_ref[:, k:k + 1] = jnp.sum(jnp.where(lane == sels[k], before, 0.0), axis=-1,
                                       keepdims=True).astype(jnp.int32)
    carry_ref[...] = carry_ref[...] + jnp.sum(onehot, axis=0, keepdims=True)
    cnt_ref[...] = carry_ref[...].astype(jnp.int32)


def _outproj_router(att, rc, h, wo_bf16, g, b, rw, rb, tm):
    s, d = h.shape
    n_att = att.shape[1]
    ne = N_EXPERTS
    row = lambda a: a.reshape(1, -1).astype(F32)
    full = lambda r, c: pl.BlockSpec((r, c), lambda i: (0, 0))
    tile = lambda c: pl.BlockSpec((tm, c), lambda i: (i, 0))
    kern = functools.partial(_outproj_kernel, tm=tm, n_att=n_att)
    return pl.pallas_call(
        kern,
        grid=(s // tm,),
        in_specs=[tile(n_att), tile(rc.shape[1]), tile(d), full(wo_bf16.shape[0], d),
                  full(1, d), full(1, d), full(d, ne), full(1, ne)],
        out_specs=[tile(d), pl.BlockSpec((tm * (d // LANES), LANES), lambda i: (i, 0)),
                   tile(TOP_K), tile(TOP_K), tile(TOP_K), full(1, ne)],
        out_shape=[jax.ShapeDtypeStruct((s, d), F32),
                   jax.ShapeDtypeStruct((s * (d // LANES), LANES), F32),
                   jax.ShapeDtypeStruct((s, TOP_K), jnp.int32),
                   jax.ShapeDtypeStruct((s, TOP_K), F32),
                   jax.ShapeDtypeStruct((s, TOP_K), jnp.int32),
                   jax.ShapeDtypeStruct((1, ne), jnp.int32)],
        scratch_shapes=[pltpu.VMEM((tm, tm), BF16), pltpu.VMEM((1, ne), F32)],
        compiler_params=_params(),
        name="outproj_router",
    )(att, rc, h, wo_bf16, row(g), row(b), rw, row(rb))


def _row_copy(src_ref, src_row, dst_ref, dst_row, sem, nc):
    src = src_ref.at[pl.ds(pl.multiple_of(src_row * nc, nc), nc)]
    dst = dst_ref.at[pl.ds(pl.multiple_of(dst_row * nc, nc), nc)]
    return pltpu.make_async_copy(src, dst, sem)


def _dispatch_kernel(pstart_ref, pend_ref, idx_ref, rank_ref, h_ref, xd_ref, zero_ref, sem, zsem,
                     *, tt, tm, nc, n_blocks):
    i = pl.program_id(0)

    def zero_copy(e):
        start = pl.multiple_of((pend_ref[e] - tm) * nc, tm * nc)
        return pltpu.make_async_copy(zero_ref, xd_ref.at[pl.ds(start, tm * nc)], zsem)

    def tail_copy(b):
        start = pl.multiple_of(b * (tm * nc), tm * nc)
        return pltpu.make_async_copy(zero_ref, xd_ref.at[pl.ds(start, tm * nc)], zsem)

    @pl.when(i == 0)
    def _clear_padding():
        zero_ref[...] = jnp.zeros(zero_ref.shape, zero_ref.dtype)
        n_used = pend_ref[N_EXPERTS - 1] // tm

        def start_tail(b, carry):
            tail_copy(b).start()
            return carry

        def wait_tail(b, carry):
            tail_copy(b).wait()
            return carry

        lax.fori_loop(n_used, n_blocks, start_tail, 0)
        lax.fori_loop(n_used, n_blocks, wait_tail, 0)
        for e in range(N_EXPERTS):
            @pl.when(pend_ref[e] > pstart_ref[e])
            def _():
                zero_copy(e).start()
        for e in range(N_EXPERTS):
            @pl.when(pend_ref[e] > pstart_ref[e])
            def _():
                zero_copy(e).wait()

    def issue(t, carry):
        for k in range(TOP_K):
            a = t * TOP_K + k
            dest = pstart_ref[idx_ref[a]] + rank_ref[a]
            _row_copy(h_ref, t, xd_ref, dest, sem, nc).start()
        return carry

    lax.fori_loop(0, tt, issue, 0)

    for k in range(TOP_K):
        pltpu.make_async_copy(h_ref, xd_ref.at[pl.ds(0, tt * nc)], sem).wait()


def _dispatch(h1c, idx_flat, rank_flat, pstart, pend, rows, s, tt, tm):
    nc = h1c.shape[0] // s
    smem = lambda n: pl.BlockSpec((n,), lambda i, *_: (i,), memory_space=pltpu.SMEM)
    kern = functools.partial(_dispatch_kernel, tt=tt, tm=tm, nc=nc, n_blocks=rows // tm)
    return pl.pallas_call(
        kern,
        grid_spec=pltpu.PrefetchScalarGridSpec(
            num_scalar_prefetch=2,
            grid=(s // tt,),
            in_specs=[smem(tt * TOP_K), smem(tt * TOP_K),
                      pl.BlockSpec((tt * nc, LANES), lambda i, *_: (i, 0))],
            out_specs=pl.BlockSpec(memory_space=pl.ANY),
            scratch_shapes=[pltpu.VMEM((tm * nc, LANES), F32), pltpu.SemaphoreType.DMA(()),
                            pltpu.SemaphoreType.DMA(())]),
        out_shape=jax.ShapeDtypeStruct((rows * nc, LANES), F32),
        compiler_params=_params(),
        name="moe_dispatch",
    )(pstart, pend, idx_flat, rank_flat, h1c)


def _expert_kernel(bexp_ref, nused_ref, x_ref, w1_ref, b1_ref, w2_ref, b2_ref, y_ref,
                   w1b_ref, w2b_ref, *, f, chunk, tm):
    b = pl.program_id(0)
    last = nused_ref[0] - 1
    e = bexp_ref[jnp.minimum(b, last)]
    e_prev = bexp_ref[jnp.maximum(jnp.minimum(b, last) - 1, 0)]

    @pl.when((b == 0) | (e != e_prev))
    def _cast_weights():
        def cast(c, carry):
            r = pl.multiple_of(c * chunk, chunk)
            w1b_ref[pl.ds(r, chunk), :] = w1_ref[pl.ds(r, chunk), :].astype(BF16)
            w2b_ref[pl.ds(r, chunk), :] = w2_ref[pl.ds(r, chunk), :].astype(BF16)
            return carry
        lax.fori_loop(0, w1_ref.shape[0] // chunk, cast, 0)

    @pl.when(b > last)
    def _unused_block():
        y_ref[...] = jnp.zeros(y_ref.shape, y_ref.dtype)

    @pl.when(b <= last)
    def _mlp():
        x = _load_rows(x_ref, tm, w1_ref.shape[0]).astype(BF16)
        hdn = jnp.dot(x, w1b_ref[...], preferred_element_type=F32) + b1_ref[...]
        glu = jnp.minimum(hdn[:, 0:f], SWIGLU_LIMIT)
        lin = jnp.clip(hdn[:, f:2 * f], -SWIGLU_LIMIT, SWIGLU_LIMIT)
        act = glu * _sigmoid(SWIGLU_ALPHA * glu) * (lin + 1.0)
        _store_rows(y_ref, jnp.dot(act.astype(BF16), w2b_ref[...], preferred_element_type=F32)
                    + b2_ref[...])


def _experts(x_disp, bexp, nused, w1, b1, w2, b2, layer, tm):
    d = w1.shape[2]
    f = w2.shape[2]
    nc = d // LANES
    rows = x_disp.shape[0] // nc
    assert f == d, "the weight-cast loop assumes d_expert == d_model"

    def blk(b, bexp_ref, nused_ref):
        return jnp.minimum(b, nused_ref[0] - 1)

    def wmap(b, bexp_ref, nused_ref):
        return (layer, bexp_ref[blk(b, bexp_ref, nused_ref)], 0, 0)

    def bmap(b, bexp_ref, nused_ref):
        return (layer, bexp_ref[blk(b, bexp_ref, nused_ref)], 0, 0)

    kern = functools.partial(_expert_kernel, f=f, chunk=128, tm=tm)
    return pl.pallas_call(
        kern,
        grid_spec=pltpu.PrefetchScalarGridSpec(
            num_scalar_prefetch=2,
            grid=(rows // tm,),
            in_specs=[pl.BlockSpec((tm * nc, LANES), lambda b, be, nu: (b, 0)),
                      pl.BlockSpec((None, None, d, 2 * f), wmap),
                      pl.BlockSpec((None, None, 1, 2 * f), bmap),
                      pl.BlockSpec((None, None, f, d), wmap),
                      pl.BlockSpec((None, None, 1, d), bmap)],
            out_specs=pl.BlockSpec((tm * nc, LANES), lambda b, be, nu: (b, 0)),
            scratch_shapes=[pltpu.VMEM((d, 2 * f), BF16), pltpu.VMEM((f, d), BF16)]),
        out_shape=jax.ShapeDtypeStruct((rows * nc, LANES), F32),
        compiler_params=_params(),
        name="moe_experts",
    )(bexp, nused, x_disp, w1, b1.reshape(b1.shape[0], b1.shape[1], 1, -1), w2,
      b2.reshape(b2.shape[0], b2.shape[1], 1, -1))


def _combine_kernel(pstart_ref, idx_ref, rank_ref, h_ref, gate_ref, g_ref, b_ref, yd_ref, o_ref,
                    ybuf_ref, sem, *, tt, nc):
    def issue(t, carry):
        for k in range(TOP_K):
            a = t * TOP_K + k
            src = pstart_ref[idx_ref[a]] + rank_ref[a]
            _row_copy(yd_ref, src, ybuf_ref.at[k], t, sem, nc).start()
        return carry

    lax.fori_loop(0, tt, issue, 0)

    for k in range(TOP_K):
        pltpu.make_async_copy(yd_ref.at[pl.ds(0, tt * nc)], ybuf_ref.at[k], sem).wait()

    d = h_ref.shape[1]
    ff = gate_ref[:, 0:1] * _load_rows(ybuf_ref, tt, d, (0,))
    for k in range(1, TOP_K):
        ff = ff + gate_ref[:, k:k + 1] * _load_rows(ybuf_ref, tt, d, (k,))
    o_ref[...] = _layer_norm(RESID_ALPHA * h_ref[...] + ff, g_ref[...], b_ref[...])


def _combine(y_disp, h1, gates, idx_flat, rank_flat, pstart, g, b, tt):
    s, d = h1.shape
    nc = d // LANES
    smem = lambda n: pl.BlockSpec((n,), lambda i, *_: (i,), memory_space=pltpu.SMEM)
    row = lambda a: a.reshape(1, -1).astype(F32)
    kern = functools.partial(_combine_kernel, tt=tt, nc=nc)
    return pl.pallas_call(
        kern,
        grid_spec=pltpu.PrefetchScalarGridSpec(
            num_scalar_prefetch=1,
            grid=(s // tt,),
            in_specs=[smem(tt * TOP_K), smem(tt * TOP_K),
                      pl.BlockSpec((tt, d), lambda i, *_: (i, 0)),
                      pl.BlockSpec((tt, TOP_K), lambda i, *_: (i, 0)),
                      pl.BlockSpec((1, d), lambda i, *_: (0, 0)),
                      pl.BlockSpec((1, d), lambda i, *_: (0, 0)),
                      pl.BlockSpec(memory_space=pl.ANY)],
            out_specs=pl.BlockSpec((tt, d), lambda i, *_: (i, 0)),
            scratch_shapes=[pltpu.VMEM((TOP_K, tt * nc, LANES), F32), pltpu.SemaphoreType.DMA(())]),
        out_shape=jax.ShapeDtypeStruct((s, d), F32),
        compiler_params=_params(),
        name="moe_combine",
    )(pstart, idx_flat, rank_flat, h1, gates, row(g), row(b), y_disp)


def _tiles(s):
    t = lambda want: math.gcd(s, want)
    return dict(dense=t(512), attn=t(512), seq=t(512), route=t(512), expert=t(256), combine=t(256))


def kernel(x, ln_in_g, ln_in_b, w_in, lam_q1, lam_k1, lam_q2, lam_k2, subln_g, lru_conv_w, lru_conv_b, lru_wa, lru_ba, lru_wx, lru_bx, lru_lambda, cf_conv_w, cf_conv_b, cf_ln_g, cf_ln_b, w_out, ln1_g, ln1_b, router_w, router_b, moe_w1, moe_b1, moe_w2, moe_b2, ln2_g, ln2_b):
    bsz, s, d = x.shape
    assert bsz == 1
    tl = _tiles(s)
    tm = tl["expert"]
    n_blocks = (s * TOP_K) // tm + N_EXPERTS
    rows = n_blocks * tm

    h = _input_ln(x.reshape(s, d), ln_in_g, ln_in_b, tl["dense"])
    for l in range(DEPTH):
        lam_init = 0.8 - 0.6 * math.exp(-0.3 * l)
        qt, k, vt, rest = _inproj(h, w_in[l].astype(BF16), tl["attn"])
        att = _attention(qt, k, vt, lam_q1[l], lam_k1[l], lam_q2[l], lam_k2[l], subln_g[l],
                         lam_init)
        rc = _seqmix(rest, lru_conv_w[l], lru_conv_b[l], lru_wa[l], lru_ba[l], lru_wx[l],
                     lru_bx[l], lru_lambda[l], cf_conv_w[l], cf_conv_b[l], cf_ln_g[l], cf_ln_b[l],
                     tl["seq"])
        h1, h1c, idx, gates, rank, counts = _outproj_router(
            att, rc, h, w_out[l].astype(BF16), ln1_g[l], ln1_b[l], router_w[l], router_b[l],
            tl["route"])
        counts = counts.reshape(N_EXPERTS)
        padded = (counts + tm - 1) // tm * tm
        e_ids = jnp.arange(N_EXPERTS, dtype=jnp.int32)
        pend = jnp.sum(jnp.where(e_ids[None, :] <= e_ids[:, None], padded[None, :], 0),
                       axis=1).astype(jnp.int32)
        pstart = pend - padded
        block_row = jnp.arange(n_blocks, dtype=jnp.int32) * tm
        bexp = jnp.minimum(jnp.sum((pend[None, :] <= block_row[:, None]).astype(jnp.int32), axis=1),
                           N_EXPERTS - 1).astype(jnp.int32)
        nused = (pend[-1:] // tm).astype(jnp.int32)
        idx_flat = idx.reshape(s * TOP_K)
        rank_flat = rank.reshape(s * TOP_K)
        x_disp = _dispatch(h1c, idx_flat, rank_flat, pstart, pend, rows, s, tl["route"], tm)
        y_disp = _experts(x_disp, bexp, nused, moe_w1, moe_b1, moe_w2, moe_b2, l, tm)
        h = _combine(y_disp, h1, gates, idx_flat, rank_flat, pstart, ln2_g[l], ln2_b[l],
                     tl["combine"])
    return h.reshape(bsz, s, d)
```

```python
import functools
import math

import jax
import jax.numpy as jnp
from jax import lax
from jax.experimental import pallas as pl
from jax.experimental.pallas import tpu as pltpu

F32 = jnp.float32
BF16 = jnp.bfloat16

DEPTH = 2
ATT_HEADS = 4
ATT_QK_DIM = 64
ATT_V_DIM = 128
LRU_BLOCKS = 4
LRU_CONV = 4
LRU_C = 8.0
CONV_KERNEL = 31
N_EXPERTS = 32
TOP_K = 4
SWIGLU_LIMIT = 7.0
SWIGLU_ALPHA = 1.702
LN_EPS = 1e-5
RESID_ALPHA = (2.0 * DEPTH) ** 0.25

VMEM_LIMIT_BYTES = 56 * 1024 * 1024
HALO = 32
LANES = 128
SUBLANES = 8
MXU_DIM = 256
LOG2E = math.log2(math.e)
V_PAD_ROWS = 16
Q_GROUP = 256
QK_LOOKAHEAD = 4


def _load_rows(ref, n_rows, d, lead=()):
    nc = d // LANES
    return jnp.concatenate([ref[lead + (pl.ds(c, n_rows, stride=nc), slice(None))]
                            for c in range(nc)], axis=-1)


def _store_rows(ref, val):
    n_rows, d = val.shape
    nc = d // LANES
    for c in range(nc):
        ref[pl.ds(c, n_rows, stride=nc), :] = val[:, c * LANES:(c + 1) * LANES]


def _params(n_axes=1):
    return pltpu.CompilerParams(dimension_semantics=("arbitrary",) * n_axes,
                                vmem_limit_bytes=VMEM_LIMIT_BYTES)


def _layer_norm(x, g, b):
    mu = jnp.mean(x, axis=-1, keepdims=True)
    xc = x - mu
    var = jnp.mean(xc * xc, axis=-1, keepdims=True)
    return xc * lax.rsqrt(var + LN_EPS) * g + b


def _sigmoid(x):
    return 1.0 / (1.0 + jnp.exp(-x))


def _ln_kernel(x_ref, g_ref, b_ref, o_ref):
    o_ref[...] = _layer_norm(x_ref[...], g_ref[...], b_ref[...])


def _input_ln(x, g, b, tm):
    s, d = x.shape
    return pl.pallas_call(
        _ln_kernel,
        grid=(s // tm,),
        in_specs=[pl.BlockSpec((tm, d), lambda i: (i, 0)),
                  pl.BlockSpec((1, d), lambda i: (0, 0)),
                  pl.BlockSpec((1, d), lambda i: (0, 0))],
        out_specs=pl.BlockSpec((tm, d), lambda i: (i, 0)),
        out_shape=jax.ShapeDtypeStruct((s, d), F32),
        compiler_params=_params(),
        name="input_ln",
    )(x, g.reshape(1, d), b.reshape(1, d))


def _inproj_kernel(h_ref, w_ref, qt_ref, k_ref, vt_ref, rest_ref, *, scale):
    nh, hd = ATT_HEADS, ATT_V_DIM
    w_att = nh * hd
    tm = h_ref.shape[0]
    hb = h_ref[...].astype(BF16)
    q = jnp.dot(hb, w_ref[:, 0:w_att], preferred_element_type=F32) * scale
    k_ref[...] = jnp.dot(hb, w_ref[:, w_att:2 * w_att], preferred_element_type=F32).astype(BF16)
    v = jnp.dot(hb, w_ref[:, 2 * w_att:3 * w_att], preferred_element_type=F32)
    ones = jnp.ones((V_PAD_ROWS, tm), BF16)
    for h in range(nh):
        qt_ref[h] = q[:, h * hd:(h + 1) * hd].T.astype(BF16)
        vt_ref[h, 0:hd, :] = v[:, h * hd:(h + 1) * hd].T.astype(BF16)
        vt_ref[h, hd:hd + V_PAD_ROWS, :] = ones
    rest_ref[...] = jnp.dot(hb, w_ref[:, 3 * w_att:], preferred_element_type=F32)


def _inproj(h, w_bf16, tm):
    s, d = h.shape
    n = w_bf16.shape[1]
    nh, hd = ATT_HEADS, ATT_V_DIM
    n_att = 3 * nh * hd
    kern = functools.partial(_inproj_kernel, scale=ATT_QK_DIM ** -0.5 * LOG2E)
    tposed = lambda r: pl.BlockSpec((nh, None, r, tm), lambda i: (0, i, 0, 0))
    return pl.pallas_call(
        kern,
        grid=(s // tm,),
        in_specs=[pl.BlockSpec((tm, d), lambda i: (i, 0)),
                  pl.BlockSpec((d, n), lambda i: (0, 0))],
        out_specs=[tposed(hd),
                   pl.BlockSpec((tm, nh * hd), lambda i: (i, 0)),
                   tposed(hd + V_PAD_ROWS),
                   pl.BlockSpec((tm, n - n_att), lambda i: (i, 0))],
        out_shape=[jax.ShapeDtypeStruct((nh, s // tm, hd, tm), BF16),
                   jax.ShapeDtypeStruct((s, nh * hd), BF16),
                   jax.ShapeDtypeStruct((nh, s // tm, hd + V_PAD_ROWS, tm), BF16),
                   jax.ShapeDtypeStruct((s, n - n_att), F32)],
        compiler_params=_params(),
        name="inproj",
    )(h, w_bf16)


def _split3(x):
    hi = x.astype(BF16)
    rem = x - hi.astype(F32)
    mid = rem.astype(BF16)
    lo = (rem - mid.astype(F32)).astype(BF16)
    return hi, mid, lo


def _attn_kernel(slopes_ref, qt_ref, k_ref, vt_ref, lq1_ref, lk1_ref, lq2_ref, lk2_ref, g_ref,
                 o_ref, qs_ref, kc_ref, mask_ref, *stat_refs, tq, lam_init):
    h = pl.program_id(0)
    qi = pl.program_id(1)
    slope2 = slopes_ref[h]
    dk, hd = ATT_QK_DIM, ATT_V_DIM
    n_groups = (2 * tq) // Q_GROUP
    m_refs, acc_refs = stat_refs[:n_groups], stat_refs[n_groups:]

    @pl.when(qi == 0)
    def _build_constants():
        c = lax.broadcasted_iota(jnp.int32, (tq, hd), 0)
        col = lax.broadcasted_iota(jnp.int32, (tq, hd), 1)
        c_lo = jnp.bitwise_and(c, MXU_DIM - 1)
        c_hi = c - c_lo
        kc = jnp.where(col < 3, c_hi, jnp.where(col < 6, c_lo, jnp.where(col < 9, 1, 0)))
        kc_ref[...] = kc.astype(F32).astype(BF16)
        r = lax.broadcasted_iota(jnp.int32, (hd, 2 * tq), 1)
        r = jnp.where(r >= tq, r - tq, r)
        row = lax.broadcasted_iota(jnp.int32, (hd, 2 * tq), 0)
        sl = jnp.full((hd, 2 * tq), slope2, F32)
        s_hi, s_mid, s_lo = _split3(sl)
        t_hi, t_mid, t_lo = _split3(-(sl * r.astype(F32)))
        aug = jnp.zeros((hd, 2 * tq), F32)
        for i, piece in enumerate((s_hi, s_mid, s_lo, s_hi, s_mid, s_lo, t_hi, t_mid, t_lo)):
            aug = jnp.where(row == i, piece.astype(F32), aug)
        qs_ref[hd:2 * hd, :] = aug.astype(BF16)
        ck = lax.broadcasted_iota(jnp.int32, (tq, 2 * tq), 0)
        rq = lax.broadcasted_iota(jnp.int32, (tq, 2 * tq), 1)
        rq = jnp.where(rq >= tq, rq - tq, rq)
        mask_ref[...] = jnp.where(ck <= rq, 0.0, -jnp.inf)

    qt = qt_ref[...]
    dim = lax.broadcasted_iota(jnp.int32, qt.shape, 0)
    zero_q = jnp.zeros_like(qt)
    qs_ref[0:hd, 0:tq] = jnp.where(dim < dk, qt, zero_q)
    qs_ref[0:hd, tq:2 * tq] = jnp.where(dim >= dk, qt, zero_q)
    for m_ref, acc_ref in zip(m_refs, acc_refs):
        m_ref[...] = jnp.full(m_ref.shape, -jnp.inf, F32)
        acc_ref[...] = jnp.zeros(acc_ref.shape, F32)

    def run_blocks(js, diagonal_last):
        kas, vts, offs = [], [], []
        for j in js:
            start = pl.multiple_of(j * tq, tq)
            kas.append(jnp.concatenate([k_ref[pl.ds(start, tq), :], kc_ref[...]], axis=1))
            vts.append(vt_ref[j])
            offs.append(slope2 * ((j - qi) * tq).astype(F32))
        items = [(b, g) for b in range(len(js)) for g in range(n_groups)]
        scores = {}

        def qk(item):
            b, g = item
            cols = slice(g * Q_GROUP, (g + 1) * Q_GROUP)
            st = jnp.dot(kas[b], qs_ref[:, cols], preferred_element_type=F32)
            if diagonal_last and b == len(js) - 1:
                st = st + mask_ref[:, cols]
            scores[item] = st

        def softmax_pv(item):
            b, g = item
            st = scores.pop(item)
            m_old = m_refs[g][...]
            m_new = jnp.maximum(m_old, jnp.max(st, axis=0, keepdims=True) + offs[b])
            p = jnp.exp2(st - (m_new - offs[b])).astype(BF16)
            alpha = jnp.exp2(m_old - m_new)
            acc_refs[g][...] = (alpha * acc_refs[g][...]
                                + jnp.dot(vts[b], p, preferred_element_type=F32))
            m_refs[g][...] = m_new

        for item in items[:QK_LOOKAHEAD]:
            qk(item)
        for i, item in enumerate(items):
            softmax_pv(item)
            if i + QK_LOOKAHEAD < len(items):
                qk(items[i + QK_LOOKAHEAD])

    def body(pair, carry):
        run_blocks([2 * pair, 2 * pair + 1], False)
        return carry

    lax.fori_loop(0, lax.shift_right_logical(qi, 1), body, 0)
    odd = jnp.bitwise_and(qi, 1) == 1

    @pl.when(odd)
    def _tail_pair():
        run_blocks([qi - 1, qi], True)

    @pl.when(jnp.logical_not(odd))
    def _tail_single():
        run_blocks([qi], True)

    lam = (jnp.exp(jnp.sum(lq1_ref[...] * lk1_ref[...], axis=-1, keepdims=True))
           - jnp.exp(jnp.sum(lq2_ref[...] * lk2_ref[...], axis=-1, keepdims=True)) + lam_init)
    ot = jnp.concatenate([a[0:hd, :] / a[hd:hd + 1, :] for a in acc_refs], axis=1)
    o = ot[:, 0:tq] - lam * ot[:, tq:2 * tq]
    o = o * lax.rsqrt(jnp.mean(o * o, axis=0, keepdims=True) + LN_EPS)
    o_ref[...] = (o * g_ref[...] * (1.0 - lam_init)).T.astype(o_ref.dtype)


def _attention(qt, k, vt, lq1, lk1, lq2, lk2, subln_g, lam_init):
    nh, nblk, hd, tq = qt.shape
    vrows = vt.shape[2]
    s = k.shape[0]
    assert (2 * tq) % Q_GROUP == 0 and hd + 9 <= MXU_DIM
    n_groups = (2 * tq) // Q_GROUP
    slopes = jnp.exp2(-8.0 * (jnp.arange(nh, dtype=F32) + 1.0) / nh) * LOG2E
    vec = lambda a: a.reshape(1, -1).astype(F32)
    small = lambda n: pl.BlockSpec((1, n), lambda h, i, *_: (0, 0))
    kern = functools.partial(_attn_kernel, tq=tq, lam_init=lam_init)
    return pl.pallas_call(
        kern,
        grid_spec=pltpu.PrefetchScalarGridSpec(
            num_scalar_prefetch=1,
            grid=(nh, nblk),
            in_specs=[pl.BlockSpec((None, None, hd, tq), lambda h, i, *_: (h, i, 0, 0)),
                      pl.BlockSpec((s, hd), lambda h, i, *_: (0, h)),
                      pl.BlockSpec((None, nblk, vrows, tq), lambda h, i, *_: (h, 0, 0, 0)),
                      small(ATT_QK_DIM), small(ATT_QK_DIM), small(ATT_QK_DIM), small(ATT_QK_DIM),
                      pl.BlockSpec((hd, 1), lambda h, i, *_: (0, 0))],
            out_specs=pl.BlockSpec((tq, hd), lambda h, i, *_: (i, h)),
            scratch_shapes=[pltpu.VMEM((2 * hd, 2 * tq), BF16),
                            pltpu.VMEM((tq, hd), BF16),
                            pltpu.VMEM((tq, 2 * tq), F32)]
            + [pltpu.VMEM((1, Q_GROUP), F32)] * n_groups
            + [pltpu.VMEM((vrows, Q_GROUP), F32)] * n_groups),
        out_shape=jax.ShapeDtypeStruct((s, nh * hd), BF16),
        compiler_params=_params(2),
        name="diff_attention",
    )(slopes, qt, k, vt, vec(lq1), vec(lk1), vec(lq2), vec(lk2),
      subln_g.reshape(hd, 1).astype(F32))


def _seqmix_kernel(u_ref, lcw_ref, lcb_ref, wa_ref, ba_ref, wx_ref, bx_ref, lam_ref,
                   ccw_ref, ccb_ref, cg_ref, cb_ref, o_ref,
                   xbuf_ref, cbuf_ref, shift_ref, a_ref, b_ref, hs_ref, hc_ref, *, tt, w):
    i = pl.program_id(0)

    @pl.when(i == 0)
    def _init():
        xbuf_ref[0:HALO, :] = jnp.zeros((HALO, w), F32)
        cbuf_ref[0:HALO, :] = jnp.zeros((HALO, w), F32)
        hc_ref[...] = jnp.zeros(hc_ref.shape, F32)

    xbuf_ref[HALO:HALO + tt, :] = u_ref[:, 0:w]
    xc = jnp.zeros((tt, w), F32)
    for j in range(LRU_CONV):
        off = HALO - (LRU_CONV - 1) + j
        xc = xc + lcw_ref[j:j + 1, :] * xbuf_ref[off:off + tt, :]
    xc = xc + lcb_ref[...]
    xcb = xc.astype(BF16)
    gate_a = _sigmoid(jnp.dot(xcb, wa_ref[...], preferred_element_type=F32) + ba_ref[...])
    gate_x = _sigmoid(jnp.dot(xcb, wx_ref[...], preferred_element_type=F32) + bx_ref[...])
    nl = -lam_ref[...]
    softplus = jnp.maximum(nl, 0.0) + jnp.log(1.0 + jnp.exp(-jnp.abs(nl)))
    log_a = -LRU_C * gate_a * softplus
    a_ref[...] = jnp.exp(log_a)
    b_ref[...] = jnp.sqrt(1.0 - jnp.exp(2.0 * log_a)) * gate_x * xc

    def step(t, hprev):
        hnew = a_ref[pl.ds(t, 1), :] * hprev + b_ref[pl.ds(t, 1), :]
        hs_ref[pl.ds(t, 1), :] = hnew
        return hnew

    hc_ref[...] = lax.fori_loop(0, tt, step, hc_ref[...], unroll=8)
    o_ref[:, 0:w] = (hs_ref[...] * jax.nn.gelu(u_ref[:, w:2 * w], approximate=True)).astype(o_ref.dtype)
    xbuf_ref[0:HALO, :] = xbuf_ref[tt:tt + HALO, :]

    cbuf_ref[HALO:HALO + tt, :] = u_ref[:, 2 * w:3 * w] * _sigmoid(u_ref[:, 3 * w:4 * w])
    span = shift_ref.shape[1]
    for p in range(1, SUBLANES):
        shift_ref[p - 1] = cbuf_ref[p:p + span, :]
    y = jnp.zeros((tt, w), F32)
    for j in range(CONV_KERNEL):
        off = HALO - (CONV_KERNEL - 1) + j
        p, base = off % SUBLANES, off - off % SUBLANES
        win = cbuf_ref[base:base + tt, :] if p == 0 else shift_ref[p - 1, base:base + tt, :]
        y = y + ccw_ref[j:j + 1, :] * win
    y = _layer_norm(y + ccb_ref[...], cg_ref[...], cb_ref[...])
    o_ref[:, w:2 * w] = (y * _sigmoid(y)).astype(o_ref.dtype)
    cbuf_ref[0:HALO, :] = cbuf_ref[tt:tt + HALO, :]


def _block_diag(wb):
    nb, bd, _ = wb.shape
    eye = jnp.eye(nb, dtype=jnp.bool_)
    return jnp.where(eye[:, None, :, None], wb[:, :, None, :], 0.0).reshape(nb * bd, nb * bd)


def _seqmix(rest, lcw, lcb, wa, ba, wx, bx, lam, ccw, ccb, cg, cb, tt):
    s, n = rest.shape
    w = n // 4
    row = lambda a: a.reshape(1, w).astype(F32)
    full = lambda r, c: pl.BlockSpec((r, c), lambda i: (0, 0))
    kern = functools.partial(_seqmix_kernel, tt=tt, w=w)
    return pl.pallas_call(
        kern,
        grid=(s // tt,),
        in_specs=[pl.BlockSpec((tt, n), lambda i: (i, 0)),
                  full(LRU_CONV, w), full(1, w), full(w, w), full(1, w), full(w, w), full(1, w),
                  full(1, w), full(CONV_KERNEL, w), full(1, w), full(1, w), full(1, w)],
        out_specs=pl.BlockSpec((tt, 2 * w), lambda i: (i, 0)),
        out_shape=jax.ShapeDtypeStruct((s, 2 * w), BF16),
        scratch_shapes=[pltpu.VMEM((tt + HALO, w), F32), pltpu.VMEM((tt + HALO, w), F32),
                        pltpu.VMEM((SUBLANES - 1, tt + HALO - SUBLANES, w), F32),
                        pltpu.VMEM((tt, w), F32), pltpu.VMEM((tt, w), F32), pltpu.VMEM((tt, w), F32),
                        pltpu.VMEM((1, w), F32)],
        compiler_params=_params(),
        name="seqmix",
    )(rest, lcw, row(lcb), _block_diag(wa).astype(BF16), row(ba), _block_diag(wx).astype(BF16),
      row(bx), row(lam), ccw, row(ccb), row(cg), row(cb))


def _outproj_kernel(att_ref, rc_ref, h_ref, wo_ref, g_ref, b_ref, rwh_ref, rwl_ref, rb_ref,
                    h1_ref, h1c_ref, idx_ref, gate_ref, rank_ref, cnt_ref, tri_ref, carry_ref,
                    *, tm, n_att):
    i = pl.program_id(0)
    ne = N_EXPERTS

    @pl.when(i == 0)
    def _init():
        r = lax.broadcasted_iota(jnp.int32, (tm, tm), 0)
        c = lax.broadcasted_iota(jnp.int32, (tm, tm), 1)
        tri_ref[...] = jnp.where(c < r, 1.0, 0.0).astype(BF16)
        carry_ref[...] = jnp.zeros(carry_ref.shape, F32)

    mix = (jnp.dot(att_ref[...], wo_ref[0:n_att, :], preferred_element_type=F32)
           + jnp.dot(rc_ref[...], wo_ref[n_att:, :], preferred_element_type=F32))
    h1 = _layer_norm(RESID_ALPHA * h_ref[...] + mix, g_ref[...], b_ref[...])
    h1_ref[...] = h1
    _store_rows(h1c_ref, h1)

    h1_hi = h1.astype(BF16)
    h1_lo = (h1 - h1_hi.astype(F32)).astype(BF16)
    logits = (jnp.dot(h1_hi, rwh_ref[...], preferred_element_type=F32)
              + jnp.dot(h1_hi, rwl_ref[...], preferred_element_type=F32)
              + jnp.dot(h1_lo, rwh_ref[...], preferred_element_type=F32)) + rb_ref[...]
    lane = lax.broadcasted_iota(jnp.int32, (tm, ne), 1).astype(F32)
    onehot = jnp.zeros((tm, ne), F32)
    vals, sels = [], []
    for _ in range(TOP_K):
        mx = jnp.max(logits, axis=-1, keepdims=True)
        sel = jnp.min(jnp.where(logits == mx, lane, float(ne)), axis=-1, keepdims=True)
        hit = lane == sel
        onehot = onehot + jnp.where(hit, 1.0, 0.0)
        logits = jnp.where(hit, -jnp.inf, logits)
        vals.append(mx)
        sels.append(sel)
    ex = [jnp.exp(v - vals[0]) for v in vals]
    den = ex[0] + ex[1] + ex[2] + ex[3]
    before = jnp.dot(tri_ref[...], onehot.astype(BF16), preferred_element_type=F32) + carry_ref[...]
    for k in range(TOP_K):
        idx_ref[:, k:k + 1] = sels[k].astype(jnp.int32)
        gate_ref[:, k:k + 1] = ex[k] / den
        rank_ref[:, k:k + 1] = jnp.sum(jnp.where(lane == sels[k], before, 0.0), axis=-1,
                                       keepdims=True).astype(jnp.int32)
    carry_ref[...] = carry_ref[...] + jnp.sum(onehot, axis=0, keepdims=True)
    cnt_ref[...] = carry_ref[...].astype(jnp.int32)


def _outproj_router(att, rc, h, wo_bf16, g, b, rw, rb, tm):
    s, d = h.shape
    n_att = att.shape[1]
    ne = N_EXPERTS
    row = lambda a: a.reshape(1, -1).astype(F32)
    full = lambda r, c: pl.BlockSpec((r, c), lambda i: (0, 0))
    tile = lambda c: pl.BlockSpec((tm, c), lambda i: (i, 0))
    kern = functools.partial(_outproj_kernel, tm=tm, n_att=n_att)
    rw_hi = rw.astype(BF16)
    rw_lo = (rw - rw_hi.astype(F32)).astype(BF16)
    return pl.pallas_call(
        kern,
        grid=(s // tm,),
        in_specs=[tile(n_att), tile(rc.shape[1]), tile(d), full(wo_bf16.shape[0], d),
                  full(1, d), full(1, d), full(d, ne), full(d, ne), full(1, ne)],
        out_specs=[tile(d), pl.BlockSpec((tm * (d // LANES), LANES), lambda i: (i, 0)),
                   tile(TOP_K), tile(TOP_K), tile(TOP_K), full(1, ne)],
        out_shape=[jax.ShapeDtypeStruct((s, d), F32),
                   jax.ShapeDtypeStruct((s * (d // LANES), LANES), F32),
                   jax.ShapeDtypeStruct((s, TOP_K), jnp.int32),
                   jax.ShapeDtypeStruct((s, TOP_K), F32),
                   jax.ShapeDtypeStruct((s, TOP_K), jnp.int32),
                   jax.ShapeDtypeStruct((1, ne), jnp.int32)],
        scratch_shapes=[pltpu.VMEM((tm, tm), BF16), pltpu.VMEM((1, ne), F32)],
        compiler_params=_params(),
        name="outproj_router",
    )(att, rc, h, wo_bf16, row(g), row(b), rw_hi, rw_lo, row(rb))


def _row_copy(src_ref, src_row, dst_ref, dst_row, sem, nc):
    src = src_ref.at[pl.ds(pl.multiple_of(src_row * nc, nc), nc)]
    dst = dst_ref.at[pl.ds(pl.multiple_of(dst_row * nc, nc), nc)]
    return pltpu.make_async_copy(src, dst, sem)


def _dest_kernel(idx_ref, rank_ref, pstart_ref, dest_ref):
    tm, ne = idx_ref.shape[0], pstart_ref.shape[1]
    lane = lax.broadcasted_iota(jnp.int32, (tm, ne), 1)
    pstart = pstart_ref[...].astype(F32)
    for k in range(TOP_K):
        hit = lane == idx_ref[:, k:k + 1]
        start = jnp.sum(jnp.where(hit, pstart, 0.0), axis=-1, keepdims=True)
        dest_ref[:, k:k + 1] = start.astype(jnp.int32) + rank_ref[:, k:k + 1]


def _dest_rows(idx, rank, pstart, tm):
    s = idx.shape[0]
    ne = pstart.shape[0]
    tile = pl.BlockSpec((tm, TOP_K), lambda i: (i, 0))
    return pl.pallas_call(
        _dest_kernel,
        grid=(s // tm,),
        in_specs=[tile, tile, pl.BlockSpec((1, ne), lambda i: (0, 0))],
        out_specs=tile,
        out_shape=jax.ShapeDtypeStruct((s, TOP_K), jnp.int32),
        compiler_params=_params(),
        name="moe_dest_rows",
    )(idx, rank, pstart.reshape(1, ne))


def _dispatch_kernel(pstart_ref, pend_ref, dest_ref, h_ref, xd_ref, zero_ref, sem, zsem,
                     *, tt, tm, nc, n_blocks):
    i = pl.program_id(0)

    def zero_copy(e):
        start = pl.multiple_of((pend_ref[e] - tm) * nc, tm * nc)
        return pltpu.make_async_copy(zero_ref, xd_ref.at[pl.ds(start, tm * nc)], zsem)

    def tail_copy(b):
        start = pl.multiple_of(b * (tm * nc), tm * nc)
        return pltpu.make_async_copy(zero_ref, xd_ref.at[pl.ds(start, tm * nc)], zsem)

    @pl.when(i == 0)
    def _clear_padding():
        zero_ref[...] = jnp.zeros(zero_ref.shape, zero_ref.dtype)
        n_used = pend_ref[N_EXPERTS - 1] // tm

        def start_tail(b, carry):
            tail_copy(b).start()
            return carry

        def wait_tail(b, carry):
            tail_copy(b).wait()
            return carry

        lax.fori_loop(n_used, n_blocks, start_tail, 0)
        lax.fori_loop(n_used, n_blocks, wait_tail, 0)
        for e in range(N_EXPERTS):
            @pl.when(pend_ref[e] > pstart_ref[e])
            def _():
                zero_copy(e).start()
        for e in range(N_EXPERTS):
            @pl.when(pend_ref[e] > pstart_ref[e])
            def _():
                zero_copy(e).wait()

    def issue(t, carry):
        for k in range(TOP_K):
            _row_copy(h_ref, t, xd_ref, dest_ref[t * TOP_K + k], sem, nc).start(priority=k % 2)
        return carry

    lax.fori_loop(0, tt, issue, 0)

    for k in range(TOP_K):
        pltpu.make_async_copy(h_ref, xd_ref.at[pl.ds(0, tt * nc)], sem).wait()


def _dispatch(h1c, dest_flat, pstart, pend, rows, s, tt, tm):
    nc = h1c.shape[0] // s
    smem = lambda n: pl.BlockSpec((n,), lambda i, *_: (i,), memory_space=pltpu.SMEM)
    kern = functools.partial(_dispatch_kernel, tt=tt, tm=tm, nc=nc, n_blocks=rows // tm)
    return pl.pallas_call(
        kern,
        grid_spec=pltpu.PrefetchScalarGridSpec(
            num_scalar_prefetch=2,
            grid=(s // tt,),
            in_specs=[smem(tt * TOP_K),
                      pl.BlockSpec((tt * nc, LANES), lambda i, *_: (i, 0))],
            out_specs=pl.BlockSpec(memory_space=pl.ANY),
            scratch_shapes=[pltpu.VMEM((tm * nc, LANES), F32), pltpu.SemaphoreType.DMA(()),
                            pltpu.SemaphoreType.DMA(())]),
        out_shape=jax.ShapeDtypeStruct((rows * nc, LANES), F32),
        compiler_params=_params(),
        name="moe_dispatch",
    )(pstart, pend, dest_flat, h1c)


def _expert_kernel(bexp_ref, nused_ref, x_ref, w1_ref, b1_ref, w2_ref, b2_ref, y_ref,
                   w1b_ref, w2b_ref, *, f, chunk, tm):
    b = pl.program_id(0)
    last = nused_ref[0] - 1
    e = bexp_ref[jnp.minimum(b, last)]
    e_prev = bexp_ref[jnp.maximum(jnp.minimum(b, last) - 1, 0)]

    @pl.when((b == 0) | (e != e_prev))
    def _cast_weights():
        def cast(c, carry):
            r = pl.multiple_of(c * chunk, chunk)
            w1b_ref[pl.ds(r, chunk), :] = w1_ref[pl.ds(r, chunk), :].astype(BF16)
            w2b_ref[pl.ds(r, chunk), :] = w2_ref[pl.ds(r, chunk), :].astype(BF16)
            return carry
        lax.fori_loop(0, w1_ref.shape[0] // chunk, cast, 0)

    @pl.when(b > last)
    def _unused_block():
        y_ref[...] = jnp.zeros(y_ref.shape, y_ref.dtype)

    @pl.when(b <= last)
    def _mlp():
        x = _load_rows(x_ref, tm, w1_ref.shape[0]).astype(BF16)
        hdn = jnp.dot(x, w1b_ref[...], preferred_element_type=F32) + b1_ref[...]
        glu = jnp.minimum(hdn[:, 0:f], SWIGLU_LIMIT)
        lin = jnp.clip(hdn[:, f:2 * f], -SWIGLU_LIMIT, SWIGLU_LIMIT)
        act = glu * _sigmoid(SWIGLU_ALPHA * glu) * (lin + 1.0)
        _store_rows(y_ref, jnp.dot(act.astype(BF16), w2b_ref[...], preferred_element_type=F32)
                    + b2_ref[...])


def _experts(x_disp, bexp, nused, w1, b1, w2, b2, layer, tm):
    d = w1.shape[2]
    f = w2.shape[2]
    nc = d // LANES
    rows = x_disp.shape[0] // nc
    assert f == d, "the weight-cast loop assumes d_expert == d_model"

    def blk(b, bexp_ref, nused_ref):
        return jnp.minimum(b, nused_ref[0] - 1)

    def wmap(b, bexp_ref, nused_ref):
        return (layer, bexp_ref[blk(b, bexp_ref, nused_ref)], 0, 0)

    def bmap(b, bexp_ref, nused_ref):
        return (layer, bexp_ref[blk(b, bexp_ref, nused_ref)], 0, 0)

    kern = functools.partial(_expert_kernel, f=f, chunk=128, tm=tm)
    return pl.pallas_call(
        kern,
        grid_spec=pltpu.PrefetchScalarGridSpec(
            num_scalar_prefetch=2,
            grid=(rows // tm,),
            in_specs=[pl.BlockSpec((tm * nc, LANES), lambda b, be, nu: (b, 0)),
                      pl.BlockSpec((None, None, d, 2 * f), wmap),
                      pl.BlockSpec((None, None, 1, 2 * f), bmap),
                      pl.BlockSpec((None, None, f, d), wmap),
                      pl.BlockSpec((None, None, 1, d), bmap)],
            out_specs=pl.BlockSpec((tm * nc, LANES), lambda b, be, nu: (b, 0)),
            scratch_shapes=[pltpu.VMEM((d, 2 * f), BF16), pltpu.VMEM((f, d), BF16)]),
        out_shape=jax.ShapeDtypeStruct((rows * nc, LANES), F32),
        compiler_params=_params(),
        name="moe_experts",
    )(bexp, nused, x_disp, w1, b1.reshape(b1.shape[0], b1.shape[1], 1, -1), w2,
      b2.reshape(b2.shape[0], b2.shape[1], 1, -1))


def _combine_kernel(dest_ref, h_ref, gate_ref, g_ref, b_ref, yd_ref, o_ref,
                    ybuf_ref, sem, *, tt, nc):
    def issue(t, carry):
        for k in range(TOP_K):
            _row_copy(yd_ref, dest_ref[t * TOP_K + k], ybuf_ref.at[k], t, sem, nc).start(
                priority=k % 2)
        return carry

    lax.fori_loop(0, tt, issue, 0)

    for k in range(TOP_K):
        pltpu.make_async_copy(yd_ref.at[pl.ds(0, tt * nc)], ybuf_ref.at[k], sem).wait()

    d = h_ref.shape[1]
    ff = gate_ref[:, 0:1] * _load_rows(ybuf_ref, tt, d, (0,))
    for k in range(1, TOP_K):
        ff = ff + gate_ref[:, k:k + 1] * _load_rows(ybuf_ref, tt, d, (k,))
    o_ref[...] = _layer_norm(RESID_ALPHA * h_ref[...] + ff, g_ref[...], b_ref[...])


def _combine(y_disp, h1, gates, dest_flat, g, b, tt):
    s, d = h1.shape
    nc = d // LANES
    row = lambda a: a.reshape(1, -1).astype(F32)
    kern = functools.partial(_combine_kernel, tt=tt, nc=nc)
    return pl.pallas_call(
        kern,
        grid=(s // tt,),
        in_specs=[pl.BlockSpec((tt * TOP_K,), lambda i: (i,), memory_space=pltpu.SMEM),
                  pl.BlockSpec((tt, d), lambda i: (i, 0)),
                  pl.BlockSpec((tt, TOP_K), lambda i: (i, 0)),
                  pl.BlockSpec((1, d), lambda i: (0, 0)),
                  pl.BlockSpec((1, d), lambda i: (0, 0)),
                  pl.BlockSpec(memory_space=pl.ANY)],
        out_specs=pl.BlockSpec((tt, d), lambda i: (i, 0)),
        scratch_shapes=[pltpu.VMEM((TOP_K, tt * nc, LANES), F32), pltpu.SemaphoreType.DMA(())],
        out_shape=jax.ShapeDtypeStruct((s, d), F32),
        compiler_params=_params(),
        name="moe_combine",
    )(dest_flat, h1, gates, row(g), row(b), y_disp)


def _tiles(s):
    t = lambda want: math.gcd(s, want)
    return dict(dense=t(512), attn=t(512), seq=t(512), route=t(512), expert=t(512), combine=t(256))


def kernel(x, ln_in_g, ln_in_b, w_in, lam_q1, lam_k1, lam_q2, lam_k2, subln_g, lru_conv_w, lru_conv_b, lru_wa, lru_ba, lru_wx, lru_bx, lru_lambda, cf_conv_w, cf_conv_b, cf_ln_g, cf_ln_b, w_out, ln1_g, ln1_b, router_w, router_b, moe_w1, moe_b1, moe_w2, moe_b2, ln2_g, ln2_b):
    bsz, s, d = x.shape
    assert bsz == 1
    tl = _tiles(s)
    tm = tl["expert"]
    n_blocks = (s * TOP_K) // tm + N_EXPERTS
    rows = n_blocks * tm

    h = _input_ln(x.reshape(s, d), ln_in_g, ln_in_b, tl["dense"])
    for l in range(DEPTH):
        lam_init = 0.8 - 0.6 * math.exp(-0.3 * l)
        qt, k, vt, rest = _inproj(h, w_in[l].astype(BF16), tl["attn"])
        att = _attention(qt, k, vt, lam_q1[l], lam_k1[l], lam_q2[l], lam_k2[l], subln_g[l],
                         lam_init)
        rc = _seqmix(rest, lru_conv_w[l], lru_conv_b[l], lru_wa[l], lru_ba[l], lru_wx[l],
                     lru_bx[l], lru_lambda[l], cf_conv_w[l], cf_conv_b[l], cf_ln_g[l], cf_ln_b[l],
                     tl["seq"])
        h1, h1c, idx, gates, rank, counts = _outproj_router(
            att, rc, h, w_out[l].astype(BF16), ln1_g[l], ln1_b[l], router_w[l], router_b[l],
            tl["route"])
        counts = counts.reshape(N_EXPERTS)
        padded = (counts + tm - 1) // tm * tm
        e_ids = jnp.arange(N_EXPERTS, dtype=jnp.int32)
        pend = jnp.sum(jnp.where(e_ids[None, :] <= e_ids[:, None], padded[None, :], 0),
                       axis=1).astype(jnp.int32)
        pstart = pend - padded
        block_row = jnp.arange(n_blocks, dtype=jnp.int32) * tm
        bexp = jnp.minimum(jnp.sum((pend[None, :] <= block_row[:, None]).astype(jnp.int32), axis=1),
                           N_EXPERTS - 1).astype(jnp.int32)
        nused = (pend[-1:] // tm).astype(jnp.int32)
        dest_flat = _dest_rows(idx, rank, pstart, tl["route"]).reshape(s * TOP_K)
        x_disp = _dispatch(h1c, dest_flat, pstart, pend, rows, s, tl["route"], tm)
        y_disp = _experts(x_disp, bexp, nused, moe_w1, moe_b1, moe_w2, moe_b2, l, tm)
        h = _combine(y_disp, h1, gates, dest_flat, ln2_g[l], ln2_b[l], tl["combine"])
    return h.reshape(bsz, s, d)
```

```python
import functools
import math

import jax
import jax.numpy as jnp
from jax import lax
from jax.experimental import pallas as pl
from jax.experimental.pallas import tpu as pltpu

F32 = jnp.float32
BF16 = jnp.bfloat16

DEPTH = 2
ATT_HEADS = 4
ATT_QK_DIM = 64
ATT_V_DIM = 128
LRU_BLOCKS = 4
LRU_CONV = 4
LRU_C = 8.0
CONV_KERNEL = 31
N_EXPERTS = 32
TOP_K = 4
SWIGLU_LIMIT = 7.0
SWIGLU_ALPHA = 1.702
LN_EPS = 1e-5
RESID_ALPHA = (2.0 * DEPTH) ** 0.25

VMEM_LIMIT_BYTES = 56 * 1024 * 1024
HALO = 32
LANES = 128
SUBLANES = 8
MXU_DIM = 256
LOG2E = math.log2(math.e)
V_PAD_ROWS = 16
Q_GROUP = 256
COMBINE_PARTS = 4

def _load_rows(ref, n_rows, d, lead=(), first_row=0):
    nc = d // LANES
    return jnp.concatenate([ref[lead + (pl.ds(first_row * nc + c, n_rows, stride=nc), slice(None))]
                            for c in range(nc)], axis=-1)


def _store_rows(ref, val):
    n_rows, d = val.shape
    nc = d // LANES
    for c in range(nc):
        ref[pl.ds(c, n_rows, stride=nc), :] = val[:, c * LANES:(c + 1) * LANES]


def _params(n_axes=1):
    return pltpu.CompilerParams(dimension_semantics=("arbitrary",) * n_axes,
                                vmem_limit_bytes=VMEM_LIMIT_BYTES)


def _layer_norm(x, g, b):
    mu = jnp.mean(x, axis=-1, keepdims=True)
    xc = x - mu
    var = jnp.mean(xc * xc, axis=-1, keepdims=True)
    return xc * lax.rsqrt(var + LN_EPS) * g + b


def _sigmoid(x):
    return 1.0 / (1.0 + jnp.exp(-x))


def _ln_kernel(x_ref, g_ref, b_ref, o_ref):
    o_ref[...] = _layer_norm(x_ref[...], g_ref[...], b_ref[...])


def _input_ln(x, g, b, tm):
    s, d = x.shape
    return pl.pallas_call(
        _ln_kernel,
        grid=(s // tm,),
        in_specs=[pl.BlockSpec((tm, d), lambda i: (i, 0)),
                  pl.BlockSpec((1, d), lambda i: (0, 0)),
                  pl.BlockSpec((1, d), lambda i: (0, 0))],
        out_specs=pl.BlockSpec((tm, d), lambda i: (i, 0)),
        out_shape=jax.ShapeDtypeStruct((s, d), F32),
        compiler_params=_params(),
        name="input_ln",
    )(x, g.reshape(1, d), b.reshape(1, d))


def _inproj_kernel(h_ref, w_ref, qt_ref, k_ref, vt_ref, rest_ref, *, scale):
    nh, hd = ATT_HEADS, ATT_V_DIM
    w_att = nh * hd
    tm = h_ref.shape[0]
    hb = h_ref[...].astype(BF16)
    q = jnp.dot(hb, w_ref[:, 0:w_att], preferred_element_type=F32) * scale
    k_ref[...] = jnp.dot(hb, w_ref[:, w_att:2 * w_att], preferred_element_type=F32).astype(BF16)
    v = jnp.dot(hb, w_ref[:, 2 * w_att:3 * w_att], preferred_element_type=F32)
    ones = jnp.ones((V_PAD_ROWS, tm), BF16)
    for h in range(nh):
        qt_ref[h] = q[:, h * hd:(h + 1) * hd].T.astype(BF16)
        vt_ref[h, 0:hd, :] = v[:, h * hd:(h + 1) * hd].T.astype(BF16)
        vt_ref[h, hd:hd + V_PAD_ROWS, :] = ones
    rest_ref[...] = jnp.dot(hb, w_ref[:, 3 * w_att:], preferred_element_type=F32)


def _inproj(h, w_bf16, tm):
    s, d = h.shape
    n = w_bf16.shape[1]
    nh, hd = ATT_HEADS, ATT_V_DIM
    n_att = 3 * nh * hd
    kern = functools.partial(_inproj_kernel, scale=ATT_QK_DIM ** -0.5 * LOG2E)
    tposed = lambda r: pl.BlockSpec((nh, None, r, tm), lambda i: (0, i, 0, 0))
    return pl.pallas_call(
        kern,
        grid=(s // tm,),
        in_specs=[pl.BlockSpec((tm, d), lambda i: (i, 0)),
                  pl.BlockSpec((d, n), lambda i: (0, 0))],
        out_specs=[tposed(hd),
                   pl.BlockSpec((tm, nh * hd), lambda i: (i, 0)),
                   tposed(hd + V_PAD_ROWS),
                   pl.BlockSpec((tm, n - n_att), lambda i: (i, 0))],
        out_shape=[jax.ShapeDtypeStruct((nh, s // tm, hd, tm), BF16),
                   jax.ShapeDtypeStruct((s, nh * hd), BF16),
                   jax.ShapeDtypeStruct((nh, s // tm, hd + V_PAD_ROWS, tm), BF16),
                   jax.ShapeDtypeStruct((s, n - n_att), F32)],
        compiler_params=_params(),
        name="inproj",
    )(h, w_bf16)


def _split3(x):
    hi = x.astype(BF16)
    rem = x - hi.astype(F32)
    mid = rem.astype(BF16)
    lo = (rem - mid.astype(F32)).astype(BF16)
    return hi, mid, lo


def _attn_kernel(slopes_ref, qt_ref, k_ref, vt_ref, lq1_ref, lk1_ref, lq2_ref, lk2_ref, g_ref,
                 o_ref, qs_ref, kc_ref, mask_ref, *stat_refs, tq, lam_init):
    h = pl.program_id(0)
    qi = pl.program_id(1)
    slope2 = slopes_ref[h]
    dk, hd = ATT_QK_DIM, ATT_V_DIM
    n_groups = (2 * tq) // Q_GROUP
    m_refs, acc_refs = stat_refs[:n_groups], stat_refs[n_groups:2 * n_groups]
    st_refs = stat_refs[2 * n_groups:]

    @pl.when(qi == 0)
    def _build_constants():
        c = lax.broadcasted_iota(jnp.int32, (tq, hd), 0)
        col = lax.broadcasted_iota(jnp.int32, (tq, hd), 1)
        c_lo = jnp.bitwise_and(c, MXU_DIM - 1)
        c_hi = c - c_lo
        kc = jnp.where(col < 3, c_hi, jnp.where(col < 6, c_lo, jnp.where(col < 9, 1, 0)))
        kc_ref[...] = kc.astype(F32).astype(BF16)
        r = lax.broadcasted_iota(jnp.int32, (hd, 2 * tq), 1)
        r = jnp.where(r >= tq, r - tq, r)
        row = lax.broadcasted_iota(jnp.int32, (hd, 2 * tq), 0)
        sl = jnp.full((hd, 2 * tq), slope2, F32)
        s_hi, s_mid, s_lo = _split3(sl)
        t_hi, t_mid, t_lo = _split3(-(sl * r.astype(F32)))
        aug = jnp.zeros((hd, 2 * tq), F32)
        for i, piece in enumerate((s_hi, s_mid, s_lo, s_hi, s_mid, s_lo, t_hi, t_mid, t_lo)):
            aug = jnp.where(row == i, piece.astype(F32), aug)
        qs_ref[hd:2 * hd, :] = aug.astype(BF16)
        ck = lax.broadcasted_iota(jnp.int32, (tq, 2 * tq), 0)
        rq = lax.broadcasted_iota(jnp.int32, (tq, 2 * tq), 1)
        rq = jnp.where(rq >= tq, rq - tq, rq)
        mask_ref[...] = jnp.where(ck <= rq, 0.0, -jnp.inf)

    qt = qt_ref[...]
    dim = lax.broadcasted_iota(jnp.int32, qt.shape, 0)
    zero_q = jnp.zeros_like(qt)
    qs_ref[0:hd, 0:tq] = jnp.where(dim < dk, qt, zero_q)
    qs_ref[0:hd, tq:2 * tq] = jnp.where(dim >= dk, qt, zero_q)
    for m_ref, acc_ref in zip(m_refs, acc_refs):
        m_ref[...] = jnp.full(m_ref.shape, -jnp.inf, F32)
        acc_ref[...] = jnp.zeros(acc_ref.shape, F32)

    cols = [slice(g * Q_GROUP, (g + 1) * Q_GROUP) for g in range(n_groups)]

    def keys_of(j):
        start = pl.multiple_of(j * tq, tq)
        return jnp.concatenate([k_ref[pl.ds(start, tq), :], kc_ref[...]], axis=1)

    def qk(ka, g):
        return jnp.dot(ka, qs_ref[:, cols[g]], preferred_element_type=F32)

    def softmax_pv(st, j, vt, g, diagonal):
        if diagonal:
            st = st + mask_ref[:, cols[g]]
        off = slope2 * ((j - qi) * tq).astype(F32)
        m_old = m_refs[g][...]
        m_new = jnp.maximum(m_old, jnp.max(st, axis=0, keepdims=True) + off)
        p = jnp.exp2(st - (m_new - off)).astype(BF16)
        alpha = jnp.exp2(m_old - m_new)
        acc_refs[g][...] = alpha * acc_refs[g][...] + jnp.dot(vt, p, preferred_element_type=F32)
        m_refs[g][...] = m_new

    def run_blocks(j_a, j_b, j_next, diagonal_last):
        vt_a = vt_ref[j_a]
        if j_b is None:
            for g in range(n_groups):
                softmax_pv(st_refs[g][...], j_a, vt_a, g, diagonal_last)
            return
        ka_b, vt_b = keys_of(j_b), vt_ref[j_b]
        s_b = {}
        for g in range(n_groups):
            softmax_pv(st_refs[g][...], j_a, vt_a, g, False)
            s_b[g] = qk(ka_b, g)
        ka_n = keys_of(j_next) if j_next is not None else None
        for g in range(n_groups):
            softmax_pv(s_b.pop(g), j_b, vt_b, g, diagonal_last)
            if ka_n is not None:
                st_refs[g][...] = qk(ka_n, g)

    ka_0 = keys_of(0)
    for g in range(n_groups):
        st_refs[g][...] = qk(ka_0, g)

    def body(pair, carry):
        run_blocks(2 * pair, 2 * pair + 1, 2 * pair + 2, False)
        return carry

    lax.fori_loop(0, lax.shift_right_logical(qi, 1), body, 0)
    odd = jnp.bitwise_and(qi, 1) == 1

    @pl.when(odd)
    def _tail_pair():
        run_blocks(qi - 1, qi, None, True)

    @pl.when(jnp.logical_not(odd))
    def _tail_single():
        run_blocks(qi, None, None, True)

    lam = (jnp.exp(jnp.sum(lq1_ref[...] * lk1_ref[...], axis=-1, keepdims=True))
           - jnp.exp(jnp.sum(lq2_ref[...] * lk2_ref[...], axis=-1, keepdims=True)) + lam_init)
    ot = jnp.concatenate([a[0:hd, :] / a[hd:hd + 1, :] for a in acc_refs], axis=1)
    o = ot[:, 0:tq] - lam * ot[:, tq:2 * tq]
    o = o * lax.rsqrt(jnp.mean(o * o, axis=0, keepdims=True) + LN_EPS)
    o_ref[...] = (o * g_ref[...] * (1.0 - lam_init)).T.astype(o_ref.dtype)


def _attention(qt, k, vt, lq1, lk1, lq2, lk2, subln_g, lam_init):
    nh, nblk, hd, tq = qt.shape
    vrows = vt.shape[2]
    s = k.shape[0]
    assert (2 * tq) % Q_GROUP == 0 and hd + 9 <= MXU_DIM
    n_groups = (2 * tq) // Q_GROUP
    slopes = jnp.exp2(-8.0 * (jnp.arange(nh, dtype=F32) + 1.0) / nh) * LOG2E
    vec = lambda a: a.reshape(1, -1).astype(F32)
    small = lambda n: pl.BlockSpec((1, n), lambda h, i, *_: (0, 0))
    kern = functools.partial(_attn_kernel, tq=tq, lam_init=lam_init)
    return pl.pallas_call(
        kern,
        grid_spec=pltpu.PrefetchScalarGridSpec(
            num_scalar_prefetch=1,
            grid=(nh, nblk),
            in_specs=[pl.BlockSpec((None, None, hd, tq), lambda h, i, *_: (h, i, 0, 0)),
                      pl.BlockSpec((s, hd), lambda h, i, *_: (0, h)),
                      pl.BlockSpec((None, nblk, vrows, tq), lambda h, i, *_: (h, 0, 0, 0)),
                      small(ATT_QK_DIM), small(ATT_QK_DIM), small(ATT_QK_DIM), small(ATT_QK_DIM),
                      pl.BlockSpec((hd, 1), lambda h, i, *_: (0, 0))],
            out_specs=pl.BlockSpec((tq, hd), lambda h, i, *_: (i, h)),
            scratch_shapes=[pltpu.VMEM((2 * hd, 2 * tq), BF16),
                            pltpu.VMEM((tq, hd), BF16),
                            pltpu.VMEM((tq, 2 * tq), F32)]
            + [pltpu.VMEM((1, Q_GROUP), F32)] * n_groups
            + [pltpu.VMEM((vrows, Q_GROUP), F32)] * n_groups
            + [pltpu.VMEM((tq, Q_GROUP), F32)] * n_groups),
        out_shape=jax.ShapeDtypeStruct((s, nh * hd), BF16),
        compiler_params=_params(2),
        name="diff_attention",
    )(slopes, qt, k, vt, vec(lq1), vec(lk1), vec(lq2), vec(lk2),
      subln_g.reshape(hd, 1).astype(F32))


def _seqmix_kernel(u_ref, lcw_ref, lcb_ref, wa_ref, ba_ref, wx_ref, bx_ref, lam_ref,
                   ccw_ref, ccb_ref, cg_ref, cb_ref, o_ref,
                   xbuf_ref, cbuf_ref, shift_ref, a_ref, b_ref, hs_ref, hc_ref, *, tt, w):
    i = pl.program_id(0)

    @pl.when(i == 0)
    def _init():
        xbuf_ref[0:HALO, :] = jnp.zeros((HALO, w), F32)
        cbuf_ref[0:HALO, :] = jnp.zeros((HALO, w), F32)
        hc_ref[...] = jnp.zeros(hc_ref.shape, F32)

    xbuf_ref[HALO:HALO + tt, :] = u_ref[:, 0:w]
    xc = jnp.zeros((tt, w), F32)
    for j in range(LRU_CONV):
        off = HALO - (LRU_CONV - 1) + j
        xc = xc + lcw_ref[j:j + 1, :] * xbuf_ref[off:off + tt, :]
    xc = xc + lcb_ref[...]
    xcb = xc.astype(BF16)
    gate_a = _sigmoid(jnp.dot(xcb, wa_ref[...], preferred_element_type=F32) + ba_ref[...])
    gate_x = _sigmoid(jnp.dot(xcb, wx_ref[...], preferred_element_type=F32) + bx_ref[...])
    nl = -lam_ref[...]
    softplus = jnp.maximum(nl, 0.0) + jnp.log(1.0 + jnp.exp(-jnp.abs(nl)))
    log_a = -LRU_C * gate_a * softplus
    a_ref[...] = jnp.exp(log_a)
    b_ref[...] = jnp.sqrt(1.0 - jnp.exp(2.0 * log_a)) * gate_x * xc

    def step(t, hprev):
        hnew = a_ref[pl.ds(t, 1), :] * hprev + b_ref[pl.ds(t, 1), :]
        hs_ref[pl.ds(t, 1), :] = hnew
        return hnew

    hc_ref[...] = lax.fori_loop(0, tt, step, hc_ref[...], unroll=8)
    o_ref[:, 0:w] = (hs_ref[...] * jax.nn.gelu(u_ref[:, w:2 * w], approximate=True)).astype(o_ref.dtype)
    xbuf_ref[0:HALO, :] = xbuf_ref[tt:tt + HALO, :]

    cbuf_ref[HALO:HALO + tt, :] = u_ref[:, 2 * w:3 * w] * _sigmoid(u_ref[:, 3 * w:4 * w])
    span = shift_ref.shape[1]
    for p in range(1, SUBLANES):
        shift_ref[p - 1] = cbuf_ref[p:p + span, :]
    y = jnp.zeros((tt, w), F32)
    for j in range(CONV_KERNEL):
        off = HALO - (CONV_KERNEL - 1) + j
        p, base = off % SUBLANES, off - off % SUBLANES
        win = cbuf_ref[base:base + tt, :] if p == 0 else shift_ref[p - 1, base:base + tt, :]
        y = y + ccw_ref[j:j + 1, :] * win
    y = _layer_norm(y + ccb_ref[...], cg_ref[...], cb_ref[...])
    o_ref[:, w:2 * w] = (y * _sigmoid(y)).astype(o_ref.dtype)
    cbuf_ref[0:HALO, :] = cbuf_ref[tt:tt + HALO, :]


def _block_diag(wb):
    nb, bd, _ = wb.shape
    eye = jnp.eye(nb, dtype=jnp.bool_)
    return jnp.where(eye[:, None, :, None], wb[:, :, None, :], 0.0).reshape(nb * bd, nb * bd)


def _seqmix(rest, lcw, lcb, wa, ba, wx, bx, lam, ccw, ccb, cg, cb, tt):
    s, n = rest.shape
    w = n // 4
    row = lambda a: a.reshape(1, w).astype(F32)
    full = lambda r, c: pl.BlockSpec((r, c), lambda i: (0, 0))
    kern = functools.partial(_seqmix_kernel, tt=tt, w=w)
    return pl.pallas_call(
        kern,
        grid=(s // tt,),
        in_specs=[pl.BlockSpec((tt, n), lambda i: (i, 0)),
                  full(LRU_CONV, w), full(1, w), full(w, w), full(1, w), full(w, w), full(1, w),
                  full(1, w), full(CONV_KERNEL, w), full(1, w), full(1, w), full(1, w)],
        out_specs=pl.BlockSpec((tt, 2 * w), lambda i: (i, 0)),
        out_shape=jax.ShapeDtypeStruct((s, 2 * w), BF16),
        scratch_shapes=[pltpu.VMEM((tt + HALO, w), F32), pltpu.VMEM((tt + HALO, w), F32),
                        pltpu.VMEM((SUBLANES - 1, tt + HALO - SUBLANES, w), F32),
                        pltpu.VMEM((tt, w), F32), pltpu.VMEM((tt, w), F32), pltpu.VMEM((tt, w), F32),
                        pltpu.VMEM((1, w), F32)],
        compiler_params=_params(),
        name="seqmix",
    )(rest, lcw, row(lcb), _block_diag(wa).astype(BF16), row(ba), _block_diag(wx).astype(BF16),
      row(bx), row(lam), ccw, row(ccb), row(cg), row(cb))


def _outproj_kernel(att_ref, rc_ref, h_ref, wo_ref, g_ref, b_ref, rwh_ref, rwl_ref, rb_ref,
                    h1_ref, h1c_ref, idx_ref, gate_ref, rank_ref, cnt_ref, tri_ref, carry_ref,
                    *, tm, n_att):
    i = pl.program_id(0)
    ne = N_EXPERTS

    @pl.when(i == 0)
    def _init():
        r = lax.broadcasted_iota(jnp.int32, (tm, tm), 0)
        c = lax.broadcasted_iota(jnp.int32, (tm, tm), 1)
        tri_ref[...] = jnp.where(c < r, 1.0, 0.0).astype(BF16)
        carry_ref[...] = jnp.zeros(carry_ref.shape, F32)

    mix = (jnp.dot(att_ref[...], wo_ref[0:n_att, :], preferred_element_type=F32)
           + jnp.dot(rc_ref[...], wo_ref[n_att:, :], preferred_element_type=F32))
    h1 = _layer_norm(RESID_ALPHA * h_ref[...] + mix, g_ref[...], b_ref[...])
    h1_ref[...] = h1
    _store_rows(h1c_ref, h1)

    h1_hi = h1.astype(BF16)
    h1_lo = (h1 - h1_hi.astype(F32)).astype(BF16)
    logits = (jnp.dot(h1_hi, rwh_ref[...], preferred_element_type=F32)
              + jnp.dot(h1_hi, rwl_ref[...], preferred_element_type=F32)
              + jnp.dot(h1_lo, rwh_ref[...], preferred_element_type=F32)) + rb_ref[...]
    lane = lax.broadcasted_iota(jnp.int32, (tm, ne), 1).astype(F32)
    onehot = jnp.zeros((tm, ne), F32)
    vals, sels = [], []
    for _ in range(TOP_K):
        mx = jnp.max(logits, axis=-1, keepdims=True)
        sel = jnp.min(jnp.where(logits == mx, lane, float(ne)), axis=-1, keepdims=True)
        hit = lane == sel
        onehot = onehot + jnp.where(hit, 1.0, 0.0)
        logits = jnp.where(hit, -jnp.inf, logits)
        vals.append(mx)
        sels.append(sel)
    ex = [jnp.exp(v - vals[0]) for v in vals]
    den = ex[0] + ex[1] + ex[2] + ex[3]
    before = jnp.dot(tri_ref[...], onehot.astype(BF16), preferred_element_type=F32) + carry_ref[...]
    for k in range(TOP_K):
        idx_ref[:, k:k + 1] = sels[k].astype(jnp.int32)
        gate_ref[:, k:k + 1] = ex[k] / den
        rank_ref[:, k:k + 1] = jnp.sum(jnp.where(lane == sels[k], before, 0.0), axis=-1,
                                       keepdims=True).astype(jnp.int32)
    carry_ref[...] = carry_ref[...] + jnp.sum(onehot, axis=0, keepdims=True)
    cnt_ref[...] = carry_ref[...].astype(jnp.int32)


def _outproj_router(att, rc, h, wo_bf16, g, b, rw, rb, tm):
    s, d = h.shape
    n_att = att.shape[1]
    ne = N_EXPERTS
    row = lambda a: a.reshape(1, -1).astype(F32)
    full = lambda r, c: pl.BlockSpec((r, c), lambda i: (0, 0))
    tile = lambda c: pl.BlockSpec((tm, c), lambda i: (i, 0))
    kern = functools.partial(_outproj_kernel, tm=tm, n_att=n_att)
    rw_hi = rw.astype(BF16)
    rw_lo = (rw - rw_hi.astype(F32)).astype(BF16)
    return pl.pallas_call(
        kern,
        grid=(s // tm,),
        in_specs=[tile(n_att), tile(rc.shape[1]), tile(d), full(wo_bf16.shape[0], d),
                  full(1, d), full(1, d), full(d, ne), full(d, ne), full(1, ne)],
        out_specs=[tile(d), pl.BlockSpec((tm * (d // LANES), LANES), lambda i: (i, 0)),
                   tile(TOP_K), tile(TOP_K), tile(TOP_K), full(1, ne)],
        out_shape=[jax.ShapeDtypeStruct((s, d), F32),
                   jax.ShapeDtypeStruct((s * (d // LANES), LANES), F32),
                   jax.ShapeDtypeStruct((s, TOP_K), jnp.int32),
                   jax.ShapeDtypeStruct((s, TOP_K), F32),
                   jax.ShapeDtypeStruct((s, TOP_K), jnp.int32),
                   jax.ShapeDtypeStruct((1, ne), jnp.int32)],
        scratch_shapes=[pltpu.VMEM((tm, tm), BF16), pltpu.VMEM((1, ne), F32)],
        compiler_params=_params(),
        name="outproj_router",
    )(att, rc, h, wo_bf16, row(g), row(b), rw_hi, rw_lo, row(rb))


def _row_copy(src_ref, src_row, dst_ref, dst_row, sem, nc):
    src = src_ref.at[pl.ds(pl.multiple_of(src_row * nc, nc), nc)]
    dst = dst_ref.at[pl.ds(pl.multiple_of(dst_row * nc, nc), nc)]
    return pltpu.make_async_copy(src, dst, sem)


def _dest_kernel(idx_ref, rank_ref, pstart_ref, dest_ref):
    tm, ne = idx_ref.shape[0], pstart_ref.shape[1]
    lane = lax.broadcasted_iota(jnp.int32, (tm, ne), 1)
    pstart = pstart_ref[...].astype(F32)
    for k in range(TOP_K):
        hit = lane == idx_ref[:, k:k + 1]
        start = jnp.sum(jnp.where(hit, pstart, 0.0), axis=-1, keepdims=True)
        dest_ref[:, k:k + 1] = start.astype(jnp.int32) + rank_ref[:, k:k + 1]


def _dest_rows(idx, rank, pstart, tm):
    s = idx.shape[0]
    ne = pstart.shape[0]
    tile = pl.BlockSpec((tm, TOP_K), lambda i: (i, 0))
    return pl.pallas_call(
        _dest_kernel,
        grid=(s // tm,),
        in_specs=[tile, tile, pl.BlockSpec((1, ne), lambda i: (0, 0))],
        out_specs=tile,
        out_shape=jax.ShapeDtypeStruct((s, TOP_K), jnp.int32),
        compiler_params=_params(),
        name="moe_dest_rows",
    )(idx, rank, pstart.reshape(1, ne))


def _dispatch_kernel(pstart_ref, pend_ref, dest_ref, h_ref, xd_ref, zero_ref, sem, zsem,
                     *, tt, tm, nc, n_blocks):
    i = pl.program_id(0)

    def zero_copy(e):
        start = pl.multiple_of((pend_ref[e] - tm) * nc, tm * nc)
        return pltpu.make_async_copy(zero_ref, xd_ref.at[pl.ds(start, tm * nc)], zsem)

    def tail_copy(b):
        start = pl.multiple_of(b * (tm * nc), tm * nc)
        return pltpu.make_async_copy(zero_ref, xd_ref.at[pl.ds(start, tm * nc)], zsem)

    @pl.when(i == 0)
    def _clear_padding():
        zero_ref[...] = jnp.zeros(zero_ref.shape, zero_ref.dtype)
        n_used = pend_ref[N_EXPERTS - 1] // tm

        def start_tail(b, carry):
            tail_copy(b).start()
            return carry

        def wait_tail(b, carry):
            tail_copy(b).wait()
            return carry

        lax.fori_loop(n_used, n_blocks, start_tail, 0)
        lax.fori_loop(n_used, n_blocks, wait_tail, 0)
        for e in range(N_EXPERTS):
            @pl.when(pend_ref[e] > pstart_ref[e])
            def _():
                zero_copy(e).start()
        for e in range(N_EXPERTS):
            @pl.when(pend_ref[e] > pstart_ref[e])
            def _():
                zero_copy(e).wait()

    def issue(t, carry):
        for k in range(TOP_K):
            _row_copy(h_ref, t, xd_ref, dest_ref[t * TOP_K + k], sem, nc).start(priority=k % 2)
        return carry

    lax.fori_loop(0, tt, issue, 0)

    for k in range(TOP_K):
        pltpu.make_async_copy(h_ref, xd_ref.at[pl.ds(0, tt * nc)], sem).wait()


def _dispatch(h1c, dest_flat, pstart, pend, rows, s, tt, tm):
    nc = h1c.shape[0] // s
    smem = lambda n: pl.BlockSpec((n,), lambda i, *_: (i,), memory_space=pltpu.SMEM)
    kern = functools.partial(_dispatch_kernel, tt=tt, tm=tm, nc=nc, n_blocks=rows // tm)
    return pl.pallas_call(
        kern,
        grid_spec=pltpu.PrefetchScalarGridSpec(
            num_scalar_prefetch=2,
            grid=(s // tt,),
            in_specs=[smem(tt * TOP_K),
                      pl.BlockSpec((tt * nc, LANES), lambda i, *_: (i, 0))],
            out_specs=pl.BlockSpec(memory_space=pl.ANY),
            scratch_shapes=[pltpu.VMEM((tm * nc, LANES), F32), pltpu.SemaphoreType.DMA(()),
                            pltpu.SemaphoreType.DMA(())]),
        out_shape=jax.ShapeDtypeStruct((rows * nc, LANES), F32),
        compiler_params=_params(),
        name="moe_dispatch",
    )(pstart, pend, dest_flat, h1c)


def _expert_kernel(bexp_ref, nused_ref, x_ref, w1_ref, b1_ref, w2_ref, b2_ref, y_ref,
                   w1b_ref, w2b_ref, *, f, chunk, tm):
    b = pl.program_id(0)
    last = nused_ref[0] - 1
    e = bexp_ref[jnp.minimum(b, last)]
    e_prev = bexp_ref[jnp.maximum(jnp.minimum(b, last) - 1, 0)]

    @pl.when((b == 0) | (e != e_prev))
    def _cast_weights():
        def cast(c, carry):
            r = pl.multiple_of(c * chunk, chunk)
            w1b_ref[pl.ds(r, chunk), :] = w1_ref[pl.ds(r, chunk), :].astype(BF16)
            w2b_ref[pl.ds(r, chunk), :] = w2_ref[pl.ds(r, chunk), :].astype(BF16)
            return carry
        lax.fori_loop(0, w1_ref.shape[0] // chunk, cast, 0)

    @pl.when(b > last)
    def _unused_block():
        y_ref[...] = jnp.zeros(y_ref.shape, y_ref.dtype)

    @pl.when(b <= last)
    def _mlp():
        x = _load_rows(x_ref, tm, w1_ref.shape[0]).astype(BF16)
        hdn = jnp.dot(x, w1b_ref[...], preferred_element_type=F32) + b1_ref[...]
        glu = jnp.minimum(hdn[:, 0:f], SWIGLU_LIMIT)
        lin = jnp.clip(hdn[:, f:2 * f], -SWIGLU_LIMIT, SWIGLU_LIMIT)
        act = glu * _sigmoid(SWIGLU_ALPHA * glu) * (lin + 1.0)
        _store_rows(y_ref, jnp.dot(act.astype(BF16), w2b_ref[...], preferred_element_type=F32)
                    + b2_ref[...])


def _experts(x_disp, bexp, nused, w1, b1, w2, b2, layer, tm):
    d = w1.shape[2]
    f = w2.shape[2]
    nc = d // LANES
    rows = x_disp.shape[0] // nc
    assert f == d, "the weight-cast loop assumes d_expert == d_model"

    def blk(b, bexp_ref, nused_ref):
        return jnp.minimum(b, nused_ref[0] - 1)

    def wmap(b, bexp_ref, nused_ref):
        return (layer, bexp_ref[blk(b, bexp_ref, nused_ref)], 0, 0)

    def bmap(b, bexp_ref, nused_ref):
        return (layer, bexp_ref[blk(b, bexp_ref, nused_ref)], 0, 0)

    kern = functools.partial(_expert_kernel, f=f, chunk=128, tm=tm)
    return pl.pallas_call(
        kern,
        grid_spec=pltpu.PrefetchScalarGridSpec(
            num_scalar_prefetch=2,
            grid=(rows // tm,),
            in_specs=[pl.BlockSpec((tm * nc, LANES), lambda b, be, nu: (b, 0)),
                      pl.BlockSpec((None, None, d, 2 * f), wmap),
                      pl.BlockSpec((None, None, 1, 2 * f), bmap),
                      pl.BlockSpec((None, None, f, d), wmap),
                      pl.BlockSpec((None, None, 1, d), bmap)],
            out_specs=pl.BlockSpec((tm * nc, LANES), lambda b, be, nu: (b, 0)),
            scratch_shapes=[pltpu.VMEM((d, 2 * f), BF16), pltpu.VMEM((f, d), BF16)]),
        out_shape=jax.ShapeDtypeStruct((rows * nc, LANES), F32),
        compiler_params=_params(),
        name="moe_experts",
    )(bexp, nused, x_disp, w1, b1.reshape(b1.shape[0], b1.shape[1], 1, -1), w2,
      b2.reshape(b2.shape[0], b2.shape[1], 1, -1))


def _combine_kernel(dest_ref, h_ref, gate_ref, g_ref, b_ref, yd_ref, o_ref,
                    ybuf_ref, sems, *, tt, nc):
    th = tt // COMBINE_PARTS
    d = h_ref.shape[1]

    def issue(part):
        def one(t, carry):
            for k in range(TOP_K):
                _row_copy(yd_ref, dest_ref[t * TOP_K + k], ybuf_ref.at[k], t, sems.at[part],
                          nc).start(priority=k % 2)
            return carry
        lax.fori_loop(part * th, (part + 1) * th, one, 0)

    def finish(part):
        for k in range(TOP_K):
            pltpu.make_async_copy(yd_ref.at[pl.ds(0, th * nc)],
                                  ybuf_ref.at[k, pl.ds(part * th * nc, th * nc)],
                                  sems.at[part]).wait()
        rows = slice(part * th, (part + 1) * th)
        ff = gate_ref[rows, 0:1] * _load_rows(ybuf_ref, th, d, (0,), part * th)
        for k in range(1, TOP_K):
            ff = ff + gate_ref[rows, k:k + 1] * _load_rows(ybuf_ref, th, d, (k,), part * th)
        o_ref[rows, :] = _layer_norm(RESID_ALPHA * h_ref[rows, :] + ff, g_ref[...], b_ref[...])

    issue(0)
    for part in range(COMBINE_PARTS):
        if part + 1 < COMBINE_PARTS:
            issue(part + 1)
        finish(part)


def _combine(y_disp, h1, gates, dest_flat, g, b, tt):
    s, d = h1.shape
    nc = d // LANES
    row = lambda a: a.reshape(1, -1).astype(F32)
    kern = functools.partial(_combine_kernel, tt=tt, nc=nc)
    return pl.pallas_call(
        kern,
        grid=(s // tt,),
        in_specs=[pl.BlockSpec((tt * TOP_K,), lambda i: (i,), memory_space=pltpu.SMEM),
                  pl.BlockSpec((tt, d), lambda i: (i, 0)),
                  pl.BlockSpec((tt, TOP_K), lambda i: (i, 0)),
                  pl.BlockSpec((1, d), lambda i: (0, 0)),
                  pl.BlockSpec((1, d), lambda i: (0, 0)),
                  pl.BlockSpec(memory_space=pl.ANY)],
        out_specs=pl.BlockSpec((tt, d), lambda i: (i, 0)),
        scratch_shapes=[pltpu.VMEM((TOP_K, tt * nc, LANES), F32),
                        pltpu.SemaphoreType.DMA((COMBINE_PARTS,))],
        out_shape=jax.ShapeDtypeStruct((s, d), F32),
        compiler_params=_params(),
        name="moe_combine",
    )(dest_flat, h1, gates, row(g), row(b), y_disp)


def _tiles(s):
    t = lambda want: math.gcd(s, want)
    return dict(dense=t(512), attn=t(512), seq=t(512), route=t(512), expert=t(512), combine=t(512))


def kernel(x, ln_in_g, ln_in_b, w_in, lam_q1, lam_k1, lam_q2, lam_k2, subln_g, lru_conv_w, lru_conv_b, lru_wa, lru_ba, lru_wx, lru_bx, lru_lambda, cf_conv_w, cf_conv_b, cf_ln_g, cf_ln_b, w_out, ln1_g, ln1_b, router_w, router_b, moe_w1, moe_b1, moe_w2, moe_b2, ln2_g, ln2_b):
    bsz, s, d = x.shape
    assert bsz == 1
    tl = _tiles(s)
    tm = tl["expert"]
    n_blocks = (s * TOP_K) // tm + N_EXPERTS
    rows = n_blocks * tm

    h = _input_ln(x.reshape(s, d), ln_in_g, ln_in_b, tl["dense"])
    for l in range(DEPTH):
        lam_init = 0.8 - 0.6 * math.exp(-0.3 * l)
        qt, k, vt, rest = _inproj(h, w_in[l].astype(BF16), tl["attn"])
        att = _attention(qt, k, vt, lam_q1[l], lam_k1[l], lam_q2[l], lam_k2[l], subln_g[l],
                         lam_init)
        rc = _seqmix(rest, lru_conv_w[l], lru_conv_b[l], lru_wa[l], lru_ba[l], lru_wx[l],
                     lru_bx[l], lru_lambda[l], cf_conv_w[l], cf_conv_b[l], cf_ln_g[l], cf_ln_b[l],
                     tl["seq"])
        h1, h1c, idx, gates, rank, counts = _outproj_router(
            att, rc, h, w_out[l].astype(BF16), ln1_g[l], ln1_b[l], router_w[l], router_b[l],
            tl["route"])
        counts = counts.reshape(N_EXPERTS)
        padded = (counts + tm - 1) // tm * tm
        e_ids = jnp.arange(N_EXPERTS, dtype=jnp.int32)
        pend = jnp.sum(jnp.where(e_ids[None, :] <= e_ids[:, None], padded[None, :], 0),
                       axis=1).astype(jnp.int32)
        pstart = pend - padded
        block_row = jnp.arange(n_blocks, dtype=jnp.int32) * tm
        bexp = jnp.minimum(jnp.sum((pend[None, :] <= block_row[:, None]).astype(jnp.int32), axis=1),
                           N_EXPERTS - 1).astype(jnp.int32)
        nused = (pend[-1:] // tm).astype(jnp.int32)
        dest_flat = _dest_rows(idx, rank, pstart, tl["route"]).reshape(s * TOP_K)
        x_disp = _dispatch(h1c, dest_flat, pstart, pend, rows, s, tl["route"], tm)
        y_disp = _experts(x_disp, bexp, nused, moe_w1, moe_b1, moe_w2, moe_b2, l, tm)
        h = _combine(y_disp, h1, gates, dest_flat, ln2_g[l], ln2_b[l], tl["combine"])
    return h.reshape(bsz, s, d)
```

```python
import functools
import math

import jax
import jax.numpy as jnp
from jax import lax
from jax.experimental import pallas as pl
from jax.experimental.pallas import tpu as pltpu

F32 = jnp.float32
BF16 = jnp.bfloat16

DEPTH = 2
ATT_HEADS = 4
ATT_QK_DIM = 64
ATT_V_DIM = 128
LRU_BLOCKS = 4
LRU_CONV = 4
LRU_C = 8.0
CONV_KERNEL = 31
N_EXPERTS = 32
TOP_K = 4
SWIGLU_LIMIT = 7.0
SWIGLU_ALPHA = 1.702
LN_EPS = 1e-5
RESID_ALPHA = (2.0 * DEPTH) ** 0.25

VMEM_LIMIT_BYTES = 56 * 1024 * 1024
HALO = 32
LANES = 128
SUBLANES = 8
MXU_DIM = 256
LOG2E = math.log2(math.e)
V_PAD_ROWS = 16
Q_GROUP = 256
COMBINE_PARTS = 4

def _load_rows(ref, n_rows, d, lead=(), first_row=0):
    nc = d // LANES
    return jnp.concatenate([ref[lead + (pl.ds(first_row * nc + c, n_rows, stride=nc), slice(None))]
                            for c in range(nc)], axis=-1)


def _store_rows(ref, val):
    n_rows, d = val.shape
    nc = d // LANES
    for c in range(nc):
        ref[pl.ds(c, n_rows, stride=nc), :] = val[:, c * LANES:(c + 1) * LANES]


def _params(n_axes=1):
    return pltpu.CompilerParams(dimension_semantics=("arbitrary",) * n_axes,
                                vmem_limit_bytes=VMEM_LIMIT_BYTES)


def _layer_norm(x, g, b):
    mu = jnp.mean(x, axis=-1, keepdims=True)
    xc = x - mu
    var = jnp.mean(xc * xc, axis=-1, keepdims=True)
    return xc * lax.rsqrt(var + LN_EPS) * g + b


def _sigmoid(x):
    return 1.0 / (1.0 + jnp.exp(-x))


def _inproj_kernel(h_ref, g_ref, b_ref, w_ref, qt_ref, k_ref, vt_ref, rest_ref, *maybe_h_out,
                   scale):
    nh, hd = ATT_HEADS, ATT_V_DIM
    w_att = nh * hd
    tm = h_ref.shape[0]
    h = h_ref[...]
    if maybe_h_out:
        h = _layer_norm(h, g_ref[...], b_ref[...])
        maybe_h_out[0][...] = h
    hb = h.astype(BF16)
    q = jnp.dot(hb, w_ref[:, 0:w_att], preferred_element_type=F32) * scale
    k_ref[...] = jnp.dot(hb, w_ref[:, w_att:2 * w_att], preferred_element_type=F32).astype(BF16)
    v = jnp.dot(hb, w_ref[:, 2 * w_att:3 * w_att], preferred_element_type=F32)
    ones = jnp.ones((V_PAD_ROWS, tm), BF16)
    for h in range(nh):
        qt_ref[h] = q[:, h * hd:(h + 1) * hd].T.astype(BF16)
        vt_ref[h, 0:hd, :] = v[:, h * hd:(h + 1) * hd].T.astype(BF16)
        vt_ref[h, hd:hd + V_PAD_ROWS, :] = ones
    rest_ref[...] = jnp.dot(hb, w_ref[:, 3 * w_att:], preferred_element_type=F32)


def _inproj(h, ln_g, ln_b, w_bf16, tm, input_ln):
    s, d = h.shape
    n = w_bf16.shape[1]
    nh, hd = ATT_HEADS, ATT_V_DIM
    n_att = 3 * nh * hd
    kern = functools.partial(_inproj_kernel, scale=ATT_QK_DIM ** -0.5 * LOG2E)
    tposed = lambda r: pl.BlockSpec((nh, None, r, tm), lambda i: (0, i, 0, 0))
    rows = pl.BlockSpec((tm, d), lambda i: (i, 0))
    vec = pl.BlockSpec((1, d), lambda i: (0, 0))
    out_specs = [tposed(hd),
                 pl.BlockSpec((tm, nh * hd), lambda i: (i, 0)),
                 tposed(hd + V_PAD_ROWS),
                 pl.BlockSpec((tm, n - n_att), lambda i: (i, 0))]
    out_shape = [jax.ShapeDtypeStruct((nh, s // tm, hd, tm), BF16),
                 jax.ShapeDtypeStruct((s, nh * hd), BF16),
                 jax.ShapeDtypeStruct((nh, s // tm, hd + V_PAD_ROWS, tm), BF16),
                 jax.ShapeDtypeStruct((s, n - n_att), F32)]
    if input_ln:
        out_specs.append(rows)
        out_shape.append(jax.ShapeDtypeStruct((s, d), F32))
    return pl.pallas_call(
        kern,
        grid=(s // tm,),
        in_specs=[rows, vec, vec, pl.BlockSpec((d, n), lambda i: (0, 0))],
        out_specs=out_specs,
        out_shape=out_shape,
        compiler_params=_params(),
        name="inproj",
    )(h, ln_g.reshape(1, d).astype(F32), ln_b.reshape(1, d).astype(F32), w_bf16)


def _split3(x):
    hi = x.astype(BF16)
    rem = x - hi.astype(F32)
    mid = rem.astype(BF16)
    lo = (rem - mid.astype(F32)).astype(BF16)
    return hi, mid, lo


def _attn_kernel(slopes_ref, qt_ref, k_ref, vt_ref, lq1_ref, lk1_ref, lq2_ref, lk2_ref, g_ref,
                 o_ref, qs_ref, kc_ref, mask_ref, *stat_refs, tq, lam_init):
    h = pl.program_id(0)
    qi = pl.program_id(1)
    slope2 = slopes_ref[h]
    dk, hd = ATT_QK_DIM, ATT_V_DIM
    n_groups = (2 * tq) // Q_GROUP
    m_refs, acc_refs = stat_refs[:n_groups], stat_refs[n_groups:2 * n_groups]
    st_refs = stat_refs[2 * n_groups:]

    @pl.when(qi == 0)
    def _build_constants():
        c = lax.broadcasted_iota(jnp.int32, (tq, hd), 0)
        col = lax.broadcasted_iota(jnp.int32, (tq, hd), 1)
        c_lo = jnp.bitwise_and(c, MXU_DIM - 1)
        c_hi = c - c_lo
        kc = jnp.where(col < 3, c_hi, jnp.where(col < 6, c_lo, jnp.where(col < 9, 1, 0)))
        kc_ref[...] = kc.astype(F32).astype(BF16)
        r = lax.broadcasted_iota(jnp.int32, (hd, 2 * tq), 1)
        r = jnp.where(r >= tq, r - tq, r)
        row = lax.broadcasted_iota(jnp.int32, (hd, 2 * tq), 0)
        sl = jnp.full((hd, 2 * tq), slope2, F32)
        s_hi, s_mid, s_lo = _split3(sl)
        t_hi, t_mid, t_lo = _split3(-(sl * r.astype(F32)))
        aug = jnp.zeros((hd, 2 * tq), F32)
        for i, piece in enumerate((s_hi, s_mid, s_lo, s_hi, s_mid, s_lo, t_hi, t_mid, t_lo)):
            aug = jnp.where(row == i, piece.astype(F32), aug)
        qs_ref[hd:2 * hd, :] = aug.astype(BF16)
        ck = lax.broadcasted_iota(jnp.int32, (tq, 2 * tq), 0)
        rq = lax.broadcasted_iota(jnp.int32, (tq, 2 * tq), 1)
        rq = jnp.where(rq >= tq, rq - tq, rq)
        mask_ref[...] = jnp.where(ck <= rq, 0.0, -jnp.inf)

    qt = qt_ref[...]
    dim = lax.broadcasted_iota(jnp.int32, qt.shape, 0)
    zero_q = jnp.zeros_like(qt)
    qs_ref[0:hd, 0:tq] = jnp.where(dim < dk, qt, zero_q)
    qs_ref[0:hd, tq:2 * tq] = jnp.where(dim >= dk, qt, zero_q)
    for m_ref, acc_ref in zip(m_refs, acc_refs):
        m_ref[...] = jnp.full(m_ref.shape, -jnp.inf, F32)
        acc_ref[...] = jnp.zeros(acc_ref.shape, F32)

    cols = [slice(g * Q_GROUP, (g + 1) * Q_GROUP) for g in range(n_groups)]

    def keys_of(j):
        start = pl.multiple_of(j * tq, tq)
        return jnp.concatenate([k_ref[pl.ds(start, tq), :], kc_ref[...]], axis=1)

    def qk(ka, g):
        return jnp.dot(ka, qs_ref[:, cols[g]], preferred_element_type=F32)

    def softmax_pv(st, j, vt, g, diagonal):
        if diagonal:
            st = st + mask_ref[:, cols[g]]
        off = slope2 * ((j - qi) * tq).astype(F32)
        m_old = m_refs[g][...]
        m_new = jnp.maximum(m_old, jnp.max(st, axis=0, keepdims=True) + off)
        p = jnp.exp2(st - (m_new - off)).astype(BF16)
        alpha = jnp.exp2(m_old - m_new)
        acc_refs[g][...] = alpha * acc_refs[g][...] + jnp.dot(vt, p, preferred_element_type=F32)
        m_refs[g][...] = m_new

    def one_block(j_a, j_next, diagonal):
        vt_a = vt_ref[j_a]
        ka_n = keys_of(j_next) if j_next is not None else None
        for g in range(n_groups):
            softmax_pv(st_refs[g][...], j_a, vt_a, g, diagonal)
            if ka_n is not None:
                st_refs[g][...] = qk(ka_n, g)

    def two_blocks(j_a, j_b, j_next):
        vt_a, vt_b = vt_ref[j_a], vt_ref[j_b]
        ka_b, ka_n = keys_of(j_b), keys_of(j_next)
        s_b = {}
        for g in range(n_groups):
            softmax_pv(st_refs[g][...], j_a, vt_a, g, False)
            s_b[g] = qk(ka_b, g)
        for g in range(n_groups):
            softmax_pv(s_b.pop(g), j_b, vt_b, g, False)
            st_refs[g][...] = qk(ka_n, g)

    ka_0 = keys_of(0)
    for g in range(n_groups):
        st_refs[g][...] = qk(ka_0, g)

    def body(pair, carry):
        two_blocks(2 * pair, 2 * pair + 1, 2 * pair + 2)
        return carry

    lax.fori_loop(0, lax.shift_right_logical(qi, 1), body, 0)

    @pl.when(jnp.bitwise_and(qi, 1) == 1)
    def _odd_block():
        one_block(qi - 1, qi, False)

    one_block(qi, None, True)

    lam = (jnp.exp(jnp.sum(lq1_ref[...] * lk1_ref[...], axis=-1, keepdims=True))
           - jnp.exp(jnp.sum(lq2_ref[...] * lk2_ref[...], axis=-1, keepdims=True)) + lam_init)
    ot = jnp.concatenate([a[0:hd, :] / a[hd:hd + 1, :] for a in acc_refs], axis=1)
    o = ot[:, 0:tq] - lam * ot[:, tq:2 * tq]
    o = o * lax.rsqrt(jnp.mean(o * o, axis=0, keepdims=True) + LN_EPS)
    o_ref[...] = (o * g_ref[...] * (1.0 - lam_init)).T.astype(o_ref.dtype)


def _attention(qt, k, vt, lq1, lk1, lq2, lk2, subln_g, lam_init):
    nh, nblk, hd, tq = qt.shape
    vrows = vt.shape[2]
    s = k.shape[0]
    assert (2 * tq) % Q_GROUP == 0 and hd + 9 <= MXU_DIM
    n_groups = (2 * tq) // Q_GROUP
    slopes = jnp.exp2(-8.0 * (jnp.arange(nh, dtype=F32) + 1.0) / nh) * LOG2E
    vec = lambda a: a.reshape(1, -1).astype(F32)
    small = lambda n: pl.BlockSpec((1, n), lambda h, i, *_: (0, 0))
    kern = functools.partial(_attn_kernel, tq=tq, lam_init=lam_init)
    return pl.pallas_call(
        kern,
        grid_spec=pltpu.PrefetchScalarGridSpec(
            num_scalar_prefetch=1,
            grid=(nh, nblk),
            in_specs=[pl.BlockSpec((None, None, hd, tq), lambda h, i, *_: (h, i, 0, 0)),
                      pl.BlockSpec((s, hd), lambda h, i, *_: (0, h)),
                      pl.BlockSpec((None, nblk, vrows, tq), lambda h, i, *_: (h, 0, 0, 0)),
                      small(ATT_QK_DIM), small(ATT_QK_DIM), small(ATT_QK_DIM), small(ATT_QK_DIM),
                      pl.BlockSpec((hd, 1), lambda h, i, *_: (0, 0))],
            out_specs=pl.BlockSpec((tq, hd), lambda h, i, *_: (i, h)),
            scratch_shapes=[pltpu.VMEM((2 * hd, 2 * tq), BF16),
                            pltpu.VMEM((tq, hd), BF16),
                            pltpu.VMEM((tq, 2 * tq), F32)]
            + [pltpu.VMEM((1, Q_GROUP), F32)] * n_groups
            + [pltpu.VMEM((vrows, Q_GROUP), F32)] * n_groups
            + [pltpu.VMEM((tq, Q_GROUP), F32)] * n_groups),
        out_shape=jax.ShapeDtypeStruct((s, nh * hd), BF16),
        compiler_params=_params(2),
        name="diff_attention",
    )(slopes, qt, k, vt, vec(lq1), vec(lk1), vec(lq2), vec(lk2),
      subln_g.reshape(hd, 1).astype(F32))


def _seqmix_kernel(u_ref, lcw_ref, lcb_ref, wa_ref, ba_ref, wx_ref, bx_ref, lam_ref,
                   ccw_ref, ccb_ref, cg_ref, cb_ref, o_ref,
                   xbuf_ref, cbuf_ref, shift_ref, a_ref, b_ref, hs_ref, hc_ref, *, tt, w):
    i = pl.program_id(0)

    @pl.when(i == 0)
    def _init():
        xbuf_ref[0:HALO, :] = jnp.zeros((HALO, w), F32)
        cbuf_ref[0:HALO, :] = jnp.zeros((HALO, w), F32)
        hc_ref[...] = jnp.zeros(hc_ref.shape, F32)

    xbuf_ref[HALO:HALO + tt, :] = u_ref[:, 0:w]
    xc = jnp.zeros((tt, w), F32)
    for j in range(LRU_CONV):
        off = HALO - (LRU_CONV - 1) + j
        xc = xc + lcw_ref[j:j + 1, :] * xbuf_ref[off:off + tt, :]
    xc = xc + lcb_ref[...]
    xcb = xc.astype(BF16)
    gate_a = _sigmoid(jnp.dot(xcb, wa_ref[...], preferred_element_type=F32) + ba_ref[...])
    gate_x = _sigmoid(jnp.dot(xcb, wx_ref[...], preferred_element_type=F32) + bx_ref[...])
    nl = -lam_ref[...]
    softplus = jnp.maximum(nl, 0.0) + jnp.log(1.0 + jnp.exp(-jnp.abs(nl)))
    log_a = -LRU_C * gate_a * softplus
    a_ref[...] = jnp.exp(log_a)
    b_ref[...] = jnp.sqrt(1.0 - jnp.exp(2.0 * log_a)) * gate_x * xc

    def step(t, hprev):
        hnew = a_ref[pl.ds(t, 1), :] * hprev + b_ref[pl.ds(t, 1), :]
        hs_ref[pl.ds(t, 1), :] = hnew
        return hnew

    hc_ref[...] = lax.fori_loop(0, tt, step, hc_ref[...], unroll=8)
    o_ref[:, 0:w] = (hs_ref[...] * jax.nn.gelu(u_ref[:, w:2 * w], approximate=True)).astype(o_ref.dtype)
    xbuf_ref[0:HALO, :] = xbuf_ref[tt:tt + HALO, :]

    cbuf_ref[HALO:HALO + tt, :] = u_ref[:, 2 * w:3 * w] * _sigmoid(u_ref[:, 3 * w:4 * w])
    span = shift_ref.shape[1]
    for p in range(1, SUBLANES):
        shift_ref[p - 1] = cbuf_ref[p:p + span, :]
    y = jnp.zeros((tt, w), F32)
    for j in range(CONV_KERNEL):
        off = HALO - (CONV_KERNEL - 1) + j
        p, base = off % SUBLANES, off - off % SUBLANES
        win = cbuf_ref[base:base + tt, :] if p == 0 else shift_ref[p - 1, base:base + tt, :]
        y = y + ccw_ref[j:j + 1, :] * win
    y = _layer_norm(y + ccb_ref[...], cg_ref[...], cb_ref[...])
    o_ref[:, w:2 * w] = (y * _sigmoid(y)).astype(o_ref.dtype)
    cbuf_ref[0:HALO, :] = cbuf_ref[tt:tt + HALO, :]


def _block_diag(wb):
    nb, bd, _ = wb.shape
    eye = jnp.eye(nb, dtype=jnp.bool_)
    return jnp.where(eye[:, None, :, None], wb[:, :, None, :], 0.0).reshape(nb * bd, nb * bd)


def _seqmix(rest, lcw, lcb, wa, ba, wx, bx, lam, ccw, ccb, cg, cb, tt):
    s, n = rest.shape
    w = n // 4
    row = lambda a: a.reshape(1, w).astype(F32)
    full = lambda r, c: pl.BlockSpec((r, c), lambda i: (0, 0))
    kern = functools.partial(_seqmix_kernel, tt=tt, w=w)
    return pl.pallas_call(
        kern,
        grid=(s // tt,),
        in_specs=[pl.BlockSpec((tt, n), lambda i: (i, 0)),
                  full(LRU_CONV, w), full(1, w), full(w, w), full(1, w), full(w, w), full(1, w),
                  full(1, w), full(CONV_KERNEL, w), full(1, w), full(1, w), full(1, w)],
        out_specs=pl.BlockSpec((tt, 2 * w), lambda i: (i, 0)),
        out_shape=jax.ShapeDtypeStruct((s, 2 * w), BF16),
        scratch_shapes=[pltpu.VMEM((tt + HALO, w), F32), pltpu.VMEM((tt + HALO, w), F32),
                        pltpu.VMEM((SUBLANES - 1, tt + HALO - SUBLANES, w), F32),
                        pltpu.VMEM((tt, w), F32), pltpu.VMEM((tt, w), F32), pltpu.VMEM((tt, w), F32),
                        pltpu.VMEM((1, w), F32)],
        compiler_params=_params(),
        name="seqmix",
    )(rest, lcw, row(lcb), _block_diag(wa).astype(BF16), row(ba), _block_diag(wx).astype(BF16),
      row(bx), row(lam), ccw, row(ccb), row(cg), row(cb))


def _outproj_kernel(att_ref, rc_ref, h_ref, wo_ref, g_ref, b_ref, rwh_ref, rwl_ref, rb_ref,
                    h1_ref, h1c_ref, idx_ref, gate_ref, rank_ref, cnt_ref, tri_ref, carry_ref,
                    *, tm, n_att):
    i = pl.program_id(0)
    ne = N_EXPERTS

    @pl.when(i == 0)
    def _init():
        r = lax.broadcasted_iota(jnp.int32, (tm, tm), 0)
        c = lax.broadcasted_iota(jnp.int32, (tm, tm), 1)
        tri_ref[...] = jnp.where(c < r, 1.0, 0.0).astype(BF16)
        carry_ref[...] = jnp.zeros(carry_ref.shape, F32)

    mix = (jnp.dot(att_ref[...], wo_ref[0:n_att, :], preferred_element_type=F32)
           + jnp.dot(rc_ref[...], wo_ref[n_att:, :], preferred_element_type=F32))
    h1 = _layer_norm(RESID_ALPHA * h_ref[...] + mix, g_ref[...], b_ref[...])
    h1_ref[...] = h1
    _store_rows(h1c_ref, h1)

    h1_hi = h1.astype(BF16)
    h1_lo = (h1 - h1_hi.astype(F32)).astype(BF16)
    logits = (jnp.dot(h1_hi, rwh_ref[...], preferred_element_type=F32)
              + jnp.dot(h1_hi, rwl_ref[...], preferred_element_type=F32)
              + jnp.dot(h1_lo, rwh_ref[...], preferred_element_type=F32)) + rb_ref[...]
    lane = lax.broadcasted_iota(jnp.int32, (tm, ne), 1).astype(F32)
    onehot = jnp.zeros((tm, ne), F32)
    vals, sels = [], []
    for _ in range(TOP_K):
        mx = jnp.max(logits, axis=-1, keepdims=True)
        sel = jnp.min(jnp.where(logits == mx, lane, float(ne)), axis=-1, keepdims=True)
        hit = lane == sel
        onehot = onehot + jnp.where(hit, 1.0, 0.0)
        logits = jnp.where(hit, -jnp.inf, logits)
        vals.append(mx)
        sels.append(sel)
    ex = [jnp.exp(v - vals[0]) for v in vals]
    den = ex[0] + ex[1] + ex[2] + ex[3]
    before = jnp.dot(tri_ref[...], onehot.astype(BF16), preferred_element_type=F32) + carry_ref[...]
    for k in range(TOP_K):
        idx_ref[:, k:k + 1] = sels[k].astype(jnp.int32)
        gate_ref[:, k:k + 1] = ex[k] / den
        rank_ref[:, k:k + 1] = jnp.sum(jnp.where(lane == sels[k], before, 0.0), axis=-1,
                                       keepdims=True).astype(jnp.int32)
    carry_ref[...] = carry_ref[...] + jnp.sum(onehot, axis=0, keepdims=True)
    cnt_ref[...] = carry_ref[...].astype(jnp.int32)


def _outproj_router(att, rc, h, wo_bf16, g, b, rw, rb, tm):
    s, d = h.shape
    n_att = att.shape[1]
    ne = N_EXPERTS
    row = lambda a: a.reshape(1, -1).astype(F32)
    full = lambda r, c: pl.BlockSpec((r, c), lambda i: (0, 0))
    tile = lambda c: pl.BlockSpec((tm, c), lambda i: (i, 0))
    kern = functools.partial(_outproj_kernel, tm=tm, n_att=n_att)
    rw_hi = rw.astype(BF16)
    rw_lo = (rw - rw_hi.astype(F32)).astype(BF16)
    return pl.pallas_call(
        kern,
        grid=(s // tm,),
        in_specs=[tile(n_att), tile(rc.shape[1]), tile(d), full(wo_bf16.shape[0], d),
                  full(1, d), full(1, d), full(d, ne), full(d, ne), full(1, ne)],
        out_specs=[tile(d), pl.BlockSpec((tm * (d // LANES), LANES), lambda i: (i, 0)),
                   tile(TOP_K), tile(TOP_K), tile(TOP_K), full(1, ne)],
        out_shape=[jax.ShapeDtypeStruct((s, d), F32),
                   jax.ShapeDtypeStruct((s * (d // LANES), LANES), F32),
                   jax.ShapeDtypeStruct((s, TOP_K), jnp.int32),
                   jax.ShapeDtypeStruct((s, TOP_K), F32),
                   jax.ShapeDtypeStruct((s, TOP_K), jnp.int32),
                   jax.ShapeDtypeStruct((1, ne), jnp.int32)],
        scratch_shapes=[pltpu.VMEM((tm, tm), BF16), pltpu.VMEM((1, ne), F32)],
        compiler_params=_params(),
        name="outproj_router",
    )(att, rc, h, wo_bf16, row(g), row(b), rw_hi, rw_lo, row(rb))


def _row_copy(src_ref, src_row, dst_ref, dst_row, sem, nc):
    src = src_ref.at[pl.ds(pl.multiple_of(src_row * nc, nc), nc)]
    dst = dst_ref.at[pl.ds(pl.multiple_of(dst_row * nc, nc), nc)]
    return pltpu.make_async_copy(src, dst, sem)


def _dest_kernel(idx_ref, rank_ref, pstart_ref, dest_ref):
    tm, ne = idx_ref.shape[0], pstart_ref.shape[1]
    lane = lax.broadcasted_iota(jnp.int32, (tm, ne), 1)
    pstart = pstart_ref[...].astype(F32)
    for k in range(TOP_K):
        hit = lane == idx_ref[:, k:k + 1]
        start = jnp.sum(jnp.where(hit, pstart, 0.0), axis=-1, keepdims=True)
        dest_ref[:, k:k + 1] = start.astype(jnp.int32) + rank_ref[:, k:k + 1]


def _dest_rows(idx, rank, pstart, tm):
    s = idx.shape[0]
    ne = pstart.shape[0]
    tile = pl.BlockSpec((tm, TOP_K), lambda i: (i, 0))
    return pl.pallas_call(
        _dest_kernel,
        grid=(s // tm,),
        in_specs=[tile, tile, pl.BlockSpec((1, ne), lambda i: (0, 0))],
        out_specs=tile,
        out_shape=jax.ShapeDtypeStruct((s, TOP_K), jnp.int32),
        compiler_params=_params(),
        name="moe_dest_rows",
    )(idx, rank, pstart.reshape(1, ne))


def _dispatch_kernel(pstart_ref, pend_ref, dest_ref, h_ref, xd_ref, zero_ref, sem, zsem,
                     *, tt, tm, nc, n_blocks):
    i = pl.program_id(0)

    def zero_copy(e):
        start = pl.multiple_of((pend_ref[e] - tm) * nc, tm * nc)
        return pltpu.make_async_copy(zero_ref, xd_ref.at[pl.ds(start, tm * nc)], zsem)

    def tail_copy(b):
        start = pl.multiple_of(b * (tm * nc), tm * nc)
        return pltpu.make_async_copy(zero_ref, xd_ref.at[pl.ds(start, tm * nc)], zsem)

    @pl.when(i == 0)
    def _clear_padding():
        zero_ref[...] = jnp.zeros(zero_ref.shape, zero_ref.dtype)
        n_used = pend_ref[N_EXPERTS - 1] // tm

        def start_tail(b, carry):
            tail_copy(b).start()
            return carry

        def wait_tail(b, carry):
            tail_copy(b).wait()
            return carry

        lax.fori_loop(n_used, n_blocks, start_tail, 0)
        lax.fori_loop(n_used, n_blocks, wait_tail, 0)
        for e in range(N_EXPERTS):
            @pl.when(pend_ref[e] > pstart_ref[e])
            def _():
                zero_copy(e).start()
        for e in range(N_EXPERTS):
            @pl.when(pend_ref[e] > pstart_ref[e])
            def _():
                zero_copy(e).wait()

    def issue(t, carry):
        for k in range(TOP_K):
            _row_copy(h_ref, t, xd_ref, dest_ref[t * TOP_K + k], sem, nc).start(priority=k % 2)
        return carry

    lax.fori_loop(0, tt, issue, 0)

    for k in range(TOP_K):
        pltpu.make_async_copy(h_ref, xd_ref.at[pl.ds(0, tt * nc)], sem).wait()


def _dispatch(h1c, dest_flat, pstart, pend, rows, s, tt, tm):
    nc = h1c.shape[0] // s
    smem = lambda n: pl.BlockSpec((n,), lambda i, *_: (i,), memory_space=pltpu.SMEM)
    kern = functools.partial(_dispatch_kernel, tt=tt, tm=tm, nc=nc, n_blocks=rows // tm)
    return pl.pallas_call(
        kern,
        grid_spec=pltpu.PrefetchScalarGridSpec(
            num_scalar_prefetch=2,
            grid=(s // tt,),
            in_specs=[smem(tt * TOP_K),
                      pl.BlockSpec((tt * nc, LANES), lambda i, *_: (i, 0))],
            out_specs=pl.BlockSpec(memory_space=pl.ANY),
            scratch_shapes=[pltpu.VMEM((tm * nc, LANES), F32), pltpu.SemaphoreType.DMA(()),
                            pltpu.SemaphoreType.DMA(())]),
        out_shape=jax.ShapeDtypeStruct((rows * nc, LANES), F32),
        compiler_params=_params(),
        name="moe_dispatch",
    )(pstart, pend, dest_flat, h1c)


def _expert_kernel(bexp_ref, nused_ref, x_ref, w1_ref, b1_ref, w2_ref, b2_ref, y_ref,
                   w1b_ref, w2b_ref, *, f, chunk, tm):
    b = pl.program_id(0)
    last = nused_ref[0] - 1
    e = bexp_ref[jnp.minimum(b, last)]
    e_prev = bexp_ref[jnp.maximum(jnp.minimum(b, last) - 1, 0)]

    @pl.when((b == 0) | (e != e_prev))
    def _cast_weights():
        def cast(c, carry):
            r = pl.multiple_of(c * chunk, chunk)
            w1b_ref[pl.ds(r, chunk), :] = w1_ref[pl.ds(r, chunk), :].astype(BF16)
            w2b_ref[pl.ds(r, chunk), :] = w2_ref[pl.ds(r, chunk), :].astype(BF16)
            return carry
        lax.fori_loop(0, w1_ref.shape[0] // chunk, cast, 0)

    @pl.when(b > last)
    def _unused_block():
        y_ref[...] = jnp.zeros(y_ref.shape, y_ref.dtype)

    @pl.when(b <= last)
    def _mlp():
        x = _load_rows(x_ref, tm, w1_ref.shape[0]).astype(BF16)
        hdn = jnp.dot(x, w1b_ref[...], preferred_element_type=F32) + b1_ref[...]
        glu = jnp.minimum(hdn[:, 0:f], SWIGLU_LIMIT)
        lin = jnp.clip(hdn[:, f:2 * f], -SWIGLU_LIMIT, SWIGLU_LIMIT)
        act = glu * _sigmoid(SWIGLU_ALPHA * glu) * (lin + 1.0)
        _store_rows(y_ref, jnp.dot(act.astype(BF16), w2b_ref[...], preferred_element_type=F32)
                    + b2_ref[...])


def _experts(x_disp, bexp, nused, w1, b1, w2, b2, layer, tm):
    d = w1.shape[2]
    f = w2.shape[2]
    nc = d // LANES
    rows = x_disp.shape[0] // nc
    assert f == d, "the weight-cast loop assumes d_expert == d_model"

    def blk(b, bexp_ref, nused_ref):
        return jnp.minimum(b, nused_ref[0] - 1)

    def wmap(b, bexp_ref, nused_ref):
        return (layer, bexp_ref[blk(b, bexp_ref, nused_ref)], 0, 0)

    def bmap(b, bexp_ref, nused_ref):
        return (layer, bexp_ref[blk(b, bexp_ref, nused_ref)], 0, 0)

    kern = functools.partial(_expert_kernel, f=f, chunk=128, tm=tm)
    return pl.pallas_call(
        kern,
        grid_spec=pltpu.PrefetchScalarGridSpec(
            num_scalar_prefetch=2,
            grid=(rows // tm,),
            in_specs=[pl.BlockSpec((tm * nc, LANES), lambda b, be, nu: (b, 0)),
                      pl.BlockSpec((None, None, d, 2 * f), wmap),
                      pl.BlockSpec((None, None, 1, 2 * f), bmap),
                      pl.BlockSpec((None, None, f, d), wmap),
                      pl.BlockSpec((None, None, 1, d), bmap)],
            out_specs=pl.BlockSpec((tm * nc, LANES), lambda b, be, nu: (b, 0)),
            scratch_shapes=[pltpu.VMEM((d, 2 * f), BF16), pltpu.VMEM((f, d), BF16)]),
        out_shape=jax.ShapeDtypeStruct((rows * nc, LANES), F32),
        compiler_params=_params(),
        name="moe_experts",
    )(bexp, nused, x_disp, w1, b1.reshape(b1.shape[0], b1.shape[1], 1, -1), w2,
      b2.reshape(b2.shape[0], b2.shape[1], 1, -1))


def _combine_kernel(dest_ref, h_ref, gate_ref, g_ref, b_ref, yd_ref, o_ref,
                    ybuf_ref, sems, *, tt, nc):
    th = tt // COMBINE_PARTS
    d = h_ref.shape[1]

    def issue(part):
        def one(t, carry):
            for k in range(TOP_K):
                _row_copy(yd_ref, dest_ref[t * TOP_K + k], ybuf_ref.at[k], t, sems.at[part],
                          nc).start(priority=k % 2)
            return carry
        lax.fori_loop(part * th, (part + 1) * th, one, 0)

    def finish(part):
        for k in range(TOP_K):
            pltpu.make_async_copy(yd_ref.at[pl.ds(0, th * nc)],
                                  ybuf_ref.at[k, pl.ds(part * th * nc, th * nc)],
                                  sems.at[part]).wait()
        rows = slice(part * th, (part + 1) * th)
        ff = gate_ref[rows, 0:1] * _load_rows(ybuf_ref, th, d, (0,), part * th)
        for k in range(1, TOP_K):
            ff = ff + gate_ref[rows, k:k + 1] * _load_rows(ybuf_ref, th, d, (k,), part * th)
        o_ref[rows, :] = _layer_norm(RESID_ALPHA * h_ref[rows, :] + ff, g_ref[...], b_ref[...])

    issue(0)
    for part in range(COMBINE_PARTS):
        if part + 1 < COMBINE_PARTS:
            issue(part + 1)
        finish(part)


def _combine(y_disp, h1, gates, dest_flat, g, b, tt):
    s, d = h1.shape
    nc = d // LANES
    row = lambda a: a.reshape(1, -1).astype(F32)
    kern = functools.partial(_combine_kernel, tt=tt, nc=nc)
    return pl.pallas_call(
        kern,
        grid=(s // tt,),
        in_specs=[pl.BlockSpec((tt * TOP_K,), lambda i: (i,), memory_space=pltpu.SMEM),
                  pl.BlockSpec((tt, d), lambda i: (i, 0)),
                  pl.BlockSpec((tt, TOP_K), lambda i: (i, 0)),
                  pl.BlockSpec((1, d), lambda i: (0, 0)),
                  pl.BlockSpec((1, d), lambda i: (0, 0)),
                  pl.BlockSpec(memory_space=pl.ANY)],
        out_specs=pl.BlockSpec((tt, d), lambda i: (i, 0)),
        scratch_shapes=[pltpu.VMEM((TOP_K, tt * nc, LANES), F32),
                        pltpu.SemaphoreType.DMA((COMBINE_PARTS,))],
        out_shape=jax.ShapeDtypeStruct((s, d), F32),
        compiler_params=_params(),
        name="moe_combine",
    )(dest_flat, h1, gates, row(g), row(b), y_disp)


def _tiles(s):
    t = lambda want: math.gcd(s, want)
    return dict(attn=t(512), seq=t(512), route=t(512), expert=t(512), combine=t(512))


def kernel(x, ln_in_g, ln_in_b, w_in, lam_q1, lam_k1, lam_q2, lam_k2, subln_g, lru_conv_w, lru_conv_b, lru_wa, lru_ba, lru_wx, lru_bx, lru_lambda, cf_conv_w, cf_conv_b, cf_ln_g, cf_ln_b, w_out, ln1_g, ln1_b, router_w, router_b, moe_w1, moe_b1, moe_w2, moe_b2, ln2_g, ln2_b):
    bsz, s, d = x.shape
    assert bsz == 1
    tl = _tiles(s)
    tm = tl["expert"]
    n_blocks = (s * TOP_K) // tm + N_EXPERTS
    rows = n_blocks * tm

    h = x.reshape(s, d)
    for l in range(DEPTH):
        lam_init = 0.8 - 0.6 * math.exp(-0.3 * l)
        proj = _inproj(h, ln_in_g, ln_in_b, w_in[l].astype(BF16), tl["attn"], input_ln=(l == 0))
        qt, k, vt, rest = proj[:4]
        if l == 0:
            h = proj[4]
        att = _attention(qt, k, vt, lam_q1[l], lam_k1[l], lam_q2[l], lam_k2[l], subln_g[l],
                         lam_init)
        rc = _seqmix(rest, lru_conv_w[l], lru_conv_b[l], lru_wa[l], lru_ba[l], lru_wx[l],
                     lru_bx[l], lru_lambda[l], cf_conv_w[l], cf_conv_b[l], cf_ln_g[l], cf_ln_b[l],
                     tl["seq"])
        h1, h1c, idx, gates, rank, counts = _outproj_router(
            att, rc, h, w_out[l].astype(BF16), ln1_g[l], ln1_b[l], router_w[l], router_b[l],
            tl["route"])
        counts = counts.reshape(N_EXPERTS)
        padded = (counts + tm - 1) // tm * tm
        e_ids = jnp.arange(N_EXPERTS, dtype=jnp.int32)
        pend = jnp.sum(jnp.where(e_ids[None, :] <= e_ids[:, None], padded[None, :], 0),
                       axis=1).astype(jnp.int32)
        pstart = pend - padded
        block_row = jnp.arange(n_blocks, dtype=jnp.int32) * tm
        bexp = jnp.minimum(jnp.sum((pend[None, :] <= block_row[:, None]).astype(jnp.int32), axis=1),
                           N_EXPERTS - 1).astype(jnp.int32)
        nused = (pend[-1:] // tm).astype(jnp.int32)
        dest_flat = _dest_rows(idx, rank, pstart, tl["route"]).reshape(s * TOP_K)
        x_disp = _dispatch(h1c, dest_flat, pstart, pend, rows, s, tl["route"], tm)
        y_disp = _experts(x_disp, bexp, nused, moe_w1, moe_b1, moe_w2, moe_b2, l, tm)
        h = _combine(y_disp, h1, gates, dest_flat, ln2_g[l], ln2_b[l], tl["combine"])
    return h.reshape(bsz, s, d)
```

```python
import functools
import math

import jax
import jax.numpy as jnp
from jax import lax
from jax.experimental import pallas as pl
from jax.experimental.pallas import tpu as pltpu

F32 = jnp.float32
BF16 = jnp.bfloat16

DEPTH = 2
ATT_HEADS = 4
ATT_QK_DIM = 64
ATT_V_DIM = 128
LRU_BLOCKS = 4
LRU_CONV = 4
LRU_C = 8.0
CONV_KERNEL = 31
N_EXPERTS = 32
TOP_K = 4
SWIGLU_LIMIT = 7.0
SWIGLU_ALPHA = 1.702
LN_EPS = 1e-5
RESID_ALPHA = (2.0 * DEPTH) ** 0.25

VMEM_LIMIT_BYTES = 56 * 1024 * 1024
HALO = 32
LANES = 128
SUBLANES = 8
MXU_DIM = 256
LOG2E = math.log2(math.e)
V_PAD_ROWS = 16
Q_GROUP = 256
COMBINE_PARTS = 4

def _load_rows(ref, n_rows, d, lead=(), first_row=0):
    nc = d // LANES
    return jnp.concatenate([ref[lead + (pl.ds(first_row * nc + c, n_rows, stride=nc), slice(None))]
                            for c in range(nc)], axis=-1)


def _store_rows(ref, val):
    n_rows, d = val.shape
    nc = d // LANES
    for c in range(nc):
        ref[pl.ds(c, n_rows, stride=nc), :] = val[:, c * LANES:(c + 1) * LANES]


def _params(n_axes=1):
    return pltpu.CompilerParams(dimension_semantics=("arbitrary",) * n_axes,
                                vmem_limit_bytes=VMEM_LIMIT_BYTES)


def _layer_norm(x, g, b):
    mu = jnp.mean(x, axis=-1, keepdims=True)
    xc = x - mu
    var = jnp.mean(xc * xc, axis=-1, keepdims=True)
    return xc * lax.rsqrt(var + LN_EPS) * g + b


def _sigmoid(x):
    return 1.0 / (1.0 + jnp.exp(-x))


def _inproj_kernel(h_ref, g_ref, b_ref, w_ref, qt_ref, k_ref, vt_ref, rest_ref, *maybe_h_out,
                   scale):
    nh, hd = ATT_HEADS, ATT_V_DIM
    w_att = nh * hd
    tm = h_ref.shape[0]
    h = h_ref[...]
    if maybe_h_out:
        h = _layer_norm(h, g_ref[...], b_ref[...])
        maybe_h_out[0][...] = h
    hb = h.astype(BF16)
    q = jnp.dot(hb, w_ref[:, 0:w_att], preferred_element_type=F32) * scale
    k_ref[...] = jnp.dot(hb, w_ref[:, w_att:2 * w_att], preferred_element_type=F32).astype(BF16)
    v = jnp.dot(hb, w_ref[:, 2 * w_att:3 * w_att], preferred_element_type=F32)
    ones = jnp.ones((V_PAD_ROWS, tm), BF16)
    for h in range(nh):
        qt_ref[h] = q[:, h * hd:(h + 1) * hd].T.astype(BF16)
        vt_ref[h, 0:hd, :] = v[:, h * hd:(h + 1) * hd].T.astype(BF16)
        vt_ref[h, hd:hd + V_PAD_ROWS, :] = ones
    rest_ref[...] = jnp.dot(hb, w_ref[:, 3 * w_att:], preferred_element_type=F32)


def _inproj(h, ln_g, ln_b, w_bf16, tm, input_ln):
    s, d = h.shape
    n = w_bf16.shape[1]
    nh, hd = ATT_HEADS, ATT_V_DIM
    n_att = 3 * nh * hd
    kern = functools.partial(_inproj_kernel, scale=ATT_QK_DIM ** -0.5 * LOG2E)
    tposed = lambda r: pl.BlockSpec((nh, None, r, tm), lambda i: (0, i, 0, 0))
    rows = pl.BlockSpec((tm, d), lambda i: (i, 0))
    vec = pl.BlockSpec((1, d), lambda i: (0, 0))
    out_specs = [tposed(hd),
                 pl.BlockSpec((tm, nh * hd), lambda i: (i, 0)),
                 tposed(hd + V_PAD_ROWS),
                 pl.BlockSpec((tm, n - n_att), lambda i: (i, 0))]
    out_shape = [jax.ShapeDtypeStruct((nh, s // tm, hd, tm), BF16),
                 jax.ShapeDtypeStruct((s, nh * hd), BF16),
                 jax.ShapeDtypeStruct((nh, s // tm, hd + V_PAD_ROWS, tm), BF16),
                 jax.ShapeDtypeStruct((s, n - n_att), F32)]
    if input_ln:
        out_specs.append(rows)
        out_shape.append(jax.ShapeDtypeStruct((s, d), F32))
    return pl.pallas_call(
        kern,
        grid=(s // tm,),
        in_specs=[rows, vec, vec, pl.BlockSpec((d, n), lambda i: (0, 0))],
        out_specs=out_specs,
        out_shape=out_shape,
        compiler_params=_params(),
        name="inproj",
    )(h, ln_g.reshape(1, d).astype(F32), ln_b.reshape(1, d).astype(F32), w_bf16)


def _split3(x):
    hi = x.astype(BF16)
    rem = x - hi.astype(F32)
    mid = rem.astype(BF16)
    lo = (rem - mid.astype(F32)).astype(BF16)
    return hi, mid, lo


def _attn_kernel(slopes_ref, qt_ref, k_ref, vt_ref, lq1_ref, lk1_ref, lq2_ref, lk2_ref, g_ref,
                 o_ref, qs_ref, kc_ref, mask_ref, *stat_refs, tq, lam_init):
    h = pl.program_id(0)
    qi = pl.program_id(1)
    slope2 = slopes_ref[h]
    dk, hd = ATT_QK_DIM, ATT_V_DIM
    n_groups = (2 * tq) // Q_GROUP
    m_refs, acc_refs = stat_refs[:n_groups], stat_refs[n_groups:2 * n_groups]
    st_refs = stat_refs[2 * n_groups:3 * n_groups]
    mx_refs = stat_refs[3 * n_groups:]

    @pl.when(qi == 0)
    def _build_constants():
        c = lax.broadcasted_iota(jnp.int32, (tq, hd), 0)
        col = lax.broadcasted_iota(jnp.int32, (tq, hd), 1)
        c_lo = jnp.bitwise_and(c, MXU_DIM - 1)
        c_hi = c - c_lo
        kc = jnp.where(col < 3, c_hi, jnp.where(col < 6, c_lo, jnp.where(col < 9, 1, 0)))
        kc_ref[...] = kc.astype(F32).astype(BF16)
        r = lax.broadcasted_iota(jnp.int32, (hd, 2 * tq), 1)
        r = jnp.where(r >= tq, r - tq, r)
        row = lax.broadcasted_iota(jnp.int32, (hd, 2 * tq), 0)
        sl = jnp.full((hd, 2 * tq), slope2, F32)
        s_hi, s_mid, s_lo = _split3(sl)
        t_hi, t_mid, t_lo = _split3(-(sl * r.astype(F32)))
        aug = jnp.zeros((hd, 2 * tq), F32)
        for i, piece in enumerate((s_hi, s_mid, s_lo, s_hi, s_mid, s_lo, t_hi, t_mid, t_lo)):
            aug = jnp.where(row == i, piece.astype(F32), aug)
        qs_ref[hd:2 * hd, :] = aug.astype(BF16)
        ck = lax.broadcasted_iota(jnp.int32, (tq, 2 * tq), 0)
        rq = lax.broadcasted_iota(jnp.int32, (tq, 2 * tq), 1)
        rq = jnp.where(rq >= tq, rq - tq, rq)
        mask_ref[...] = jnp.where(ck <= rq, 0.0, -jnp.inf)

    qt = qt_ref[...]
    dim = lax.broadcasted_iota(jnp.int32, qt.shape, 0)
    zero_q = jnp.zeros_like(qt)
    qs_ref[0:hd, 0:tq] = jnp.where(dim < dk, qt, zero_q)
    qs_ref[0:hd, tq:2 * tq] = jnp.where(dim >= dk, qt, zero_q)
    for m_ref, acc_ref in zip(m_refs, acc_refs):
        m_ref[...] = jnp.full(m_ref.shape, -jnp.inf, F32)
        acc_ref[...] = jnp.zeros(acc_ref.shape, F32)

    cols = [slice(g * Q_GROUP, (g + 1) * Q_GROUP) for g in range(n_groups)]

    def keys_of(j):
        start = pl.multiple_of(j * tq, tq)
        return jnp.concatenate([k_ref[pl.ds(start, tq), :], kc_ref[...]], axis=1)

    def qk(ka, g):
        st = jnp.dot(ka, qs_ref[:, cols[g]], preferred_element_type=F32)
        return st, jnp.max(st, axis=0, keepdims=True)

    def prefetch(ka, g):
        st_refs[g][...], mx_refs[g][...] = qk(ka, g)

    def softmax_pv(st, mx, j, vt, g, diagonal):
        if diagonal:
            st = st + mask_ref[:, cols[g]]
            mx = jnp.max(st, axis=0, keepdims=True)
        off = slope2 * ((j - qi) * tq).astype(F32)
        m_old = m_refs[g][...]
        m_new = jnp.maximum(m_old, mx + off)
        p = jnp.exp2(st - (m_new - off)).astype(BF16)
        alpha = jnp.exp2(m_old - m_new)
        acc_refs[g][...] = alpha * acc_refs[g][...] + jnp.dot(vt, p, preferred_element_type=F32)
        m_refs[g][...] = m_new

    def one_block(j_a, j_next, diagonal):
        vt_a = vt_ref[j_a]
        ka_n = keys_of(j_next) if j_next is not None else None
        for g in range(n_groups):
            softmax_pv(st_refs[g][...], mx_refs[g][...], j_a, vt_a, g, diagonal)
            if ka_n is not None:
                prefetch(ka_n, g)

    def two_blocks(j_a, j_b, j_next):
        vt_a, vt_b = vt_ref[j_a], vt_ref[j_b]
        ka_b, ka_n = keys_of(j_b), keys_of(j_next)
        s_b = {}
        for g in range(n_groups):
            softmax_pv(st_refs[g][...], mx_refs[g][...], j_a, vt_a, g, False)
            s_b[g] = qk(ka_b, g)
        for g in range(n_groups):
            softmax_pv(*s_b.pop(g), j_b, vt_b, g, False)
            prefetch(ka_n, g)

    ka_0 = keys_of(0)
    for g in range(n_groups):
        prefetch(ka_0, g)

    def body(pair, carry):
        two_blocks(2 * pair, 2 * pair + 1, 2 * pair + 2)
        return carry

    lax.fori_loop(0, lax.shift_right_logical(qi, 1), body, 0)

    @pl.when(jnp.bitwise_and(qi, 1) == 1)
    def _odd_block():
        one_block(qi - 1, qi, False)

    one_block(qi, None, True)

    lam = (jnp.exp(jnp.sum(lq1_ref[...] * lk1_ref[...], axis=-1, keepdims=True))
           - jnp.exp(jnp.sum(lq2_ref[...] * lk2_ref[...], axis=-1, keepdims=True)) + lam_init)
    ot = jnp.concatenate([a[0:hd, :] / a[hd:hd + 1, :] for a in acc_refs], axis=1)
    o = ot[:, 0:tq] - lam * ot[:, tq:2 * tq]
    o = o * lax.rsqrt(jnp.mean(o * o, axis=0, keepdims=True) + LN_EPS)
    o_ref[...] = (o * g_ref[...] * (1.0 - lam_init)).T.astype(o_ref.dtype)


def _attention(qt, k, vt, lq1, lk1, lq2, lk2, subln_g, lam_init):
    nh, nblk, hd, tq = qt.shape
    vrows = vt.shape[2]
    s = k.shape[0]
    assert (2 * tq) % Q_GROUP == 0 and hd + 9 <= MXU_DIM
    n_groups = (2 * tq) // Q_GROUP
    slopes = jnp.exp2(-8.0 * (jnp.arange(nh, dtype=F32) + 1.0) / nh) * LOG2E
    vec = lambda a: a.reshape(1, -1).astype(F32)
    small = lambda n: pl.BlockSpec((1, n), lambda h, i, *_: (0, 0))
    kern = functools.partial(_attn_kernel, tq=tq, lam_init=lam_init)
    return pl.pallas_call(
        kern,
        grid_spec=pltpu.PrefetchScalarGridSpec(
            num_scalar_prefetch=1,
            grid=(nh, nblk),
            in_specs=[pl.BlockSpec((None, None, hd, tq), lambda h, i, *_: (h, i, 0, 0)),
                      pl.BlockSpec((s, hd), lambda h, i, *_: (0, h)),
                      pl.BlockSpec((None, nblk, vrows, tq), lambda h, i, *_: (h, 0, 0, 0)),
                      small(ATT_QK_DIM), small(ATT_QK_DIM), small(ATT_QK_DIM), small(ATT_QK_DIM),
                      pl.BlockSpec((hd, 1), lambda h, i, *_: (0, 0))],
            out_specs=pl.BlockSpec((tq, hd), lambda h, i, *_: (i, h)),
            scratch_shapes=[pltpu.VMEM((2 * hd, 2 * tq), BF16),
                            pltpu.VMEM((tq, hd), BF16),
                            pltpu.VMEM((tq, 2 * tq), F32)]
            + [pltpu.VMEM((1, Q_GROUP), F32)] * n_groups
            + [pltpu.VMEM((vrows, Q_GROUP), F32)] * n_groups
            + [pltpu.VMEM((tq, Q_GROUP), F32)] * n_groups
            + [pltpu.VMEM((1, Q_GROUP), F32)] * n_groups),
        out_shape=jax.ShapeDtypeStruct((s, nh * hd), BF16),
        compiler_params=_params(2),
        name="diff_attention",
    )(slopes, qt, k, vt, vec(lq1), vec(lk1), vec(lq2), vec(lk2),
      subln_g.reshape(hd, 1).astype(F32))


def _seqmix_kernel(u_ref, lcw_ref, lcb_ref, wa_ref, ba_ref, wx_ref, bx_ref, lam_ref,
                   ccw_ref, ccb_ref, cg_ref, cb_ref, o_ref,
                   xbuf_ref, cbuf_ref, shift_ref, a_ref, b_ref, hs_ref, hc_ref, *, tt, w):
    i = pl.program_id(0)

    @pl.when(i == 0)
    def _init():
        xbuf_ref[0:HALO, :] = jnp.zeros((HALO, w), F32)
        cbuf_ref[0:HALO, :] = jnp.zeros((HALO, w), F32)
        hc_ref[...] = jnp.zeros(hc_ref.shape, F32)

    xbuf_ref[HALO:HALO + tt, :] = u_ref[:, 0:w]
    xc = jnp.zeros((tt, w), F32)
    for j in range(LRU_CONV):
        off = HALO - (LRU_CONV - 1) + j
        xc = xc + lcw_ref[j:j + 1, :] * xbuf_ref[off:off + tt, :]
    xc = xc + lcb_ref[...]
    xcb = xc.astype(BF16)
    gate_a = _sigmoid(jnp.dot(xcb, wa_ref[...], preferred_element_type=F32) + ba_ref[...])
    gate_x = _sigmoid(jnp.dot(xcb, wx_ref[...], preferred_element_type=F32) + bx_ref[...])
    nl = -lam_ref[...]
    softplus = jnp.maximum(nl, 0.0) + jnp.log(1.0 + jnp.exp(-jnp.abs(nl)))
    log_a = -LRU_C * gate_a * softplus
    a_ref[...] = jnp.exp(log_a)
    b_ref[...] = jnp.sqrt(1.0 - jnp.exp(2.0 * log_a)) * gate_x * xc

    def step(t, hprev):
        hnew = a_ref[pl.ds(t, 1), :] * hprev + b_ref[pl.ds(t, 1), :]
        hs_ref[pl.ds(t, 1), :] = hnew
        return hnew

    hc_ref[...] = lax.fori_loop(0, tt, step, hc_ref[...], unroll=8)
    o_ref[:, 0:w] = (hs_ref[...] * jax.nn.gelu(u_ref[:, w:2 * w], approximate=True)).astype(o_ref.dtype)
    xbuf_ref[0:HALO, :] = xbuf_ref[tt:tt + HALO, :]

    cbuf_ref[HALO:HALO + tt, :] = u_ref[:, 2 * w:3 * w] * _sigmoid(u_ref[:, 3 * w:4 * w])
    span = shift_ref.shape[1]
    for p in range(1, SUBLANES):
        shift_ref[p - 1] = cbuf_ref[p:p + span, :]
    y = jnp.zeros((tt, w), F32)
    for j in range(CONV_KERNEL):
        off = HALO - (CONV_KERNEL - 1) + j
        p, base = off % SUBLANES, off - off % SUBLANES
        win = cbuf_ref[base:base + tt, :] if p == 0 else shift_ref[p - 1, base:base + tt, :]
        y = y + ccw_ref[j:j + 1, :] * win
    y = _layer_norm(y + ccb_ref[...], cg_ref[...], cb_ref[...])
    o_ref[:, w:2 * w] = (y * _sigmoid(y)).astype(o_ref.dtype)
    cbuf_ref[0:HALO, :] = cbuf_ref[tt:tt + HALO, :]


def _block_diag(wb):
    nb, bd, _ = wb.shape
    eye = jnp.eye(nb, dtype=jnp.bool_)
    return jnp.where(eye[:, None, :, None], wb[:, :, None, :], 0.0).reshape(nb * bd, nb * bd)


def _seqmix(rest, lcw, lcb, wa, ba, wx, bx, lam, ccw, ccb, cg, cb, tt):
    s, n = rest.shape
    w = n // 4
    row = lambda a: a.reshape(1, w).astype(F32)
    full = lambda r, c: pl.BlockSpec((r, c), lambda i: (0, 0))
    kern = functools.partial(_seqmix_kernel, tt=tt, w=w)
    return pl.pallas_call(
        kern,
        grid=(s // tt,),
        in_specs=[pl.BlockSpec((tt, n), lambda i: (i, 0)),
                  full(LRU_CONV, w), full(1, w), full(w, w), full(1, w), full(w, w), full(1, w),
                  full(1, w), full(CONV_KERNEL, w), full(1, w), full(1, w), full(1, w)],
        out_specs=pl.BlockSpec((tt, 2 * w), lambda i: (i, 0)),
        out_shape=jax.ShapeDtypeStruct((s, 2 * w), BF16),
        scratch_shapes=[pltpu.VMEM((tt + HALO, w), F32), pltpu.VMEM((tt + HALO, w), F32),
                        pltpu.VMEM((SUBLANES - 1, tt + HALO - SUBLANES, w), F32),
                        pltpu.VMEM((tt, w), F32), pltpu.VMEM((tt, w), F32), pltpu.VMEM((tt, w), F32),
                        pltpu.VMEM((1, w), F32)],
        compiler_params=_params(),
        name="seqmix",
    )(rest, lcw, row(lcb), _block_diag(wa).astype(BF16), row(ba), _block_diag(wx).astype(BF16),
      row(bx), row(lam), ccw, row(ccb), row(cg), row(cb))


def _outproj_kernel(att_ref, rc_ref, h_ref, wo_ref, g_ref, b_ref, rwh_ref, rwl_ref, rb_ref,
                    h1_ref, h1c_ref, idx_ref, gate_ref, rank_ref, cnt_ref, tri_ref, carry_ref,
                    *, tm, n_att):
    i = pl.program_id(0)
    ne = N_EXPERTS

    @pl.when(i == 0)
    def _init():
        r = lax.broadcasted_iota(jnp.int32, (tm, tm), 0)
        c = lax.broadcasted_iota(jnp.int32, (tm, tm), 1)
        tri_ref[...] = jnp.where(c < r, 1.0, 0.0).astype(BF16)
        carry_ref[...] = jnp.zeros(carry_ref.shape, F32)

    mix = (jnp.dot(att_ref[...], wo_ref[0:n_att, :], preferred_element_type=F32)
           + jnp.dot(rc_ref[...], wo_ref[n_att:, :], preferred_element_type=F32))
    h1 = _layer_norm(RESID_ALPHA * h_ref[...] + mix, g_ref[...], b_ref[...])
    h1_ref[...] = h1
    _store_rows(h1c_ref, h1)

    h1_hi = h1.astype(BF16)
    h1_lo = (h1 - h1_hi.astype(F32)).astype(BF16)
    logits = (jnp.dot(h1_hi, rwh_ref[...], preferred_element_type=F32)
              + jnp.dot(h1_hi, rwl_ref[...], preferred_element_type=F32)
              + jnp.dot(h1_lo, rwh_ref[...], preferred_element_type=F32)) + rb_ref[...]
    lane = lax.broadcasted_iota(jnp.int32, (tm, ne), 1).astype(F32)
    onehot = jnp.zeros((tm, ne), F32)
    vals, sels = [], []
    for _ in range(TOP_K):
        mx = jnp.max(logits, axis=-1, keepdims=True)
        sel = jnp.min(jnp.where(logits == mx, lane, float(ne)), axis=-1, keepdims=True)
        hit = lane == sel
        onehot = onehot + jnp.where(hit, 1.0, 0.0)
        logits = jnp.where(hit, -jnp.inf, logits)
        vals.append(mx)
        sels.append(sel)
    ex = [jnp.exp(v - vals[0]) for v in vals]
    den = ex[0] + ex[1] + ex[2] + ex[3]
    before = jnp.dot(tri_ref[...], onehot.astype(BF16), preferred_element_type=F32) + carry_ref[...]
    for k in range(TOP_K):
        idx_ref[:, k:k + 1] = sels[k].astype(jnp.int32)
        gate_ref[:, k:k + 1] = ex[k] / den
        rank_ref[:, k:k + 1] = jnp.sum(jnp.where(lane == sels[k], before, 0.0), axis=-1,
                                       keepdims=True).astype(jnp.int32)
    carry_ref[...] = carry_ref[...] + jnp.sum(onehot, axis=0, keepdims=True)
    cnt_ref[...] = carry_ref[...].astype(jnp.int32)


def _outproj_router(att, rc, h, wo_bf16, g, b, rw, rb, tm):
    s, d = h.shape
    n_att = att.shape[1]
    ne = N_EXPERTS
    row = lambda a: a.reshape(1, -1).astype(F32)
    full = lambda r, c: pl.BlockSpec((r, c), lambda i: (0, 0))
    tile = lambda c: pl.BlockSpec((tm, c), lambda i: (i, 0))
    kern = functools.partial(_outproj_kernel, tm=tm, n_att=n_att)
    rw_hi = rw.astype(BF16)
    rw_lo = (rw - rw_hi.astype(F32)).astype(BF16)
    return pl.pallas_call(
        kern,
        grid=(s // tm,),
        in_specs=[tile(n_att), tile(rc.shape[1]), tile(d), full(wo_bf16.shape[0], d),
                  full(1, d), full(1, d), full(d, ne), full(d, ne), full(1, ne)],
        out_specs=[tile(d), pl.BlockSpec((tm * (d // LANES), LANES), lambda i: (i, 0)),
                   tile(TOP_K), tile(TOP_K), tile(TOP_K), full(1, ne)],
        out_shape=[jax.ShapeDtypeStruct((s, d), F32),
                   jax.ShapeDtypeStruct((s * (d // LANES), LANES), F32),
                   jax.ShapeDtypeStruct((s, TOP_K), jnp.int32),
                   jax.ShapeDtypeStruct((s, TOP_K), F32),
                   jax.ShapeDtypeStruct((s, TOP_K), jnp.int32),
                   jax.ShapeDtypeStruct((1, ne), jnp.int32)],
        scratch_shapes=[pltpu.VMEM((tm, tm), BF16), pltpu.VMEM((1, ne), F32)],
        compiler_params=_params(),
        name="outproj_router",
    )(att, rc, h, wo_bf16, row(g), row(b), rw_hi, rw_lo, row(rb))


def _row_copy(src_ref, src_row, dst_ref, dst_row, sem, nc):
    src = src_ref.at[pl.ds(pl.multiple_of(src_row * nc, nc), nc)]
    dst = dst_ref.at[pl.ds(pl.multiple_of(dst_row * nc, nc), nc)]
    return pltpu.make_async_copy(src, dst, sem)


def _dest_kernel(idx_ref, rank_ref, pstart_ref, dest_ref):
    tm, ne = idx_ref.shape[0], pstart_ref.shape[1]
    lane = lax.broadcasted_iota(jnp.int32, (tm, ne), 1)
    pstart = pstart_ref[...].astype(F32)
    for k in range(TOP_K):
        hit = lane == idx_ref[:, k:k + 1]
        start = jnp.sum(jnp.where(hit, pstart, 0.0), axis=-1, keepdims=True)
        dest_ref[:, k:k + 1] = start.astype(jnp.int32) + rank_ref[:, k:k + 1]


def _dest_rows(idx, rank, pstart, tm):
    s = idx.shape[0]
    ne = pstart.shape[0]
    tile = pl.BlockSpec((tm, TOP_K), lambda i: (i, 0))
    return pl.pallas_call(
        _dest_kernel,
        grid=(s // tm,),
        in_specs=[tile, tile, pl.BlockSpec((1, ne), lambda i: (0, 0))],
        out_specs=tile,
        out_shape=jax.ShapeDtypeStruct((s, TOP_K), jnp.int32),
        compiler_params=_params(),
        name="moe_dest_rows",
    )(idx, rank, pstart.reshape(1, ne))


def _dispatch_kernel(pstart_ref, pend_ref, dest_ref, h_ref, xd_ref, zero_ref, sem, zsem,
                     *, tt, tm, nc, n_blocks):
    i = pl.program_id(0)

    def zero_copy(e):
        start = pl.multiple_of((pend_ref[e] - tm) * nc, tm * nc)
        return pltpu.make_async_copy(zero_ref, xd_ref.at[pl.ds(start, tm * nc)], zsem)

    def tail_copy(b):
        start = pl.multiple_of(b * (tm * nc), tm * nc)
        return pltpu.make_async_copy(zero_ref, xd_ref.at[pl.ds(start, tm * nc)], zsem)

    @pl.when(i == 0)
    def _clear_padding():
        zero_ref[...] = jnp.zeros(zero_ref.shape, zero_ref.dtype)
        n_used = pend_ref[N_EXPERTS - 1] // tm

        def start_tail(b, carry):
            tail_copy(b).start()
            return carry

        def wait_tail(b, carry):
            tail_copy(b).wait()
            return carry

        lax.fori_loop(n_used, n_blocks, start_tail, 0)
        lax.fori_loop(n_used, n_blocks, wait_tail, 0)
        for e in range(N_EXPERTS):
            @pl.when(pend_ref[e] > pstart_ref[e])
            def _():
                zero_copy(e).start()
        for e in range(N_EXPERTS):
            @pl.when(pend_ref[e] > pstart_ref[e])
            def _():
                zero_copy(e).wait()

    def issue(t, carry):
        for k in range(TOP_K):
            _row_copy(h_ref, t, xd_ref, dest_ref[t * TOP_K + k], sem, nc).start(priority=k % 2)
        return carry

    lax.fori_loop(0, tt, issue, 0)

    for k in range(TOP_K):
        pltpu.make_async_copy(h_ref, xd_ref.at[pl.ds(0, tt * nc)], sem).wait()


def _dispatch(h1c, dest_flat, pstart, pend, rows, s, tt, tm):
    nc = h1c.shape[0] // s
    smem = lambda n: pl.BlockSpec((n,), lambda i, *_: (i,), memory_space=pltpu.SMEM)
    kern = functools.partial(_dispatch_kernel, tt=tt, tm=tm, nc=nc, n_blocks=rows // tm)
    return pl.pallas_call(
        kern,
        grid_spec=pltpu.PrefetchScalarGridSpec(
            num_scalar_prefetch=2,
            grid=(s // tt,),
            in_specs=[smem(tt * TOP_K),
                      pl.BlockSpec((tt * nc, LANES), lambda i, *_: (i, 0))],
            out_specs=pl.BlockSpec(memory_space=pl.ANY),
            scratch_shapes=[pltpu.VMEM((tm * nc, LANES), F32), pltpu.SemaphoreType.DMA(()),
                            pltpu.SemaphoreType.DMA(())]),
        out_shape=jax.ShapeDtypeStruct((rows * nc, LANES), F32),
        compiler_params=_params(),
        name="moe_dispatch",
    )(pstart, pend, dest_flat, h1c)


def _expert_kernel(bexp_ref, nused_ref, x_ref, w1_ref, b1_ref, w2_ref, b2_ref, y_ref,
                   w1b_ref, w2b_ref, *, f, chunk, tm):
    b = pl.program_id(0)
    last = nused_ref[0] - 1
    e = bexp_ref[jnp.minimum(b, last)]
    e_prev = bexp_ref[jnp.maximum(jnp.minimum(b, last) - 1, 0)]

    @pl.when((b == 0) | (e != e_prev))
    def _cast_weights():
        def cast(c, carry):
            r = pl.multiple_of(c * chunk, chunk)
            w1b_ref[pl.ds(r, chunk), :] = w1_ref[pl.ds(r, chunk), :].astype(BF16)
            w2b_ref[pl.ds(r, chunk), :] = w2_ref[pl.ds(r, chunk), :].astype(BF16)
            return carry
        lax.fori_loop(0, w1_ref.shape[0] // chunk, cast, 0)

    @pl.when(b > last)
    def _unused_block():
        y_ref[...] = jnp.zeros(y_ref.shape, y_ref.dtype)

    @pl.when(b <= last)
    def _mlp():
        x = _load_rows(x_ref, tm, w1_ref.shape[0]).astype(BF16)
        hdn = jnp.dot(x, w1b_ref[...], preferred_element_type=F32) + b1_ref[...]
        glu = jnp.minimum(hdn[:, 0:f], SWIGLU_LIMIT)
        lin = jnp.clip(hdn[:, f:2 * f], -SWIGLU_LIMIT, SWIGLU_LIMIT)
        act = glu * _sigmoid(SWIGLU_ALPHA * glu) * (lin + 1.0)
        _store_rows(y_ref, jnp.dot(act.astype(BF16), w2b_ref[...], preferred_element_type=F32)
                    + b2_ref[...])


def _experts(x_disp, bexp, nused, w1, b1, w2, b2, layer, tm):
    d = w1.shape[2]
    f = w2.shape[2]
    nc = d // LANES
    rows = x_disp.shape[0] // nc
    assert f == d, "the weight-cast loop assumes d_expert == d_model"

    def blk(b, bexp_ref, nused_ref):
        return jnp.minimum(b, nused_ref[0] - 1)

    def wmap(b, bexp_ref, nused_ref):
        return (layer, bexp_ref[blk(b, bexp_ref, nused_ref)], 0, 0)

    def bmap(b, bexp_ref, nused_ref):
        return (layer, bexp_ref[blk(b, bexp_ref, nused_ref)], 0, 0)

    kern = functools.partial(_expert_kernel, f=f, chunk=128, tm=tm)
    return pl.pallas_call(
        kern,
        grid_spec=pltpu.PrefetchScalarGridSpec(
            num_scalar_prefetch=2,
            grid=(rows // tm,),
            in_specs=[pl.BlockSpec((tm * nc, LANES), lambda b, be, nu: (b, 0)),
                      pl.BlockSpec((None, None, d, 2 * f), wmap),
                      pl.BlockSpec((None, None, 1, 2 * f), bmap),
                      pl.BlockSpec((None, None, f, d), wmap),
                      pl.BlockSpec((None, None, 1, d), bmap)],
            out_specs=pl.BlockSpec((tm * nc, LANES), lambda b, be, nu: (b, 0)),
            scratch_shapes=[pltpu.VMEM((d, 2 * f), BF16), pltpu.VMEM((f, d), BF16)]),
        out_shape=jax.ShapeDtypeStruct((rows * nc, LANES), F32),
        compiler_params=_params(),
        name="moe_experts",
    )(bexp, nused, x_disp, w1, b1.reshape(b1.shape[0], b1.shape[1], 1, -1), w2,
      b2.reshape(b2.shape[0], b2.shape[1], 1, -1))


def _combine_kernel(dest_ref, h_ref, gate_ref, g_ref, b_ref, yd_ref, o_ref,
                    ybuf_ref, sems, *, tt, nc):
    th = tt // COMBINE_PARTS
    d = h_ref.shape[1]

    def issue(part):
        def one(t, carry):
            for k in range(TOP_K):
                _row_copy(yd_ref, dest_ref[t * TOP_K + k], ybuf_ref.at[k], t, sems.at[part],
                          nc).start(priority=k % 2)
            return carry
        lax.fori_loop(part * th, (part + 1) * th, one, 0)

    def finish(part):
        for k in range(TOP_K):
            pltpu.make_async_copy(yd_ref.at[pl.ds(0, th * nc)],
                                  ybuf_ref.at[k, pl.ds(part * th * nc, th * nc)],
                                  sems.at[part]).wait()
        rows = slice(part * th, (part + 1) * th)
        ff = gate_ref[rows, 0:1] * _load_rows(ybuf_ref, th, d, (0,), part * th)
        for k in range(1, TOP_K):
            ff = ff + gate_ref[rows, k:k + 1] * _load_rows(ybuf_ref, th, d, (k,), part * th)
        o_ref[rows, :] = _layer_norm(RESID_ALPHA * h_ref[rows, :] + ff, g_ref[...], b_ref[...])

    issue(0)
    for part in range(COMBINE_PARTS):
        if part + 1 < COMBINE_PARTS:
            issue(part + 1)
        finish(part)


def _combine(y_disp, h1, gates, dest_flat, g, b, tt):
    s, d = h1.shape
    nc = d // LANES
    row = lambda a: a.reshape(1, -1).astype(F32)
    kern = functools.partial(_combine_kernel, tt=tt, nc=nc)
    return pl.pallas_call(
        kern,
        grid=(s // tt,),
        in_specs=[pl.BlockSpec((tt * TOP_K,), lambda i: (i,), memory_space=pltpu.SMEM),
                  pl.BlockSpec((tt, d), lambda i: (i, 0)),
                  pl.BlockSpec((tt, TOP_K), lambda i: (i, 0)),
                  pl.BlockSpec((1, d), lambda i: (0, 0)),
                  pl.BlockSpec((1, d), lambda i: (0, 0)),
                  pl.BlockSpec(memory_space=pl.ANY)],
        out_specs=pl.BlockSpec((tt, d), lambda i: (i, 0)),
        scratch_shapes=[pltpu.VMEM((TOP_K, tt * nc, LANES), F32),
                        pltpu.SemaphoreType.DMA((COMBINE_PARTS,))],
        out_shape=jax.ShapeDtypeStruct((s, d), F32),
        compiler_params=_params(),
        name="moe_combine",
    )(dest_flat, h1, gates, row(g), row(b), y_disp)


def _tiles(s):
    t = lambda want: math.gcd(s, want)
    return dict(attn=t(512), seq=t(512), route=t(512), expert=t(512), combine=t(512))


def kernel(x, ln_in_g, ln_in_b, w_in, lam_q1, lam_k1, lam_q2, lam_k2, subln_g, lru_conv_w, lru_conv_b, lru_wa, lru_ba, lru_wx, lru_bx, lru_lambda, cf_conv_w, cf_conv_b, cf_ln_g, cf_ln_b, w_out, ln1_g, ln1_b, router_w, router_b, moe_w1, moe_b1, moe_w2, moe_b2, ln2_g, ln2_b):
    bsz, s, d = x.shape
    assert bsz == 1
    tl = _tiles(s)
    tm = tl["expert"]
    n_blocks = (s * TOP_K) // tm + N_EXPERTS
    rows = n_blocks * tm

    h = x.reshape(s, d)
    for l in range(DEPTH):
        lam_init = 0.8 - 0.6 * math.exp(-0.3 * l)
        proj = _inproj(h, ln_in_g, ln_in_b, w_in[l].astype(BF16), tl["attn"], input_ln=(l == 0))
        qt, k, vt, rest = proj[:4]
        if l == 0:
            h = proj[4]
        att = _attention(qt, k, vt, lam_q1[l], lam_k1[l], lam_q2[l], lam_k2[l], subln_g[l],
                         lam_init)
        rc = _seqmix(rest, lru_conv_w[l], lru_conv_b[l], lru_wa[l], lru_ba[l], lru_wx[l],
                     lru_bx[l], lru_lambda[l], cf_conv_w[l], cf_conv_b[l], cf_ln_g[l], cf_ln_b[l],
                     tl["seq"])
        h1, h1c, idx, gates, rank, counts = _outproj_router(
            att, rc, h, w_out[l].astype(BF16), ln1_g[l], ln1_b[l], router_w[l], router_b[l],
            tl["route"])
        counts = counts.reshape(N_EXPERTS)
        padded = (counts + tm - 1) // tm * tm
        e_ids = jnp.arange(N_EXPERTS, dtype=jnp.int32)
        pend = jnp.sum(jnp.where(e_ids[None, :] <= e_ids[:, None], padded[None, :], 0),
                       axis=1).astype(jnp.int32)
        pstart = pend - padded
        block_row = jnp.arange(n_blocks, dtype=jnp.int32) * tm
        bexp = jnp.minimum(jnp.sum((pend[None, :] <= block_row[:, None]).astype(jnp.int32), axis=1),
                           N_EXPERTS - 1).astype(jnp.int32)
        nused = (pend[-1:] // tm).astype(jnp.int32)
        dest_flat = _dest_rows(idx, rank, pstart, tl["route"]).reshape(s * TOP_K)
        x_disp = _dispatch(h1c, dest_flat, pstart, pend, rows, s, tl["route"], tm)
        y_disp = _experts(x_disp, bexp, nused, moe_w1, moe_b1, moe_w2, moe_b2, l, tm)
        h = _combine(y_disp, h1, gates, dest_flat, ln2_g[l], ln2_b[l], tl["combine"])
    return h.reshape(bsz, s, d)
```

```python
import functools
import math

import jax
import jax.numpy as jnp
from jax import lax
from jax.experimental import pallas as pl
from jax.experimental.pallas import tpu as pltpu

F32 = jnp.float32
BF16 = jnp.bfloat16

DEPTH = 2
ATT_HEADS = 4
ATT_QK_DIM = 64
ATT_V_DIM = 128
LRU_BLOCKS = 4
LRU_CONV = 4
LRU_C = 8.0
CONV_KERNEL = 31
N_EXPERTS = 32
TOP_K = 4
SWIGLU_LIMIT = 7.0
SWIGLU_ALPHA = 1.702
LN_EPS = 1e-5
RESID_ALPHA = (2.0 * DEPTH) ** 0.25

VMEM_LIMIT_BYTES = 56 * 1024 * 1024
HALO = 32
LANES = 128
SUBLANES = 8
MXU_DIM = 256
LOG2E = math.log2(math.e)
V_PAD_ROWS = 16
Q_GROUP = 256
COMBINE_PARTS = 4

def _load_rows(ref, n_rows, d, lead=(), first_row=0):
    nc = d // LANES
    return jnp.concatenate([ref[lead + (pl.ds(first_row * nc + c, n_rows, stride=nc), slice(None))]
                            for c in range(nc)], axis=-1)


def _store_rows(ref, val):
    n_rows, d = val.shape
    nc = d // LANES
    for c in range(nc):
        ref[pl.ds(c, n_rows, stride=nc), :] = val[:, c * LANES:(c + 1) * LANES]


def _params(n_axes=1):
    return pltpu.CompilerParams(dimension_semantics=("arbitrary",) * n_axes,
                                vmem_limit_bytes=VMEM_LIMIT_BYTES)


def _layer_norm(x, g, b):
    mu = jnp.mean(x, axis=-1, keepdims=True)
    xc = x - mu
    var = jnp.mean(xc * xc, axis=-1, keepdims=True)
    return xc * lax.rsqrt(var + LN_EPS) * g + b


def _sigmoid(x):
    return 1.0 / (1.0 + jnp.exp(-x))


def _inproj_kernel(h_ref, g_ref, b_ref, w_ref, qt_ref, k_ref, vt_ref, rest_ref, *maybe_h_out,
                   scale):
    nh, hd = ATT_HEADS, ATT_V_DIM
    w_att = nh * hd
    tm = h_ref.shape[0]
    h = h_ref[...]
    if maybe_h_out:
        h = _layer_norm(h, g_ref[...], b_ref[...])
        maybe_h_out[0][...] = h
    hb = h.astype(BF16)
    q = jnp.dot(hb, w_ref[:, 0:w_att], preferred_element_type=F32) * scale
    k_ref[...] = jnp.dot(hb, w_ref[:, w_att:2 * w_att], preferred_element_type=F32).astype(BF16)
    v = jnp.dot(hb, w_ref[:, 2 * w_att:3 * w_att], preferred_element_type=F32)
    ones = jnp.ones((V_PAD_ROWS, tm), BF16)
    for h in range(nh):
        qt_ref[h] = q[:, h * hd:(h + 1) * hd].T.astype(BF16)
        vt_ref[h, 0:hd, :] = v[:, h * hd:(h + 1) * hd].T.astype(BF16)
        vt_ref[h, hd:hd + V_PAD_ROWS, :] = ones
    rest_ref[...] = jnp.dot(hb, w_ref[:, 3 * w_att:], preferred_element_type=F32)


def _inproj(h, ln_g, ln_b, w_bf16, tm, input_ln):
    s, d = h.shape
    n = w_bf16.shape[1]
    nh, hd = ATT_HEADS, ATT_V_DIM
    n_att = 3 * nh * hd
    kern = functools.partial(_inproj_kernel, scale=ATT_QK_DIM ** -0.5 * LOG2E)
    tposed = lambda r: pl.BlockSpec((nh, None, r, tm), lambda i: (0, i, 0, 0))
    rows = pl.BlockSpec((tm, d), lambda i: (i, 0))
    vec = pl.BlockSpec((1, d), lambda i: (0, 0))
    out_specs = [tposed(hd),
                 pl.BlockSpec((tm, nh * hd), lambda i: (i, 0)),
                 tposed(hd + V_PAD_ROWS),
                 pl.BlockSpec((tm, n - n_att), lambda i: (i, 0))]
    out_shape = [jax.ShapeDtypeStruct((nh, s // tm, hd, tm), BF16),
                 jax.ShapeDtypeStruct((s, nh * hd), BF16),
                 jax.ShapeDtypeStruct((nh, s // tm, hd + V_PAD_ROWS, tm), BF16),
                 jax.ShapeDtypeStruct((s, n - n_att), F32)]
    if input_ln:
        out_specs.append(rows)
        out_shape.append(jax.ShapeDtypeStruct((s, d), F32))
    return pl.pallas_call(
        kern,
        grid=(s // tm,),
        in_specs=[rows, vec, vec, pl.BlockSpec((d, n), lambda i: (0, 0))],
        out_specs=out_specs,
        out_shape=out_shape,
        compiler_params=_params(),
        name="inproj",
    )(h, ln_g.reshape(1, d).astype(F32), ln_b.reshape(1, d).astype(F32), w_bf16)


def _split3(x):
    hi = x.astype(BF16)
    rem = x - hi.astype(F32)
    mid = rem.astype(BF16)
    lo = (rem - mid.astype(F32)).astype(BF16)
    return hi, mid, lo


def _attn_kernel(slopes_ref, qt_ref, k_ref, vt_ref, lq1_ref, lk1_ref, lq2_ref, lk2_ref, g_ref,
                 o_ref, qs_ref, kc_ref, mask_ref, *stat_refs, tq, lam_init):
    h = pl.program_id(0)
    qi = pl.program_id(1)
    slope2 = slopes_ref[h]
    dk, hd = ATT_QK_DIM, ATT_V_DIM
    n_groups = (2 * tq) // Q_GROUP
    m_refs, acc_refs = stat_refs[:n_groups], stat_refs[n_groups:2 * n_groups]
    st_refs = stat_refs[2 * n_groups:3 * n_groups]
    mx_refs = stat_refs[3 * n_groups:]

    @pl.when(qi == 0)
    def _build_constants():
        c = lax.broadcasted_iota(jnp.int32, (tq, hd), 0)
        col = lax.broadcasted_iota(jnp.int32, (tq, hd), 1)
        c_lo = jnp.bitwise_and(c, MXU_DIM - 1)
        c_hi = c - c_lo
        kc = jnp.where(col < 3, c_hi, jnp.where(col < 6, c_lo, jnp.where(col < 9, 1, 0)))
        kc_ref[...] = kc.astype(F32).astype(BF16)
        r = lax.broadcasted_iota(jnp.int32, (hd, 2 * tq), 1)
        r = jnp.where(r >= tq, r - tq, r)
        row = lax.broadcasted_iota(jnp.int32, (hd, 2 * tq), 0)
        sl = jnp.full((hd, 2 * tq), slope2, F32)
        s_hi, s_mid, s_lo = _split3(sl)
        t_hi, t_mid, t_lo = _split3(-(sl * r.astype(F32)))
        aug = jnp.zeros((hd, 2 * tq), F32)
        for i, piece in enumerate((s_hi, s_mid, s_lo, s_hi, s_mid, s_lo, t_hi, t_mid, t_lo)):
            aug = jnp.where(row == i, piece.astype(F32), aug)
        qs_ref[hd:2 * hd, :] = aug.astype(BF16)
        ck = lax.broadcasted_iota(jnp.int32, (tq, 2 * tq), 0)
        rq = lax.broadcasted_iota(jnp.int32, (tq, 2 * tq), 1)
        rq = jnp.where(rq >= tq, rq - tq, rq)
        mask_ref[...] = jnp.where(ck <= rq, 0.0, -jnp.inf)

    qt = qt_ref[...]
    dim = lax.broadcasted_iota(jnp.int32, qt.shape, 0)
    zero_q = jnp.zeros_like(qt)
    qs_ref[0:hd, 0:tq] = jnp.where(dim < dk, qt, zero_q)
    qs_ref[0:hd, tq:2 * tq] = jnp.where(dim >= dk, qt, zero_q)
    for m_ref, acc_ref in zip(m_refs, acc_refs):
        m_ref[...] = jnp.full(m_ref.shape, -jnp.inf, F32)
        acc_ref[...] = jnp.zeros(acc_ref.shape, F32)

    cols = [slice(g * Q_GROUP, (g + 1) * Q_GROUP) for g in range(n_groups)]

    def keys_of(j):
        start = pl.multiple_of(j * tq, tq)
        return jnp.concatenate([k_ref[pl.ds(start, tq), :], kc_ref[...]], axis=1)

    def qk(ka, g):
        st = jnp.dot(ka, qs_ref[:, cols[g]], preferred_element_type=F32)
        return st, jnp.max(st, axis=0, keepdims=True)

    def prefetch(ka, g):
        st_refs[g][...], mx_refs[g][...] = qk(ka, g)

    def softmax_pv(st, mx, j, vt, g, diagonal):
        if diagonal:
            st = st + mask_ref[:, cols[g]]
            mx = jnp.max(st, axis=0, keepdims=True)
        off = slope2 * ((j - qi) * tq).astype(F32)
        m_old = m_refs[g][...]
        m_new = jnp.maximum(m_old, mx + off)
        p = jnp.exp2((st - (m_new - off)).astype(BF16))
        alpha = jnp.exp2(m_old - m_new)
        acc_refs[g][...] = alpha * acc_refs[g][...] + jnp.dot(vt, p, preferred_element_type=F32)
        m_refs[g][...] = m_new

    def one_block(j_a, j_next, diagonal):
        vt_a = vt_ref[j_a]
        ka_n = keys_of(j_next) if j_next is not None else None
        for g in range(n_groups):
            softmax_pv(st_refs[g][...], mx_refs[g][...], j_a, vt_a, g, diagonal)
            if ka_n is not None:
                prefetch(ka_n, g)

    def two_blocks(j_a, j_b, j_next):
        vt_a, vt_b = vt_ref[j_a], vt_ref[j_b]
        ka_b, ka_n = keys_of(j_b), keys_of(j_next)
        s_b = {}
        for g in range(n_groups):
            softmax_pv(st_refs[g][...], mx_refs[g][...], j_a, vt_a, g, False)
            s_b[g] = qk(ka_b, g)
        for g in range(n_groups):
            softmax_pv(*s_b.pop(g), j_b, vt_b, g, False)
            prefetch(ka_n, g)

    ka_0 = keys_of(0)
    for g in range(n_groups):
        prefetch(ka_0, g)

    def body(pair, carry):
        two_blocks(2 * pair, 2 * pair + 1, 2 * pair + 2)
        return carry

    lax.fori_loop(0, lax.shift_right_logical(qi, 1), body, 0)

    @pl.when(jnp.bitwise_and(qi, 1) == 1)
    def _odd_block():
        one_block(qi - 1, qi, False)

    one_block(qi, None, True)

    lam = (jnp.exp(jnp.sum(lq1_ref[...] * lk1_ref[...], axis=-1, keepdims=True))
           - jnp.exp(jnp.sum(lq2_ref[...] * lk2_ref[...], axis=-1, keepdims=True)) + lam_init)
    ot = jnp.concatenate([a[0:hd, :] / a[hd:hd + 1, :] for a in acc_refs], axis=1)
    o = ot[:, 0:tq] - lam * ot[:, tq:2 * tq]
    o = o * lax.rsqrt(jnp.mean(o * o, axis=0, keepdims=True) + LN_EPS)
    o_ref[...] = (o * g_ref[...] * (1.0 - lam_init)).T.astype(o_ref.dtype)


def _attention(qt, k, vt, lq1, lk1, lq2, lk2, subln_g, lam_init):
    nh, nblk, hd, tq = qt.shape
    vrows = vt.shape[2]
    s = k.shape[0]
    assert (2 * tq) % Q_GROUP == 0 and hd + 9 <= MXU_DIM
    n_groups = (2 * tq) // Q_GROUP
    slopes = jnp.exp2(-8.0 * (jnp.arange(nh, dtype=F32) + 1.0) / nh) * LOG2E
    vec = lambda a: a.reshape(1, -1).astype(F32)
    small = lambda n: pl.BlockSpec((1, n), lambda h, i, *_: (0, 0))
    kern = functools.partial(_attn_kernel, tq=tq, lam_init=lam_init)
    return pl.pallas_call(
        kern,
        grid_spec=pltpu.PrefetchScalarGridSpec(
            num_scalar_prefetch=1,
            grid=(nh, nblk),
            in_specs=[pl.BlockSpec((None, None, hd, tq), lambda h, i, *_: (h, i, 0, 0)),
                      pl.BlockSpec((s, hd), lambda h, i, *_: (0, h)),
                      pl.BlockSpec((None, nblk, vrows, tq), lambda h, i, *_: (h, 0, 0, 0)),
                      small(ATT_QK_DIM), small(ATT_QK_DIM), small(ATT_QK_DIM), small(ATT_QK_DIM),
                      pl.BlockSpec((hd, 1), lambda h, i, *_: (0, 0))],
            out_specs=pl.BlockSpec((tq, hd), lambda h, i, *_: (i, h)),
            scratch_shapes=[pltpu.VMEM((2 * hd, 2 * tq), BF16),
                            pltpu.VMEM((tq, hd), BF16),
                            pltpu.VMEM((tq, 2 * tq), F32)]
            + [pltpu.VMEM((1, Q_GROUP), F32)] * n_groups
            + [pltpu.VMEM((vrows, Q_GROUP), F32)] * n_groups
            + [pltpu.VMEM((tq, Q_GROUP), F32)] * n_groups
            + [pltpu.VMEM((1, Q_GROUP), F32)] * n_groups),
        out_shape=jax.ShapeDtypeStruct((s, nh * hd), BF16),
        compiler_params=_params(2),
        name="diff_attention",
    )(slopes, qt, k, vt, vec(lq1), vec(lk1), vec(lq2), vec(lk2),
      subln_g.reshape(hd, 1).astype(F32))


def _seqmix_kernel(u_ref, lcw_ref, lcb_ref, wa_ref, ba_ref, wx_ref, bx_ref, lam_ref,
                   ccw_ref, ccb_ref, cg_ref, cb_ref, o_ref,
                   xbuf_ref, cbuf_ref, shift_ref, a_ref, b_ref, hs_ref, hc_ref, *, tt, w):
    i = pl.program_id(0)

    @pl.when(i == 0)
    def _init():
        xbuf_ref[0:HALO, :] = jnp.zeros((HALO, w), F32)
        cbuf_ref[0:HALO, :] = jnp.zeros((HALO, w), F32)
        hc_ref[...] = jnp.zeros(hc_ref.shape, F32)

    xbuf_ref[HALO:HALO + tt, :] = u_ref[:, 0:w]
    xc = jnp.zeros((tt, w), F32)
    for j in range(LRU_CONV):
        off = HALO - (LRU_CONV - 1) + j
        xc = xc + lcw_ref[j:j + 1, :] * xbuf_ref[off:off + tt, :]
    xc = xc + lcb_ref[...]
    xcb = xc.astype(BF16)
    gate_a = _sigmoid(jnp.dot(xcb, wa_ref[...], preferred_element_type=F32) + ba_ref[...])
    gate_x = _sigmoid(jnp.dot(xcb, wx_ref[...], preferred_element_type=F32) + bx_ref[...])
    nl = -lam_ref[...]
    softplus = jnp.maximum(nl, 0.0) + jnp.log(1.0 + jnp.exp(-jnp.abs(nl)))
    log_a = -LRU_C * gate_a * softplus
    a_ref[...] = jnp.exp(log_a)
    b_ref[...] = jnp.sqrt(1.0 - jnp.exp(2.0 * log_a)) * gate_x * xc

    def step(t, hprev):
        hnew = a_ref[pl.ds(t, 1), :] * hprev + b_ref[pl.ds(t, 1), :]
        hs_ref[pl.ds(t, 1), :] = hnew
        return hnew

    hc_ref[...] = lax.fori_loop(0, tt, step, hc_ref[...], unroll=8)
    o_ref[:, 0:w] = (hs_ref[...] * jax.nn.gelu(u_ref[:, w:2 * w], approximate=True)).astype(o_ref.dtype)
    xbuf_ref[0:HALO, :] = xbuf_ref[tt:tt + HALO, :]

    cbuf_ref[HALO:HALO + tt, :] = u_ref[:, 2 * w:3 * w] * _sigmoid(u_ref[:, 3 * w:4 * w])
    span = shift_ref.shape[1]
    for p in range(1, SUBLANES):
        shift_ref[p - 1] = cbuf_ref[p:p + span, :]
    y = jnp.zeros((tt, w), F32)
    for j in range(CONV_KERNEL):
        off = HALO - (CONV_KERNEL - 1) + j
        p, base = off % SUBLANES, off - off % SUBLANES
        win = cbuf_ref[base:base + tt, :] if p == 0 else shift_ref[p - 1, base:base + tt, :]
        y = y + ccw_ref[j:j + 1, :] * win
    y = _layer_norm(y + ccb_ref[...], cg_ref[...], cb_ref[...])
    o_ref[:, w:2 * w] = (y * _sigmoid(y)).astype(o_ref.dtype)
    cbuf_ref[0:HALO, :] = cbuf_ref[tt:tt + HALO, :]


def _block_diag(wb):
    nb, bd, _ = wb.shape
    eye = jnp.eye(nb, dtype=jnp.bool_)
    return jnp.where(eye[:, None, :, None], wb[:, :, None, :], 0.0).reshape(nb * bd, nb * bd)


def _seqmix(rest, lcw, lcb, wa, ba, wx, bx, lam, ccw, ccb, cg, cb, tt):
    s, n = rest.shape
    w = n // 4
    row = lambda a: a.reshape(1, w).astype(F32)
    full = lambda r, c: pl.BlockSpec((r, c), lambda i: (0, 0))
    kern = functools.partial(_seqmix_kernel, tt=tt, w=w)
    return pl.pallas_call(
        kern,
        grid=(s // tt,),
        in_specs=[pl.BlockSpec((tt, n), lambda i: (i, 0)),
                  full(LRU_CONV, w), full(1, w), full(w, w), full(1, w), full(w, w), full(1, w),
                  full(1, w), full(CONV_KERNEL, w), full(1, w), full(1, w), full(1, w)],
        out_specs=pl.BlockSpec((tt, 2 * w), lambda i: (i, 0)),
        out_shape=jax.ShapeDtypeStruct((s, 2 * w), BF16),
        scratch_shapes=[pltpu.VMEM((tt + HALO, w), F32), pltpu.VMEM((tt + HALO, w), F32),
                        pltpu.VMEM((SUBLANES - 1, tt + HALO - SUBLANES, w), F32),
                        pltpu.VMEM((tt, w), F32), pltpu.VMEM((tt, w), F32), pltpu.VMEM((tt, w), F32),
                        pltpu.VMEM((1, w), F32)],
        compiler_params=_params(),
        name="seqmix",
    )(rest, lcw, row(lcb), _block_diag(wa).astype(BF16), row(ba), _block_diag(wx).astype(BF16),
      row(bx), row(lam), ccw, row(ccb), row(cg), row(cb))


def _outproj_kernel(att_ref, rc_ref, h_ref, wo_ref, g_ref, b_ref, rwh_ref, rwl_ref, rb_ref,
                    h1_ref, h1c_ref, idx_ref, gate_ref, rank_ref, cnt_ref, tri_ref, carry_ref,
                    *, tm, n_att):
    i = pl.program_id(0)
    ne = N_EXPERTS

    @pl.when(i == 0)
    def _init():
        r = lax.broadcasted_iota(jnp.int32, (tm, tm), 0)
        c = lax.broadcasted_iota(jnp.int32, (tm, tm), 1)
        tri_ref[...] = jnp.where(c < r, 1.0, 0.0).astype(BF16)
        carry_ref[...] = jnp.zeros(carry_ref.shape, F32)

    mix = (jnp.dot(att_ref[...], wo_ref[0:n_att, :], preferred_element_type=F32)
           + jnp.dot(rc_ref[...], wo_ref[n_att:, :], preferred_element_type=F32))
    h1 = _layer_norm(RESID_ALPHA * h_ref[...] + mix, g_ref[...], b_ref[...])
    h1_ref[...] = h1
    _store_rows(h1c_ref, h1)

    h1_hi = h1.astype(BF16)
    h1_lo = (h1 - h1_hi.astype(F32)).astype(BF16)
    logits = (jnp.dot(h1_hi, rwh_ref[...], preferred_element_type=F32)
              + jnp.dot(h1_hi, rwl_ref[...], preferred_element_type=F32)
              + jnp.dot(h1_lo, rwh_ref[...], preferred_element_type=F32)) + rb_ref[...]
    lane = lax.broadcasted_iota(jnp.int32, (tm, ne), 1).astype(F32)
    onehot = jnp.zeros((tm, ne), F32)
    vals, sels = [], []
    for _ in range(TOP_K):
        mx = jnp.max(logits, axis=-1, keepdims=True)
        sel = jnp.min(jnp.where(logits == mx, lane, float(ne)), axis=-1, keepdims=True)
        hit = lane == sel
        onehot = onehot + jnp.where(hit, 1.0, 0.0)
        logits = jnp.where(hit, -jnp.inf, logits)
        vals.append(mx)
        sels.append(sel)
    ex = [jnp.exp(v - vals[0]) for v in vals]
    den = ex[0] + ex[1] + ex[2] + ex[3]
    before = jnp.dot(tri_ref[...], onehot.astype(BF16), preferred_element_type=F32) + carry_ref[...]
    for k in range(TOP_K):
        idx_ref[:, k:k + 1] = sels[k].astype(jnp.int32)
        gate_ref[:, k:k + 1] = ex[k] / den
        rank_ref[:, k:k + 1] = jnp.sum(jnp.where(lane == sels[k], before, 0.0), axis=-1,
                                       keepdims=True).astype(jnp.int32)
    carry_ref[...] = carry_ref[...] + jnp.sum(onehot, axis=0, keepdims=True)
    cnt_ref[...] = carry_ref[...].astype(jnp.int32)


def _outproj_router(att, rc, h, wo_bf16, g, b, rw, rb, tm):
    s, d = h.shape
    n_att = att.shape[1]
    ne = N_EXPERTS
    row = lambda a: a.reshape(1, -1).astype(F32)
    full = lambda r, c: pl.BlockSpec((r, c), lambda i: (0, 0))
    tile = lambda c: pl.BlockSpec((tm, c), lambda i: (i, 0))
    kern = functools.partial(_outproj_kernel, tm=tm, n_att=n_att)
    rw_hi = rw.astype(BF16)
    rw_lo = (rw - rw_hi.astype(F32)).astype(BF16)
    return pl.pallas_call(
        kern,
        grid=(s // tm,),
        in_specs=[tile(n_att), tile(rc.shape[1]), tile(d), full(wo_bf16.shape[0], d),
                  full(1, d), full(1, d), full(d, ne), full(d, ne), full(1, ne)],
        out_specs=[tile(d), pl.BlockSpec((tm * (d // LANES), LANES), lambda i: (i, 0)),
                   tile(TOP_K), tile(TOP_K), tile(TOP_K), full(1, ne)],
        out_shape=[jax.ShapeDtypeStruct((s, d), F32),
                   jax.ShapeDtypeStruct((s * (d // LANES), LANES), F32),
                   jax.ShapeDtypeStruct((s, TOP_K), jnp.int32),
                   jax.ShapeDtypeStruct((s, TOP_K), F32),
                   jax.ShapeDtypeStruct((s, TOP_K), jnp.int32),
                   jax.ShapeDtypeStruct((1, ne), jnp.int32)],
        scratch_shapes=[pltpu.VMEM((tm, tm), BF16), pltpu.VMEM((1, ne), F32)],
        compiler_params=_params(),
        name="outproj_router",
    )(att, rc, h, wo_bf16, row(g), row(b), rw_hi, rw_lo, row(rb))


def _row_copy(src_ref, src_row, dst_ref, dst_row, sem, nc):
    src = src_ref.at[pl.ds(pl.multiple_of(src_row * nc, nc), nc)]
    dst = dst_ref.at[pl.ds(pl.multiple_of(dst_row * nc, nc), nc)]
    return pltpu.make_async_copy(src, dst, sem)


def _dest_kernel(idx_ref, rank_ref, pstart_ref, dest_ref):
    tm, ne = idx_ref.shape[0], pstart_ref.shape[1]
    lane = lax.broadcasted_iota(jnp.int32, (tm, ne), 1)
    pstart = pstart_ref[...].astype(F32)
    for k in range(TOP_K):
        hit = lane == idx_ref[:, k:k + 1]
        start = jnp.sum(jnp.where(hit, pstart, 0.0), axis=-1, keepdims=True)
        dest_ref[:, k:k + 1] = start.astype(jnp.int32) + rank_ref[:, k:k + 1]


def _dest_rows(idx, rank, pstart, tm):
    s = idx.shape[0]
    ne = pstart.shape[0]
    tile = pl.BlockSpec((tm, TOP_K), lambda i: (i, 0))
    return pl.pallas_call(
        _dest_kernel,
        grid=(s // tm,),
        in_specs=[tile, tile, pl.BlockSpec((1, ne), lambda i: (0, 0))],
        out_specs=tile,
        out_shape=jax.ShapeDtypeStruct((s, TOP_K), jnp.int32),
        compiler_params=_params(),
        name="moe_dest_rows",
    )(idx, rank, pstart.reshape(1, ne))


def _dispatch_kernel(pstart_ref, pend_ref, dest_ref, h_ref, xd_ref, zero_ref, sem, zsem,
                     *, tt, tm, nc, n_blocks):
    i = pl.program_id(0)

    def zero_copy(e):
        start = pl.multiple_of((pend_ref[e] - tm) * nc, tm * nc)
        return pltpu.make_async_copy(zero_ref, xd_ref.at[pl.ds(start, tm * nc)], zsem)

    def tail_copy(b):
        start = pl.multiple_of(b * (tm * nc), tm * nc)
        return pltpu.make_async_copy(zero_ref, xd_ref.at[pl.ds(start, tm * nc)], zsem)

    @pl.when(i == 0)
    def _clear_padding():
        zero_ref[...] = jnp.zeros(zero_ref.shape, zero_ref.dtype)
        n_used = pend_ref[N_EXPERTS - 1] // tm

        def start_tail(b, carry):
            tail_copy(b).start()
            return carry

        def wait_tail(b, carry):
            tail_copy(b).wait()
            return carry

        lax.fori_loop(n_used, n_blocks, start_tail, 0)
        lax.fori_loop(n_used, n_blocks, wait_tail, 0)
        for e in range(N_EXPERTS):
            @pl.when(pend_ref[e] > pstart_ref[e])
            def _():
                zero_copy(e).start()
        for e in range(N_EXPERTS):
            @pl.when(pend_ref[e] > pstart_ref[e])
            def _():
                zero_copy(e).wait()

    def issue(t, carry):
        for k in range(TOP_K):
            _row_copy(h_ref, t, xd_ref, dest_ref[t * TOP_K + k], sem, nc).start(priority=k % 2)
        return carry

    lax.fori_loop(0, tt, issue, 0)

    for k in range(TOP_K):
        pltpu.make_async_copy(h_ref, xd_ref.at[pl.ds(0, tt * nc)], sem).wait()


def _dispatch(h1c, dest_flat, pstart, pend, rows, s, tt, tm):
    nc = h1c.shape[0] // s
    smem = lambda n: pl.BlockSpec((n,), lambda i, *_: (i,), memory_space=pltpu.SMEM)
    kern = functools.partial(_dispatch_kernel, tt=tt, tm=tm, nc=nc, n_blocks=rows // tm)
    return pl.pallas_call(
        kern,
        grid_spec=pltpu.PrefetchScalarGridSpec(
            num_scalar_prefetch=2,
            grid=(s // tt,),
            in_specs=[smem(tt * TOP_K),
                      pl.BlockSpec((tt * nc, LANES), lambda i, *_: (i, 0))],
            out_specs=pl.BlockSpec(memory_space=pl.ANY),
            scratch_shapes=[pltpu.VMEM((tm * nc, LANES), F32), pltpu.SemaphoreType.DMA(()),
                            pltpu.SemaphoreType.DMA(())]),
        out_shape=jax.ShapeDtypeStruct((rows * nc, LANES), F32),
        compiler_params=_params(),
        name="moe_dispatch",
    )(pstart, pend, dest_flat, h1c)


def _expert_kernel(bexp_ref, nused_ref, x_ref, w1_ref, b1_ref, w2_ref, b2_ref, y_ref,
                   w1b_ref, w2b_ref, *, f, chunk, tm):
    b = pl.program_id(0)
    last = nused_ref[0] - 1
    e = bexp_ref[jnp.minimum(b, last)]
    e_prev = bexp_ref[jnp.maximum(jnp.minimum(b, last) - 1, 0)]

    @pl.when((b == 0) | (e != e_prev))
    def _cast_weights():
        def cast(c, carry):
            r = pl.multiple_of(c * chunk, chunk)
            w1b_ref[pl.ds(r, chunk), :] = w1_ref[pl.ds(r, chunk), :].astype(BF16)
            w2b_ref[pl.ds(r, chunk), :] = w2_ref[pl.ds(r, chunk), :].astype(BF16)
            return carry
        lax.fori_loop(0, w1_ref.shape[0] // chunk, cast, 0)

    @pl.when(b > last)
    def _unused_block():
        y_ref[...] = jnp.zeros(y_ref.shape, y_ref.dtype)

    @pl.when(b <= last)
    def _mlp():
        x = _load_rows(x_ref, tm, w1_ref.shape[0]).astype(BF16)
        hdn = jnp.dot(x, w1b_ref[...], preferred_element_type=F32) + b1_ref[...]
        glu = jnp.minimum(hdn[:, 0:f], SWIGLU_LIMIT)
        lin = jnp.clip(hdn[:, f:2 * f], -SWIGLU_LIMIT, SWIGLU_LIMIT)
        act = glu * _sigmoid(SWIGLU_ALPHA * glu) * (lin + 1.0)
        _store_rows(y_ref, jnp.dot(act.astype(BF16), w2b_ref[...], preferred_element_type=F32)
                    + b2_ref[...])


def _experts(x_disp, bexp, nused, w1, b1, w2, b2, layer, tm):
    d = w1.shape[2]
    f = w2.shape[2]
    nc = d // LANES
    rows = x_disp.shape[0] // nc
    assert f == d, "the weight-cast loop assumes d_expert == d_model"

    def blk(b, bexp_ref, nused_ref):
        return jnp.minimum(b, nused_ref[0] - 1)

    def wmap(b, bexp_ref, nused_ref):
        return (layer, bexp_ref[blk(b, bexp_ref, nused_ref)], 0, 0)

    def bmap(b, bexp_ref, nused_ref):
        return (layer, bexp_ref[blk(b, bexp_ref, nused_ref)], 0, 0)

    kern = functools.partial(_expert_kernel, f=f, chunk=128, tm=tm)
    return pl.pallas_call(
        kern,
        grid_spec=pltpu.PrefetchScalarGridSpec(
            num_scalar_prefetch=2,
            grid=(rows // tm,),
            in_specs=[pl.BlockSpec((tm * nc, LANES), lambda b, be, nu: (b, 0)),
                      pl.BlockSpec((None, None, d, 2 * f), wmap),
                      pl.BlockSpec((None, None, 1, 2 * f), bmap),
                      pl.BlockSpec((None, None, f, d), wmap),
                      pl.BlockSpec((None, None, 1, d), bmap)],
            out_specs=pl.BlockSpec((tm * nc, LANES), lambda b, be, nu: (b, 0)),
            scratch_shapes=[pltpu.VMEM((d, 2 * f), BF16), pltpu.VMEM((f, d), BF16)]),
        out_shape=jax.ShapeDtypeStruct((rows * nc, LANES), F32),
        compiler_params=_params(),
        name="moe_experts",
    )(bexp, nused, x_disp, w1, b1.reshape(b1.shape[0], b1.shape[1], 1, -1), w2,
      b2.reshape(b2.shape[0], b2.shape[1], 1, -1))


def _combine_kernel(dest_ref, h_ref, gate_ref, g_ref, b_ref, yd_ref, o_ref,
                    ybuf_ref, sems, *, tt, nc):
    th = tt // COMBINE_PARTS
    d = h_ref.shape[1]

    def issue(part):
        def one(t, carry):
            for k in range(TOP_K):
                _row_copy(yd_ref, dest_ref[t * TOP_K + k], ybuf_ref.at[k], t, sems.at[part],
                          nc).start(priority=k % 2)
            return carry
        lax.fori_loop(part * th, (part + 1) * th, one, 0)

    def finish(part):
        for k in range(TOP_K):
            pltpu.make_async_copy(yd_ref.at[pl.ds(0, th * nc)],
                                  ybuf_ref.at[k, pl.ds(part * th * nc, th * nc)],
                                  sems.at[part]).wait()
        rows = slice(part * th, (part + 1) * th)
        ff = gate_ref[rows, 0:1] * _load_rows(ybuf_ref, th, d, (0,), part * th)
        for k in range(1, TOP_K):
            ff = ff + gate_ref[rows, k:k + 1] * _load_rows(ybuf_ref, th, d, (k,), part * th)
        o_ref[rows, :] = _layer_norm(RESID_ALPHA * h_ref[rows, :] + ff, g_ref[...], b_ref[...])

    issue(0)
    for part in range(COMBINE_PARTS):
        if part + 1 < COMBINE_PARTS:
            issue(part + 1)
        finish(part)


def _combine(y_disp, h1, gates, dest_flat, g, b, tt):
    s, d = h1.shape
    nc = d // LANES
    row = lambda a: a.reshape(1, -1).astype(F32)
    kern = functools.partial(_combine_kernel, tt=tt, nc=nc)
    return pl.pallas_call(
        kern,
        grid=(s // tt,),
        in_specs=[pl.BlockSpec((tt * TOP_K,), lambda i: (i,), memory_space=pltpu.SMEM),
                  pl.BlockSpec((tt, d), lambda i: (i, 0)),
                  pl.BlockSpec((tt, TOP_K), lambda i: (i, 0)),
                  pl.BlockSpec((1, d), lambda i: (0, 0)),
                  pl.BlockSpec((1, d), lambda i: (0, 0)),
                  pl.BlockSpec(memory_space=pl.ANY)],
        out_specs=pl.BlockSpec((tt, d), lambda i: (i, 0)),
        scratch_shapes=[pltpu.VMEM((TOP_K, tt * nc, LANES), F32),
                        pltpu.SemaphoreType.DMA((COMBINE_PARTS,))],
        out_shape=jax.ShapeDtypeStruct((s, d), F32),
        compiler_params=_params(),
        name="moe_combine",
    )(dest_flat, h1, gates, row(g), row(b), y_disp)


def _tiles(s):
    t = lambda want: math.gcd(s, want)
    return dict(attn=t(512), seq=t(512), route=t(512), expert=t(512), combine=t(512))


def kernel(x, ln_in_g, ln_in_b, w_in, lam_q1, lam_k1, lam_q2, lam_k2, subln_g, lru_conv_w, lru_conv_b, lru_wa, lru_ba, lru_wx, lru_bx, lru_lambda, cf_conv_w, cf_conv_b, cf_ln_g, cf_ln_b, w_out, ln1_g, ln1_b, router_w, router_b, moe_w1, moe_b1, moe_w2, moe_b2, ln2_g, ln2_b):
    bsz, s, d = x.shape
    assert bsz == 1
    tl = _tiles(s)
    tm = tl["expert"]
    n_blocks = (s * TOP_K) // tm + N_EXPERTS
    rows = n_blocks * tm

    h = x.reshape(s, d)
    for l in range(DEPTH):
        lam_init = 0.8 - 0.6 * math.exp(-0.3 * l)
        proj = _inproj(h, ln_in_g, ln_in_b, w_in[l].astype(BF16), tl["attn"], input_ln=(l == 0))
        qt, k, vt, rest = proj[:4]
        if l == 0:
            h = proj[4]
        att = _attention(qt, k, vt, lam_q1[l], lam_k1[l], lam_q2[l], lam_k2[l], subln_g[l],
                         lam_init)
        rc = _seqmix(rest, lru_conv_w[l], lru_conv_b[l], lru_wa[l], lru_ba[l], lru_wx[l],
                     lru_bx[l], lru_lambda[l], cf_conv_w[l], cf_conv_b[l], cf_ln_g[l], cf_ln_b[l],
                     tl["seq"])
        h1, h1c, idx, gates, rank, counts = _outproj_router(
            att, rc, h, w_out[l].astype(BF16), ln1_g[l], ln1_b[l], router_w[l], router_b[l],
            tl["route"])
        counts = counts.reshape(N_EXPERTS)
        padded = (counts + tm - 1) // tm * tm
        e_ids = jnp.arange(N_EXPERTS, dtype=jnp.int32)
        pend = jnp.sum(jnp.where(e_ids[None, :] <= e_ids[:, None], padded[None, :], 0),
                       axis=1).astype(jnp.int32)
        pstart = pend - padded
        block_row = jnp.arange(n_blocks, dtype=jnp.int32) * tm
        bexp = jnp.minimum(jnp.sum((pend[None, :] <= block_row[:, None]).astype(jnp.int32), axis=1),
                           N_EXPERTS - 1).astype(jnp.int32)
        nused = (pend[-1:] // tm).astype(jnp.int32)
        dest_flat = _dest_rows(idx, rank, pstart, tl["route"]).reshape(s * TOP_K)
        x_disp = _dispatch(h1c, dest_flat, pstart, pend, rows, s, tl["route"], tm)
        y_disp = _experts(x_disp, bexp, nused, moe_w1, moe_b1, moe_w2, moe_b2, l, tm)
        h = _combine(y_disp, h1, gates, dest_flat, ln2_g[l], ln2_b[l], tl["combine"])
    return h.reshape(bsz, s, d)
```

```python
import functools
import math

import jax
import jax.numpy as jnp
from jax import lax
from jax.experimental import pallas as pl
from jax.experimental.pallas import tpu as pltpu

F32 = jnp.float32
BF16 = jnp.bfloat16

DEPTH = 2
ATT_HEADS = 4
ATT_QK_DIM = 64
ATT_V_DIM = 128
LRU_BLOCKS = 4
LRU_CONV = 4
LRU_C = 8.0
CONV_KERNEL = 31
N_EXPERTS = 32
TOP_K = 4
SWIGLU_LIMIT = 7.0
SWIGLU_ALPHA = 1.702
LN_EPS = 1e-5
RESID_ALPHA = (2.0 * DEPTH) ** 0.25

VMEM_LIMIT_BYTES = 56 * 1024 * 1024
HALO = 32
LANES = 128
SUBLANES = 8
MXU_DIM = 256
LOG2E = math.log2(math.e)
V_PAD_ROWS = 16
Q_GROUP = 256
COMBINE_PARTS = 4

def _load_rows(ref, n_rows, d, lead=(), first_row=0):
    nc = d // LANES
    return jnp.concatenate([ref[lead + (pl.ds(first_row * nc + c, n_rows, stride=nc), slice(None))]
                            for c in range(nc)], axis=-1)


def _store_rows(ref, val):
    n_rows, d = val.shape
    nc = d // LANES
    for c in range(nc):
        ref[pl.ds(c, n_rows, stride=nc), :] = val[:, c * LANES:(c + 1) * LANES]


def _params(n_axes=1):
    return pltpu.CompilerParams(dimension_semantics=("arbitrary",) * n_axes,
                                vmem_limit_bytes=VMEM_LIMIT_BYTES)


def _layer_norm(x, g, b):
    mu = jnp.mean(x, axis=-1, keepdims=True)
    xc = x - mu
    var = jnp.mean(xc * xc, axis=-1, keepdims=True)
    return xc * lax.rsqrt(var + LN_EPS) * g + b


def _sigmoid(x):
    return 1.0 / (1.0 + jnp.exp(-x))


def _inproj_kernel(h_ref, g_ref, b_ref, w_ref, qt_ref, k_ref, vt_ref, rest_ref, *maybe_h_out,
                   scale):
    nh, hd = ATT_HEADS, ATT_V_DIM
    w_att = nh * hd
    tm = h_ref.shape[0]
    h = h_ref[...]
    if maybe_h_out:
        h = _layer_norm(h, g_ref[...], b_ref[...])
        maybe_h_out[0][...] = h
    hb = h.astype(BF16)
    q = jnp.dot(hb, w_ref[:, 0:w_att], preferred_element_type=F32) * scale
    k_ref[...] = jnp.dot(hb, w_ref[:, w_att:2 * w_att], preferred_element_type=F32).astype(BF16)
    v = jnp.dot(hb, w_ref[:, 2 * w_att:3 * w_att], preferred_element_type=F32)
    ones = jnp.ones((V_PAD_ROWS, tm), BF16)
    for h in range(nh):
        qt_ref[h] = q[:, h * hd:(h + 1) * hd].T.astype(BF16)
        vt_ref[h, 0:hd, :] = v[:, h * hd:(h + 1) * hd].T.astype(BF16)
        vt_ref[h, hd:hd + V_PAD_ROWS, :] = ones
    rest_ref[...] = jnp.dot(hb, w_ref[:, 3 * w_att:], preferred_element_type=F32)


def _inproj(h, ln_g, ln_b, w_bf16, tm, input_ln):
    s, d = h.shape
    n = w_bf16.shape[1]
    nh, hd = ATT_HEADS, ATT_V_DIM
    n_att = 3 * nh * hd
    kern = functools.partial(_inproj_kernel, scale=ATT_QK_DIM ** -0.5 * LOG2E)
    tposed = lambda r: pl.BlockSpec((nh, None, r, tm), lambda i: (0, i, 0, 0))
    rows = pl.BlockSpec((tm, d), lambda i: (i, 0))
    vec = pl.BlockSpec((1, d), lambda i: (0, 0))
    out_specs = [tposed(hd),
                 pl.BlockSpec((tm, nh * hd), lambda i: (i, 0)),
                 tposed(hd + V_PAD_ROWS),
                 pl.BlockSpec((tm, n - n_att), lambda i: (i, 0))]
    out_shape = [jax.ShapeDtypeStruct((nh, s // tm, hd, tm), BF16),
                 jax.ShapeDtypeStruct((s, nh * hd), BF16),
                 jax.ShapeDtypeStruct((nh, s // tm, hd + V_PAD_ROWS, tm), BF16),
                 jax.ShapeDtypeStruct((s, n - n_att), F32)]
    if input_ln:
        out_specs.append(rows)
        out_shape.append(jax.ShapeDtypeStruct((s, d), F32))
    return pl.pallas_call(
        kern,
        grid=(s // tm,),
        in_specs=[rows, vec, vec, pl.BlockSpec((d, n), lambda i: (0, 0))],
        out_specs=out_specs,
        out_shape=out_shape,
        compiler_params=_params(),
        name="inproj",
    )(h, ln_g.reshape(1, d).astype(F32), ln_b.reshape(1, d).astype(F32), w_bf16)


def _split3(x):
    hi = x.astype(BF16)
    rem = x - hi.astype(F32)
    mid = rem.astype(BF16)
    lo = (rem - mid.astype(F32)).astype(BF16)
    return hi, mid, lo


def _attn_kernel(slopes_ref, qt_ref, k_ref, vt_ref, lq1_ref, lk1_ref, lq2_ref, lk2_ref, g_ref,
                 o_ref, qs_ref, kc_ref, mask_ref, *stat_refs, tq, lam_init):
    h = pl.program_id(0)
    qi = pl.program_id(1)
    slope2 = slopes_ref[h]
    dk, hd = ATT_QK_DIM, ATT_V_DIM
    n_groups = (2 * tq) // Q_GROUP
    m_refs, acc_refs = stat_refs[:n_groups], stat_refs[n_groups:2 * n_groups]
    st_refs = stat_refs[2 * n_groups:3 * n_groups]
    mx_refs = stat_refs[3 * n_groups:]

    @pl.when(qi == 0)
    def _build_constants():
        c = lax.broadcasted_iota(jnp.int32, (tq, hd), 0)
        col = lax.broadcasted_iota(jnp.int32, (tq, hd), 1)
        c_lo = jnp.bitwise_and(c, MXU_DIM - 1)
        c_hi = c - c_lo
        kc = jnp.where(col < 3, c_hi, jnp.where(col < 6, c_lo, jnp.where(col < 9, 1, 0)))
        kc_ref[...] = kc.astype(F32).astype(BF16)
        r = lax.broadcasted_iota(jnp.int32, (hd, 2 * tq), 1)
        r = jnp.where(r >= tq, r - tq, r)
        row = lax.broadcasted_iota(jnp.int32, (hd, 2 * tq), 0)
        sl = jnp.full((hd, 2 * tq), slope2, F32)
        s_hi, s_mid, s_lo = _split3(sl)
        t_hi, t_mid, t_lo = _split3(-(sl * r.astype(F32)))
        aug = jnp.zeros((hd, 2 * tq), F32)
        for i, piece in enumerate((s_hi, s_mid, s_lo, s_hi, s_mid, s_lo, t_hi, t_mid, t_lo)):
            aug = jnp.where(row == i, piece.astype(F32), aug)
        for slot in range(2):
            qs_ref[slot, hd:2 * hd, :] = aug.astype(BF16)
        ck = lax.broadcasted_iota(jnp.int32, (tq, 2 * tq), 0)
        rq = lax.broadcasted_iota(jnp.int32, (tq, 2 * tq), 1)
        rq = jnp.where(rq >= tq, rq - tq, rq)
        mask_ref[...] = jnp.where(ck <= rq, 0.0, -jnp.inf)

    cols = [slice(g * Q_GROUP, (g + 1) * Q_GROUP) for g in range(n_groups)]
    cur = jnp.bitwise_and(qi, 1)
    nxt = 1 - cur

    def stack_queries(slot, q_block):
        qt = qt_ref[q_block]
        dim = lax.broadcasted_iota(jnp.int32, qt.shape, 0)
        zero_q = jnp.zeros_like(qt)
        qs_ref[slot, 0:hd, 0:tq] = jnp.where(dim < dk, qt, zero_q)
        qs_ref[slot, 0:hd, tq:2 * tq] = jnp.where(dim >= dk, qt, zero_q)

    def keys_of(j):
        start = pl.multiple_of(j * tq, tq)
        return jnp.concatenate([k_ref[pl.ds(start, tq), :], kc_ref[...]], axis=1)

    def qk(ka, g, slot):
        st = jnp.dot(ka, qs_ref[slot, :, cols[g]], preferred_element_type=F32)
        return st, jnp.max(st, axis=0, keepdims=True)

    def prefetch(ka, g, slot):
        st_refs[g][...], mx_refs[g][...] = qk(ka, g, slot)

    @pl.when(qi == 0)
    def _first_scores_of_head():
        stack_queries(0, 0)
        ka_0 = keys_of(0)
        for g in range(n_groups):
            prefetch(ka_0, g, 0)

    for m_ref, acc_ref in zip(m_refs, acc_refs):
        m_ref[...] = jnp.full(m_ref.shape, -jnp.inf, F32)
        acc_ref[...] = jnp.zeros(acc_ref.shape, F32)

    def softmax_pv(st, mx, j, vt, g, diagonal):
        if diagonal:
            st = st + mask_ref[:, cols[g]]
            mx = jnp.max(st, axis=0, keepdims=True)
        off = slope2 * ((j - qi) * tq).astype(F32)
        m_old = m_refs[g][...]
        m_new = jnp.maximum(m_old, mx + off)
        p = jnp.exp2(st - (m_new - off)).astype(BF16)
        alpha = jnp.exp2(m_old - m_new)
        acc_refs[g][...] = alpha * acc_refs[g][...] + jnp.dot(vt, p, preferred_element_type=F32)
        m_refs[g][...] = m_new

    def one_block(j_a, j_next, diagonal, next_slot=cur):
        vt_a = vt_ref[j_a]
        ka_n = keys_of(j_next) if j_next is not None else None
        for g in range(n_groups):
            softmax_pv(st_refs[g][...], mx_refs[g][...], j_a, vt_a, g, diagonal)
            if ka_n is not None:
                prefetch(ka_n, g, next_slot)

    def two_blocks(j_a, j_b, j_next):
        vt_a, vt_b = vt_ref[j_a], vt_ref[j_b]
        ka_b, ka_n = keys_of(j_b), keys_of(j_next)
        s_b = {}
        for g in range(n_groups):
            softmax_pv(st_refs[g][...], mx_refs[g][...], j_a, vt_a, g, False)
            s_b[g] = qk(ka_b, g, cur)
        for g in range(n_groups):
            softmax_pv(*s_b.pop(g), j_b, vt_b, g, False)
            prefetch(ka_n, g, cur)

    def body(pair, carry):
        two_blocks(2 * pair, 2 * pair + 1, 2 * pair + 2)
        return carry

    lax.fori_loop(0, lax.shift_right_logical(qi, 1), body, 0)

    @pl.when(jnp.bitwise_and(qi, 1) == 1)
    def _odd_block():
        one_block(qi - 1, qi, False)

    n_q_blocks = qt_ref.shape[0]

    @pl.when(qi + 1 < n_q_blocks)
    def _diagonal_then_next_query_block():
        stack_queries(nxt, qi + 1)
        one_block(qi, 0, True, next_slot=nxt)

    @pl.when(qi + 1 == n_q_blocks)
    def _diagonal_last():
        one_block(qi, None, True)

    lam = (jnp.exp(jnp.sum(lq1_ref[...] * lk1_ref[...], axis=-1, keepdims=True))
           - jnp.exp(jnp.sum(lq2_ref[...] * lk2_ref[...], axis=-1, keepdims=True)) + lam_init)
    ot = jnp.concatenate([a[0:hd, :] / a[hd:hd + 1, :] for a in acc_refs], axis=1)
    o = ot[:, 0:tq] - lam * ot[:, tq:2 * tq]
    o = o * lax.rsqrt(jnp.mean(o * o, axis=0, keepdims=True) + LN_EPS)
    o_ref[...] = (o * g_ref[...] * (1.0 - lam_init)).T.astype(o_ref.dtype)


def _attention(qt, k, vt, lq1, lk1, lq2, lk2, subln_g, lam_init):
    nh, nblk, hd, tq = qt.shape
    vrows = vt.shape[2]
    s = k.shape[0]
    assert (2 * tq) % Q_GROUP == 0 and hd + 9 <= MXU_DIM
    n_groups = (2 * tq) // Q_GROUP
    slopes = jnp.exp2(-8.0 * (jnp.arange(nh, dtype=F32) + 1.0) / nh) * LOG2E
    vec = lambda a: a.reshape(1, -1).astype(F32)
    small = lambda n: pl.BlockSpec((1, n), lambda h, i, *_: (0, 0))
    kern = functools.partial(_attn_kernel, tq=tq, lam_init=lam_init)
    return pl.pallas_call(
        kern,
        grid_spec=pltpu.PrefetchScalarGridSpec(
            num_scalar_prefetch=1,
            grid=(nh, nblk),
            in_specs=[pl.BlockSpec((None, nblk, hd, tq), lambda h, i, *_: (h, 0, 0, 0)),
                      pl.BlockSpec((s, hd), lambda h, i, *_: (0, h)),
                      pl.BlockSpec((None, nblk, vrows, tq), lambda h, i, *_: (h, 0, 0, 0)),
                      small(ATT_QK_DIM), small(ATT_QK_DIM), small(ATT_QK_DIM), small(ATT_QK_DIM),
                      pl.BlockSpec((hd, 1), lambda h, i, *_: (0, 0))],
            out_specs=pl.BlockSpec((tq, hd), lambda h, i, *_: (i, h)),
            scratch_shapes=[pltpu.VMEM((2, 2 * hd, 2 * tq), BF16),
                            pltpu.VMEM((tq, hd), BF16),
                            pltpu.VMEM((tq, 2 * tq), F32)]
            + [pltpu.VMEM((1, Q_GROUP), F32)] * n_groups
            + [pltpu.VMEM((vrows, Q_GROUP), F32)] * n_groups
            + [pltpu.VMEM((tq, Q_GROUP), F32)] * n_groups
            + [pltpu.VMEM((1, Q_GROUP), F32)] * n_groups),
        out_shape=jax.ShapeDtypeStruct((s, nh * hd), BF16),
        compiler_params=_params(2),
        name="diff_attention",
    )(slopes, qt, k, vt, vec(lq1), vec(lk1), vec(lq2), vec(lk2),
      subln_g.reshape(hd, 1).astype(F32))


def _seqmix_kernel(u_ref, lcw_ref, lcb_ref, wa_ref, ba_ref, wx_ref, bx_ref, lam_ref,
                   ccw_ref, ccb_ref, cg_ref, cb_ref, o_ref,
                   xbuf_ref, cbuf_ref, shift_ref, a_ref, b_ref, hs_ref, hc_ref, *, tt, w):
    i = pl.program_id(0)

    @pl.when(i == 0)
    def _init():
        xbuf_ref[0:HALO, :] = jnp.zeros((HALO, w), F32)
        cbuf_ref[0:HALO, :] = jnp.zeros((HALO, w), F32)
        hc_ref[...] = jnp.zeros(hc_ref.shape, F32)

    xbuf_ref[HALO:HALO + tt, :] = u_ref[:, 0:w]
    xc = jnp.zeros((tt, w), F32)
    for j in range(LRU_CONV):
        off = HALO - (LRU_CONV - 1) + j
        xc = xc + lcw_ref[j:j + 1, :] * xbuf_ref[off:off + tt, :]
    xc = xc + lcb_ref[...]
    xcb = xc.astype(BF16)
    gate_a = _sigmoid(jnp.dot(xcb, wa_ref[...], preferred_element_type=F32) + ba_ref[...])
    gate_x = _sigmoid(jnp.dot(xcb, wx_ref[...], preferred_element_type=F32) + bx_ref[...])
    nl = -lam_ref[...]
    softplus = jnp.maximum(nl, 0.0) + jnp.log(1.0 + jnp.exp(-jnp.abs(nl)))
    log_a = -LRU_C * gate_a * softplus
    a_ref[...] = jnp.exp(log_a)
    b_ref[...] = jnp.sqrt(1.0 - jnp.exp(2.0 * log_a)) * gate_x * xc

    def step(t, hprev):
        hnew = a_ref[pl.ds(t, 1), :] * hprev + b_ref[pl.ds(t, 1), :]
        hs_ref[pl.ds(t, 1), :] = hnew
        return hnew

    hc_ref[...] = lax.fori_loop(0, tt, step, hc_ref[...], unroll=8)
    o_ref[:, 0:w] = (hs_ref[...] * jax.nn.gelu(u_ref[:, w:2 * w], approximate=True)).astype(o_ref.dtype)
    xbuf_ref[0:HALO, :] = xbuf_ref[tt:tt + HALO, :]

    cbuf_ref[HALO:HALO + tt, :] = u_ref[:, 2 * w:3 * w] * _sigmoid(u_ref[:, 3 * w:4 * w])
    span = shift_ref.shape[1]
    for p in range(1, SUBLANES):
        shift_ref[p - 1] = cbuf_ref[p:p + span, :]
    y = jnp.zeros((tt, w), F32)
    for j in range(CONV_KERNEL):
        off = HALO - (CONV_KERNEL - 1) + j
        p, base = off % SUBLANES, off - off % SUBLANES
        win = cbuf_ref[base:base + tt, :] if p == 0 else shift_ref[p - 1, base:base + tt, :]
        y = y + ccw_ref[j:j + 1, :] * win
    y = _layer_norm(y + ccb_ref[...], cg_ref[...], cb_ref[...])
    o_ref[:, w:2 * w] = (y * _sigmoid(y)).astype(o_ref.dtype)
    cbuf_ref[0:HALO, :] = cbuf_ref[tt:tt + HALO, :]


def _block_diag(wb):
    nb, bd, _ = wb.shape
    eye = jnp.eye(nb, dtype=jnp.bool_)
    return jnp.where(eye[:, None, :, None], wb[:, :, None, :], 0.0).reshape(nb * bd, nb * bd)


def _seqmix(rest, lcw, lcb, wa, ba, wx, bx, lam, ccw, ccb, cg, cb, tt):
    s, n = rest.shape
    w = n // 4
    row = lambda a: a.reshape(1, w).astype(F32)
    full = lambda r, c: pl.BlockSpec((r, c), lambda i: (0, 0))
    kern = functools.partial(_seqmix_kernel, tt=tt, w=w)
    return pl.pallas_call(
        kern,
        grid=(s // tt,),
        in_specs=[pl.BlockSpec((tt, n), lambda i: (i, 0)),
                  full(LRU_CONV, w), full(1, w), full(w, w), full(1, w), full(w, w), full(1, w),
                  full(1, w), full(CONV_KERNEL, w), full(1, w), full(1, w), full(1, w)],
        out_specs=pl.BlockSpec((tt, 2 * w), lambda i: (i, 0)),
        out_shape=jax.ShapeDtypeStruct((s, 2 * w), BF16),
        scratch_shapes=[pltpu.VMEM((tt + HALO, w), F32), pltpu.VMEM((tt + HALO, w), F32),
                        pltpu.VMEM((SUBLANES - 1, tt + HALO - SUBLANES, w), F32),
                        pltpu.VMEM((tt, w), F32), pltpu.VMEM((tt, w), F32), pltpu.VMEM((tt, w), F32),
                        pltpu.VMEM((1, w), F32)],
        compiler_params=_params(),
        name="seqmix",
    )(rest, lcw, row(lcb), _block_diag(wa).astype(BF16), row(ba), _block_diag(wx).astype(BF16),
      row(bx), row(lam), ccw, row(ccb), row(cg), row(cb))


def _outproj_kernel(att_ref, rc_ref, h_ref, wo_ref, g_ref, b_ref, rwh_ref, rwl_ref, rb_ref,
                    h1_ref, h1c_ref, idx_ref, gate_ref, rank_ref, cnt_ref, tri_ref, carry_ref,
                    *, tm, n_att):
    i = pl.program_id(0)
    ne = N_EXPERTS

    @pl.when(i == 0)
    def _init():
        r = lax.broadcasted_iota(jnp.int32, (tm, tm), 0)
        c = lax.broadcasted_iota(jnp.int32, (tm, tm), 1)
        tri_ref[...] = jnp.where(c < r, 1.0, 0.0).astype(BF16)
        carry_ref[...] = jnp.zeros(carry_ref.shape, F32)

    mix = (jnp.dot(att_ref[...], wo_ref[0:n_att, :], preferred_element_type=F32)
           + jnp.dot(rc_ref[...], wo_ref[n_att:, :], preferred_element_type=F32))
    h1 = _layer_norm(RESID_ALPHA * h_ref[...] + mix, g_ref[...], b_ref[...])
    h1_ref[...] = h1
    _store_rows(h1c_ref, h1)

    h1_hi = h1.astype(BF16)
    h1_lo = (h1 - h1_hi.astype(F32)).astype(BF16)
    logits = (jnp.dot(h1_hi, rwh_ref[...], preferred_element_type=F32)
              + jnp.dot(h1_hi, rwl_ref[...], preferred_element_type=F32)
              + jnp.dot(h1_lo, rwh_ref[...], preferred_element_type=F32)) + rb_ref[...]
    lane = lax.broadcasted_iota(jnp.int32, (tm, ne), 1).astype(F32)
    onehot = jnp.zeros((tm, ne), F32)
    vals, sels = [], []
    for _ in range(TOP_K):
        mx = jnp.max(logits, axis=-1, keepdims=True)
        sel = jnp.min(jnp.where(logits == mx, lane, float(ne)), axis=-1, keepdims=True)
        hit = lane == sel
        onehot = onehot + jnp.where(hit, 1.0, 0.0)
        logits = jnp.where(hit, -jnp.inf, logits)
        vals.append(mx)
        sels.append(sel)
    ex = [jnp.exp(v - vals[0]) for v in vals]
    den = ex[0] + ex[1] + ex[2] + ex[3]
    before = jnp.dot(tri_ref[...], onehot.astype(BF16), preferred_element_type=F32) + carry_ref[...]
    for k in range(TOP_K):
        idx_ref[:, k:k + 1] = sels[k].astype(jnp.int32)
        gate_ref[:, k:k + 1] = ex[k] / den
        rank_ref[:, k:k + 1] = jnp.sum(jnp.where(lane == sels[k], before, 0.0), axis=-1,
                                       keepdims=True).astype(jnp.int32)
    carry_ref[...] = carry_ref[...] + jnp.sum(onehot, axis=0, keepdims=True)
    cnt_ref[...] = carry_ref[...].astype(jnp.int32)


def _outproj_router(att, rc, h, wo_bf16, g, b, rw, rb, tm):
    s, d = h.shape
    n_att = att.shape[1]
    ne = N_EXPERTS
    row = lambda a: a.reshape(1, -1).astype(F32)
    full = lambda r, c: pl.BlockSpec((r, c), lambda i: (0, 0))
    tile = lambda c: pl.BlockSpec((tm, c), lambda i: (i, 0))
    kern = functools.partial(_outproj_kernel, tm=tm, n_att=n_att)
    rw_hi = rw.astype(BF16)
    rw_lo = (rw - rw_hi.astype(F32)).astype(BF16)
    return pl.pallas_call(
        kern,
        grid=(s // tm,),
        in_specs=[tile(n_att), tile(rc.shape[1]), tile(d), full(wo_bf16.shape[0], d),
                  full(1, d), full(1, d), full(d, ne), full(d, ne), full(1, ne)],
        out_specs=[tile(d), pl.BlockSpec((tm * (d // LANES), LANES), lambda i: (i, 0)),
                   tile(TOP_K), tile(TOP_K), tile(TOP_K), full(1, ne)],
        out_shape=[jax.ShapeDtypeStruct((s, d), F32),
                   jax.ShapeDtypeStruct((s * (d // LANES), LANES), F32),
                   jax.ShapeDtypeStruct((s, TOP_K), jnp.int32),
                   jax.ShapeDtypeStruct((s, TOP_K), F32),
                   jax.ShapeDtypeStruct((s, TOP_K), jnp.int32),
                   jax.ShapeDtypeStruct((1, ne), jnp.int32)],
        scratch_shapes=[pltpu.VMEM((tm, tm), BF16), pltpu.VMEM((1, ne), F32)],
        compiler_params=_params(),
        name="outproj_router",
    )(att, rc, h, wo_bf16, row(g), row(b), rw_hi, rw_lo, row(rb))


def _row_copy(src_ref, src_row, dst_ref, dst_row, sem, nc):
    src = src_ref.at[pl.ds(pl.multiple_of(src_row * nc, nc), nc)]
    dst = dst_ref.at[pl.ds(pl.multiple_of(dst_row * nc, nc), nc)]
    return pltpu.make_async_copy(src, dst, sem)


def _dest_kernel(idx_ref, rank_ref, pstart_ref, dest_ref):
    tm, ne = idx_ref.shape[0], pstart_ref.shape[1]
    lane = lax.broadcasted_iota(jnp.int32, (tm, ne), 1)
    pstart = pstart_ref[...].astype(F32)
    for k in range(TOP_K):
        hit = lane == idx_ref[:, k:k + 1]
        start = jnp.sum(jnp.where(hit, pstart, 0.0), axis=-1, keepdims=True)
        dest_ref[:, k:k + 1] = start.astype(jnp.int32) + rank_ref[:, k:k + 1]


def _dest_rows(idx, rank, pstart, tm):
    s = idx.shape[0]
    ne = pstart.shape[0]
    tile = pl.BlockSpec((tm, TOP_K), lambda i: (i, 0))
    return pl.pallas_call(
        _dest_kernel,
        grid=(s // tm,),
        in_specs=[tile, tile, pl.BlockSpec((1, ne), lambda i: (0, 0))],
        out_specs=tile,
        out_shape=jax.ShapeDtypeStruct((s, TOP_K), jnp.int32),
        compiler_params=_params(),
        name="moe_dest_rows",
    )(idx, rank, pstart.reshape(1, ne))


def _dispatch_kernel(pstart_ref, pend_ref, dest_ref, h_ref, xd_ref, zero_ref, sem, zsem,
                     *, tt, tm, nc, n_blocks):
    i = pl.program_id(0)

    def zero_copy(e):
        start = pl.multiple_of((pend_ref[e] - tm) * nc, tm * nc)
        return pltpu.make_async_copy(zero_ref, xd_ref.at[pl.ds(start, tm * nc)], zsem)

    def tail_copy(b):
        start = pl.multiple_of(b * (tm * nc), tm * nc)
        return pltpu.make_async_copy(zero_ref, xd_ref.at[pl.ds(start, tm * nc)], zsem)

    @pl.when(i == 0)
    def _clear_padding():
        zero_ref[...] = jnp.zeros(zero_ref.shape, zero_ref.dtype)
        n_used = pend_ref[N_EXPERTS - 1] // tm

        def start_tail(b, carry):
            tail_copy(b).start()
            return carry

        def wait_tail(b, carry):
            tail_copy(b).wait()
            return carry

        lax.fori_loop(n_used, n_blocks, start_tail, 0)
        lax.fori_loop(n_used, n_blocks, wait_tail, 0)
        for e in range(N_EXPERTS):
            @pl.when(pend_ref[e] > pstart_ref[e])
            def _():
                zero_copy(e).start()
        for e in range(N_EXPERTS):
            @pl.when(pend_ref[e] > pstart_ref[e])
            def _():
                zero_copy(e).wait()

    def issue(t, carry):
        for k in range(TOP_K):
            _row_copy(h_ref, i * tt + t, xd_ref, dest_ref[t * TOP_K + k], sem, nc).start(
                priority=k % 2)
        return carry

    lax.fori_loop(0, tt, issue, 0)

    def wait_one_tile():
        for k in range(TOP_K):
            pltpu.make_async_copy(h_ref.at[pl.ds(0, tt * nc)], xd_ref.at[pl.ds(0, tt * nc)],
                                  sem).wait()

    @pl.when(i > 0)
    def _():
        wait_one_tile()

    @pl.when(i == pl.num_programs(0) - 1)
    def _():
        wait_one_tile()


def _dispatch(h1c, dest_flat, pstart, pend, rows, s, tt, tm):
    nc = h1c.shape[0] // s
    smem = lambda n: pl.BlockSpec((n,), lambda i, *_: (i,), memory_space=pltpu.SMEM)
    kern = functools.partial(_dispatch_kernel, tt=tt, tm=tm, nc=nc, n_blocks=rows // tm)
    return pl.pallas_call(
        kern,
        grid_spec=pltpu.PrefetchScalarGridSpec(
            num_scalar_prefetch=2,
            grid=(s // tt,),
            in_specs=[smem(tt * TOP_K), pl.BlockSpec(memory_space=pl.ANY)],
            out_specs=pl.BlockSpec(memory_space=pl.ANY),
            scratch_shapes=[pltpu.VMEM((tm * nc, LANES), F32), pltpu.SemaphoreType.DMA(()),
                            pltpu.SemaphoreType.DMA(())]),
        out_shape=jax.ShapeDtypeStruct((rows * nc, LANES), F32),
        compiler_params=_params(),
        name="moe_dispatch",
    )(pstart, pend, dest_flat, h1c)


def _expert_kernel(bexp_ref, nused_ref, x_ref, w1_ref, b1_ref, w2_ref, b2_ref, y_ref,
                   w1b_ref, w2b_ref, *, f, chunk, tm):
    b = pl.program_id(0)
    last = nused_ref[0] - 1
    e = bexp_ref[jnp.minimum(b, last)]
    e_prev = bexp_ref[jnp.maximum(jnp.minimum(b, last) - 1, 0)]

    @pl.when((b == 0) | (e != e_prev))
    def _cast_weights():
        def cast(c, carry):
            r = pl.multiple_of(c * chunk, chunk)
            w1b_ref[pl.ds(r, chunk), :] = w1_ref[pl.ds(r, chunk), :].astype(BF16)
            w2b_ref[pl.ds(r, chunk), :] = w2_ref[pl.ds(r, chunk), :].astype(BF16)
            return carry
        lax.fori_loop(0, w1_ref.shape[0] // chunk, cast, 0)

    @pl.when(b > last)
    def _unused_block():
        y_ref[...] = jnp.zeros(y_ref.shape, y_ref.dtype)

    @pl.when(b <= last)
    def _mlp():
        x = _load_rows(x_ref, tm, w1_ref.shape[0]).astype(BF16)
        hdn = jnp.dot(x, w1b_ref[...], preferred_element_type=F32) + b1_ref[...]
        glu = jnp.minimum(hdn[:, 0:f], SWIGLU_LIMIT)
        lin = jnp.clip(hdn[:, f:2 * f], -SWIGLU_LIMIT, SWIGLU_LIMIT)
        act = glu * _sigmoid(SWIGLU_ALPHA * glu) * (lin + 1.0)
        _store_rows(y_ref, jnp.dot(act.astype(BF16), w2b_ref[...], preferred_element_type=F32)
                    + b2_ref[...])


def _experts(x_disp, bexp, nused, w1, b1, w2, b2, layer, tm):
    d = w1.shape[2]
    f = w2.shape[2]
    nc = d // LANES
    rows = x_disp.shape[0] // nc
    assert f == d, "the weight-cast loop assumes d_expert == d_model"

    def blk(b, bexp_ref, nused_ref):
        return jnp.minimum(b, nused_ref[0] - 1)

    def wmap(b, bexp_ref, nused_ref):
        return (layer, bexp_ref[blk(b, bexp_ref, nused_ref)], 0, 0)

    def bmap(b, bexp_ref, nused_ref):
        return (layer, bexp_ref[blk(b, bexp_ref, nused_ref)], 0, 0)

    kern = functools.partial(_expert_kernel, f=f, chunk=128, tm=tm)
    return pl.pallas_call(
        kern,
        grid_spec=pltpu.PrefetchScalarGridSpec(
            num_scalar_prefetch=2,
            grid=(rows // tm,),
            in_specs=[pl.BlockSpec((tm * nc, LANES), lambda b, be, nu: (b, 0)),
                      pl.BlockSpec((None, None, d, 2 * f), wmap),
                      pl.BlockSpec((None, None, 1, 2 * f), bmap),
                      pl.BlockSpec((None, None, f, d), wmap),
                      pl.BlockSpec((None, None, 1, d), bmap)],
            out_specs=pl.BlockSpec((tm * nc, LANES), lambda b, be, nu: (b, 0)),
            scratch_shapes=[pltpu.VMEM((d, 2 * f), BF16), pltpu.VMEM((f, d), BF16)]),
        out_shape=jax.ShapeDtypeStruct((rows * nc, LANES), F32),
        compiler_params=_params(),
        name="moe_experts",
    )(bexp, nused, x_disp, w1, b1.reshape(b1.shape[0], b1.shape[1], 1, -1), w2,
      b2.reshape(b2.shape[0], b2.shape[1], 1, -1))


def _combine_kernel(dest_ref, h_ref, gate_ref, g_ref, b_ref, yd_ref, o_ref,
                    ybuf_ref, sems, *, tt, nc):
    th = tt // COMBINE_PARTS
    d = h_ref.shape[1]

    def issue(part):
        def one(t, carry):
            for k in range(TOP_K):
                _row_copy(yd_ref, dest_ref[t * TOP_K + k], ybuf_ref.at[k], t, sems.at[part],
                          nc).start(priority=k % 2)
            return carry
        lax.fori_loop(part * th, (part + 1) * th, one, 0)

    def finish(part):
        for k in range(TOP_K):
            pltpu.make_async_copy(yd_ref.at[pl.ds(0, th * nc)],
                                  ybuf_ref.at[k, pl.ds(part * th * nc, th * nc)],
                                  sems.at[part]).wait()
        rows = slice(part * th, (part + 1) * th)
        ff = gate_ref[rows, 0:1] * _load_rows(ybuf_ref, th, d, (0,), part * th)
        for k in range(1, TOP_K):
            ff = ff + gate_ref[rows, k:k + 1] * _load_rows(ybuf_ref, th, d, (k,), part * th)
        o_ref[rows, :] = _layer_norm(RESID_ALPHA * h_ref[rows, :] + ff, g_ref[...], b_ref[...])

    issue(0)
    for part in range(COMBINE_PARTS):
        if part + 1 < COMBINE_PARTS:
            issue(part + 1)
        finish(part)


def _combine(y_disp, h1, gates, dest_flat, g, b, tt):
    s, d = h1.shape
    nc = d // LANES
    row = lambda a: a.reshape(1, -1).astype(F32)
    kern = functools.partial(_combine_kernel, tt=tt, nc=nc)
    return pl.pallas_call(
        kern,
        grid=(s // tt,),
        in_specs=[pl.BlockSpec((tt * TOP_K,), lambda i: (i,), memory_space=pltpu.SMEM),
                  pl.BlockSpec((tt, d), lambda i: (i, 0)),
                  pl.BlockSpec((tt, TOP_K), lambda i: (i, 0)),
                  pl.BlockSpec((1, d), lambda i: (0, 0)),
                  pl.BlockSpec((1, d), lambda i: (0, 0)),
                  pl.BlockSpec(memory_space=pl.ANY)],
        out_specs=pl.BlockSpec((tt, d), lambda i: (i, 0)),
        scratch_shapes=[pltpu.VMEM((TOP_K, tt * nc, LANES), F32),
                        pltpu.SemaphoreType.DMA((COMBINE_PARTS,))],
        out_shape=jax.ShapeDtypeStruct((s, d), F32),
        compiler_params=_params(),
        name="moe_combine",
    )(dest_flat, h1, gates, row(g), row(b), y_disp)


def _tiles(s):
    t = lambda want: math.gcd(s, want)
    return dict(attn=t(512), seq=t(512), route=t(512), expert=t(512), combine=t(512))


def kernel(x, ln_in_g, ln_in_b, w_in, lam_q1, lam_k1, lam_q2, lam_k2, subln_g, lru_conv_w, lru_conv_b, lru_wa, lru_ba, lru_wx, lru_bx, lru_lambda, cf_conv_w, cf_conv_b, cf_ln_g, cf_ln_b, w_out, ln1_g, ln1_b, router_w, router_b, moe_w1, moe_b1, moe_w2, moe_b2, ln2_g, ln2_b):
    bsz, s, d = x.shape
    assert bsz == 1
    tl = _tiles(s)
    tm = tl["expert"]
    n_blocks = (s * TOP_K) // tm + N_EXPERTS
    rows = n_blocks * tm

    h = x.reshape(s, d)
    for l in range(DEPTH):
        lam_init = 0.8 - 0.6 * math.exp(-0.3 * l)
        proj = _inproj(h, ln_in_g, ln_in_b, w_in[l].astype(BF16), tl["attn"], input_ln=(l == 0))
        qt, k, vt, rest = proj[:4]
        if l == 0:
            h = proj[4]
        att = _attention(qt, k, vt, lam_q1[l], lam_k1[l], lam_q2[l], lam_k2[l], subln_g[l],
                         lam_init)
        rc = _seqmix(rest, lru_conv_w[l], lru_conv_b[l], lru_wa[l], lru_ba[l], lru_wx[l],
                     lru_bx[l], lru_lambda[l], cf_conv_w[l], cf_conv_b[l], cf_ln_g[l], cf_ln_b[l],
                     tl["seq"])
        h1, h1c, idx, gates, rank, counts = _outproj_router(
            att, rc, h, w_out[l].astype(BF16), ln1_g[l], ln1_b[l], router_w[l], router_b[l],
            tl["route"])
        counts = counts.reshape(N_EXPERTS)
        padded = (counts + tm - 1) // tm * tm
        e_ids = jnp.arange(N_EXPERTS, dtype=jnp.int32)
        pend = jnp.sum(jnp.where(e_ids[None, :] <= e_ids[:, None], padded[None, :], 0),
                       axis=1).astype(jnp.int32)
        pstart = pend - padded
        block_row = jnp.arange(n_blocks, dtype=jnp.int32) * tm
        bexp = jnp.minimum(jnp.sum((pend[None, :] <= block_row[:, None]).astype(jnp.int32), axis=1),
                           N_EXPERTS - 1).astype(jnp.int32)
        nused = (pend[-1:] // tm).astype(jnp.int32)
        dest_flat = _dest_rows(idx, rank, pstart, tl["route"]).reshape(s * TOP_K)
        x_disp = _dispatch(h1c, dest_flat, pstart, pend, rows, s, tl["route"], tm)
        y_disp = _experts(x_disp, bexp, nused, moe_w1, moe_b1, moe_w2, moe_b2, l, tm)
        h = _combine(y_disp, h1, gates, dest_flat, ln2_g[l], ln2_b[l], tl["combine"])
    return h.reshape(bsz, s, d)
```

```python
import functools
import math

import jax
import jax.numpy as jnp
from jax import lax
from jax.experimental import pallas as pl
from jax.experimental.pallas import tpu as pltpu

F32 = jnp.float32
BF16 = jnp.bfloat16

DEPTH = 2
ATT_HEADS = 4
ATT_QK_DIM = 64
ATT_V_DIM = 128
LRU_BLOCKS = 4
LRU_CONV = 4
LRU_C = 8.0
CONV_KERNEL = 31
N_EXPERTS = 32
TOP_K = 4
SWIGLU_LIMIT = 7.0
SWIGLU_ALPHA = 1.702
LN_EPS = 1e-5
RESID_ALPHA = (2.0 * DEPTH) ** 0.25

VMEM_LIMIT_BYTES = 56 * 1024 * 1024
HALO = 32
LANES = 128
SUBLANES = 8
MXU_DIM = 256
LOG2E = math.log2(math.e)
V_PAD_ROWS = 16
Q_GROUP = 256
COMBINE_PARTS = 4

def _load_rows(ref, n_rows, d, lead=(), first_row=0):
    nc = d // LANES
    return jnp.concatenate([ref[lead + (pl.ds(first_row * nc + c, n_rows, stride=nc), slice(None))]
                            for c in range(nc)], axis=-1)


def _store_rows(ref, val):
    n_rows, d = val.shape
    nc = d // LANES
    for c in range(nc):
        ref[pl.ds(c, n_rows, stride=nc), :] = val[:, c * LANES:(c + 1) * LANES]


def _params(n_axes=1):
    return pltpu.CompilerParams(dimension_semantics=("arbitrary",) * n_axes,
                                vmem_limit_bytes=VMEM_LIMIT_BYTES)


def _layer_norm(x, g, b):
    mu = jnp.mean(x, axis=-1, keepdims=True)
    xc = x - mu
    var = jnp.mean(xc * xc, axis=-1, keepdims=True)
    return xc * lax.rsqrt(var + LN_EPS) * g + b


def _sigmoid(x):
    return 1.0 / (1.0 + jnp.exp(-x))


def _inproj_kernel(h_ref, g_ref, b_ref, w_ref, qt_ref, k_ref, vt_ref, rest_ref, *maybe_h_out,
                   scale):
    nh, hd = ATT_HEADS, ATT_V_DIM
    w_att = nh * hd
    tm = h_ref.shape[0]
    h = h_ref[...]
    if maybe_h_out:
        h = _layer_norm(h, g_ref[...], b_ref[...])
        maybe_h_out[0][...] = h
    hb = h.astype(BF16)
    q = jnp.dot(hb, w_ref[:, 0:w_att], preferred_element_type=F32) * scale
    k_ref[...] = jnp.dot(hb, w_ref[:, w_att:2 * w_att], preferred_element_type=F32).astype(BF16)
    v = jnp.dot(hb, w_ref[:, 2 * w_att:3 * w_att], preferred_element_type=F32)
    ones = jnp.ones((V_PAD_ROWS, tm), BF16)
    for h in range(nh):
        qt_ref[h] = q[:, h * hd:(h + 1) * hd].T.astype(BF16)
        vt_ref[h, 0:hd, :] = v[:, h * hd:(h + 1) * hd].T.astype(BF16)
        vt_ref[h, hd:hd + V_PAD_ROWS, :] = ones
    rest_ref[...] = jnp.dot(hb, w_ref[:, 3 * w_att:], preferred_element_type=F32)


def _inproj(h, ln_g, ln_b, w_bf16, tm, input_ln):
    s, d = h.shape
    n = w_bf16.shape[1]
    nh, hd = ATT_HEADS, ATT_V_DIM
    n_att = 3 * nh * hd
    kern = functools.partial(_inproj_kernel, scale=ATT_QK_DIM ** -0.5 * LOG2E)
    tposed = lambda r: pl.BlockSpec((nh, None, r, tm), lambda i: (0, i, 0, 0))
    rows = pl.BlockSpec((tm, d), lambda i: (i, 0))
    vec = pl.BlockSpec((1, d), lambda i: (0, 0))
    out_specs = [tposed(hd),
                 pl.BlockSpec((tm, nh * hd), lambda i: (i, 0)),
                 tposed(hd + V_PAD_ROWS),
                 pl.BlockSpec((tm, n - n_att), lambda i: (i, 0))]
    out_shape = [jax.ShapeDtypeStruct((nh, s // tm, hd, tm), BF16),
                 jax.ShapeDtypeStruct((s, nh * hd), BF16),
                 jax.ShapeDtypeStruct((nh, s // tm, hd + V_PAD_ROWS, tm), BF16),
                 jax.ShapeDtypeStruct((s, n - n_att), F32)]
    if input_ln:
        out_specs.append(rows)
        out_shape.append(jax.ShapeDtypeStruct((s, d), F32))
    return pl.pallas_call(
        kern,
        grid=(s // tm,),
        in_specs=[rows, vec, vec, pl.BlockSpec((d, n), lambda i: (0, 0))],
        out_specs=out_specs,
        out_shape=out_shape,
        compiler_params=_params(),
        name="inproj",
    )(h, ln_g.reshape(1, d).astype(F32), ln_b.reshape(1, d).astype(F32), w_bf16)


def _split3(x):
    hi = x.astype(BF16)
    rem = x - hi.astype(F32)
    mid = rem.astype(BF16)
    lo = (rem - mid.astype(F32)).astype(BF16)
    return hi, mid, lo


def _attn_kernel(slopes_ref, qt_ref, k_ref, vt_ref, lq1_ref, lk1_ref, lq2_ref, lk2_ref, g_ref,
                 o_ref, qs_ref, kc_ref, mask_ref, *stat_refs, tq, lam_init):
    h = pl.program_id(0)
    qi = pl.program_id(1)
    slope2 = slopes_ref[h]
    dk, hd = ATT_QK_DIM, ATT_V_DIM
    n_groups = (2 * tq) // Q_GROUP
    m_refs, acc_refs = stat_refs[:n_groups], stat_refs[n_groups:2 * n_groups]
    st_refs = stat_refs[2 * n_groups:3 * n_groups]
    mx_refs = stat_refs[3 * n_groups:]

    @pl.when(qi == 0)
    def _build_constants():
        c = lax.broadcasted_iota(jnp.int32, (tq, hd), 0)
        col = lax.broadcasted_iota(jnp.int32, (tq, hd), 1)
        c_lo = jnp.bitwise_and(c, MXU_DIM - 1)
        c_hi = c - c_lo
        kc = jnp.where(col < 3, c_hi, jnp.where(col < 6, c_lo, jnp.where(col < 9, 1, 0)))
        kc_ref[...] = kc.astype(F32).astype(BF16)
        r = lax.broadcasted_iota(jnp.int32, (hd, 2 * tq), 1)
        r = jnp.where(r >= tq, r - tq, r)
        row = lax.broadcasted_iota(jnp.int32, (hd, 2 * tq), 0)
        sl = jnp.full((hd, 2 * tq), slope2, F32)
        s_hi, s_mid, s_lo = _split3(sl)
        t_hi, t_mid, t_lo = _split3(-(sl * r.astype(F32)))
        aug = jnp.zeros((hd, 2 * tq), F32)
        for i, piece in enumerate((s_hi, s_mid, s_lo, s_hi, s_mid, s_lo, t_hi, t_mid, t_lo)):
            aug = jnp.where(row == i, piece.astype(F32), aug)
        qs_ref[hd:2 * hd, :] = aug.astype(BF16)
        ck = lax.broadcasted_iota(jnp.int32, (tq, 2 * tq), 0)
        rq = lax.broadcasted_iota(jnp.int32, (tq, 2 * tq), 1)
        rq = jnp.where(rq >= tq, rq - tq, rq)
        mask_ref[...] = jnp.where(ck <= rq, 0.0, -jnp.inf)

    qt = qt_ref[...]
    dim = lax.broadcasted_iota(jnp.int32, qt.shape, 0)
    zero_q = jnp.zeros_like(qt)
    qs_ref[0:hd, 0:tq] = jnp.where(dim < dk, qt, zero_q)
    qs_ref[0:hd, tq:2 * tq] = jnp.where(dim >= dk, qt, zero_q)
    for m_ref, acc_ref in zip(m_refs, acc_refs):
        m_ref[...] = jnp.full(m_ref.shape, -jnp.inf, F32)
        acc_ref[...] = jnp.zeros(acc_ref.shape, F32)

    cols = [slice(g * Q_GROUP, (g + 1) * Q_GROUP) for g in range(n_groups)]

    def keys_of(j):
        start = pl.multiple_of(j * tq, tq)
        return jnp.concatenate([k_ref[pl.ds(start, tq), :], kc_ref[...]], axis=1)

    def qk(ka, g):
        st = jnp.dot(ka, qs_ref[:, cols[g]], preferred_element_type=F32)
        return st, jnp.max(st, axis=0, keepdims=True)

    def prefetch(ka, g):
        st_refs[g][...], mx_refs[g][...] = qk(ka, g)

    def softmax_pv(st, mx, j, vt, g, diagonal):
        if diagonal:
            st = st + mask_ref[:, cols[g]]
            mx = jnp.max(st, axis=0, keepdims=True)
        off = slope2 * ((j - qi) * tq).astype(F32)
        m_old = m_refs[g][...]
        m_new = jnp.maximum(m_old, mx + off)
        p = jnp.exp2(st - (m_new - off)).astype(BF16)
        alpha = jnp.exp2(m_old - m_new)
        acc_refs[g][...] = alpha * acc_refs[g][...] + jnp.dot(vt, p, preferred_element_type=F32)
        m_refs[g][...] = m_new

    def one_block(j_a, j_next, diagonal):
        vt_a = vt_ref[j_a]
        ka_n = keys_of(j_next) if j_next is not None else None
        for g in range(n_groups):
            softmax_pv(st_refs[g][...], mx_refs[g][...], j_a, vt_a, g, diagonal)
            if ka_n is not None:
                prefetch(ka_n, g)

    def two_blocks(j_a, j_b, j_next):
        vt_a, vt_b = vt_ref[j_a], vt_ref[j_b]
        ka_b, ka_n = keys_of(j_b), keys_of(j_next)
        s_b = {}
        for g in range(n_groups):
            softmax_pv(st_refs[g][...], mx_refs[g][...], j_a, vt_a, g, False)
            s_b[g] = qk(ka_b, g)
        for g in range(n_groups):
            softmax_pv(*s_b.pop(g), j_b, vt_b, g, False)
            prefetch(ka_n, g)

    ka_0 = keys_of(0)
    for g in range(n_groups):
        prefetch(ka_0, g)

    def body(pair, carry):
        two_blocks(2 * pair, 2 * pair + 1, 2 * pair + 2)
        return carry

    lax.fori_loop(0, lax.shift_right_logical(qi, 1), body, 0)

    @pl.when(jnp.bitwise_and(qi, 1) == 1)
    def _odd_block():
        one_block(qi - 1, qi, False)

    one_block(qi, None, True)

    lam = (jnp.exp(jnp.sum(lq1_ref[...] * lk1_ref[...], axis=-1, keepdims=True))
           - jnp.exp(jnp.sum(lq2_ref[...] * lk2_ref[...], axis=-1, keepdims=True)) + lam_init)
    ot = jnp.concatenate([a[0:hd, :] / a[hd:hd + 1, :] for a in acc_refs], axis=1)
    o = ot[:, 0:tq] - lam * ot[:, tq:2 * tq]
    o = o * lax.rsqrt(jnp.mean(o * o, axis=0, keepdims=True) + LN_EPS)
    o_ref[...] = (o * g_ref[...] * (1.0 - lam_init)).T.astype(o_ref.dtype)


def _attention(qt, k, vt, lq1, lk1, lq2, lk2, subln_g, lam_init):
    nh, nblk, hd, tq = qt.shape
    vrows = vt.shape[2]
    s = k.shape[0]
    assert (2 * tq) % Q_GROUP == 0 and hd + 9 <= MXU_DIM
    n_groups = (2 * tq) // Q_GROUP
    slopes = jnp.exp2(-8.0 * (jnp.arange(nh, dtype=F32) + 1.0) / nh) * LOG2E
    vec = lambda a: a.reshape(1, -1).astype(F32)
    small = lambda n: pl.BlockSpec((1, n), lambda h, i, *_: (0, 0))
    kern = functools.partial(_attn_kernel, tq=tq, lam_init=lam_init)
    return pl.pallas_call(
        kern,
        grid_spec=pltpu.PrefetchScalarGridSpec(
            num_scalar_prefetch=1,
            grid=(nh, nblk),
            in_specs=[pl.BlockSpec((None, None, hd, tq), lambda h, i, *_: (h, i, 0, 0)),
                      pl.BlockSpec((s, hd), lambda h, i, *_: (0, h)),
                      pl.BlockSpec((None, nblk, vrows, tq), lambda h, i, *_: (h, 0, 0, 0)),
                      small(ATT_QK_DIM), small(ATT_QK_DIM), small(ATT_QK_DIM), small(ATT_QK_DIM),
                      pl.BlockSpec((hd, 1), lambda h, i, *_: (0, 0))],
            out_specs=pl.BlockSpec((tq, hd), lambda h, i, *_: (i, h)),
            scratch_shapes=[pltpu.VMEM((2 * hd, 2 * tq), BF16),
                            pltpu.VMEM((tq, hd), BF16),
                            pltpu.VMEM((tq, 2 * tq), F32)]
            + [pltpu.VMEM((1, Q_GROUP), F32)] * n_groups
            + [pltpu.VMEM((vrows, Q_GROUP), F32)] * n_groups
            + [pltpu.VMEM((tq, Q_GROUP), F32)] * n_groups
            + [pltpu.VMEM((1, Q_GROUP), F32)] * n_groups),
        out_shape=jax.ShapeDtypeStruct((s, nh * hd), BF16),
        compiler_params=_params(2),
        name="diff_attention",
    )(slopes, qt, k, vt, vec(lq1), vec(lk1), vec(lq2), vec(lk2),
      subln_g.reshape(hd, 1).astype(F32))


def _seqmix_kernel(u_ref, lcw_ref, lcb_ref, wa_ref, ba_ref, wx_ref, bx_ref, lam_ref,
                   ccw_ref, ccb_ref, cg_ref, cb_ref, o_ref,
                   xbuf_ref, cbuf_ref, shift_ref, a_ref, b_ref, hs_ref, hc_ref, *, tt, w):
    i = pl.program_id(0)

    @pl.when(i == 0)
    def _init():
        xbuf_ref[0:HALO, :] = jnp.zeros((HALO, w), F32)
        cbuf_ref[0:HALO, :] = jnp.zeros((HALO, w), F32)
        hc_ref[...] = jnp.zeros(hc_ref.shape, F32)

    xbuf_ref[HALO:HALO + tt, :] = u_ref[:, 0:w]
    xc = jnp.zeros((tt, w), F32)
    for j in range(LRU_CONV):
        off = HALO - (LRU_CONV - 1) + j
        xc = xc + lcw_ref[j:j + 1, :] * xbuf_ref[off:off + tt, :]
    xc = xc + lcb_ref[...]
    xcb = xc.astype(BF16)
    gate_a = _sigmoid(jnp.dot(xcb, wa_ref[...], preferred_element_type=F32) + ba_ref[...])
    gate_x = _sigmoid(jnp.dot(xcb, wx_ref[...], preferred_element_type=F32) + bx_ref[...])
    nl = -lam_ref[...]
    softplus = jnp.maximum(nl, 0.0) + jnp.log(1.0 + jnp.exp(-jnp.abs(nl)))
    log_a = -LRU_C * gate_a * softplus
    a_ref[...] = jnp.exp(log_a)
    b_ref[...] = jnp.sqrt(1.0 - jnp.exp(2.0 * log_a)) * gate_x * xc

    def step(t, hprev):
        hnew = a_ref[pl.ds(t, 1), :] * hprev + b_ref[pl.ds(t, 1), :]
        hs_ref[pl.ds(t, 1), :] = hnew
        return hnew

    hc_ref[...] = lax.fori_loop(0, tt, step, hc_ref[...], unroll=8)
    o_ref[:, 0:w] = (hs_ref[...] * jax.nn.gelu(u_ref[:, w:2 * w], approximate=True)).astype(o_ref.dtype)
    xbuf_ref[0:HALO, :] = xbuf_ref[tt:tt + HALO, :]

    cbuf_ref[HALO:HALO + tt, :] = u_ref[:, 2 * w:3 * w] * _sigmoid(u_ref[:, 3 * w:4 * w])
    span = shift_ref.shape[1]
    for p in range(1, SUBLANES):
        shift_ref[p - 1] = cbuf_ref[p:p + span, :]
    y = jnp.zeros((tt, w), F32)
    for j in range(CONV_KERNEL):
        off = HALO - (CONV_KERNEL - 1) + j
        p, base = off % SUBLANES, off - off % SUBLANES
        win = cbuf_ref[base:base + tt, :] if p == 0 else shift_ref[p - 1, base:base + tt, :]
        y = y + ccw_ref[j:j + 1, :] * win
    y = _layer_norm(y + ccb_ref[...], cg_ref[...], cb_ref[...])
    o_ref[:, w:2 * w] = (y * _sigmoid(y)).astype(o_ref.dtype)
    cbuf_ref[0:HALO, :] = cbuf_ref[tt:tt + HALO, :]


def _block_diag(wb):
    nb, bd, _ = wb.shape
    eye = jnp.eye(nb, dtype=jnp.bool_)
    return jnp.where(eye[:, None, :, None], wb[:, :, None, :], 0.0).reshape(nb * bd, nb * bd)


def _seqmix(rest, lcw, lcb, wa, ba, wx, bx, lam, ccw, ccb, cg, cb, tt):
    s, n = rest.shape
    w = n // 4
    row = lambda a: a.reshape(1, w).astype(F32)
    full = lambda r, c: pl.BlockSpec((r, c), lambda i: (0, 0))
    kern = functools.partial(_seqmix_kernel, tt=tt, w=w)
    return pl.pallas_call(
        kern,
        grid=(s // tt,),
        in_specs=[pl.BlockSpec((tt, n), lambda i: (i, 0)),
                  full(LRU_CONV, w), full(1, w), full(w, w), full(1, w), full(w, w), full(1, w),
                  full(1, w), full(CONV_KERNEL, w), full(1, w), full(1, w), full(1, w)],
        out_specs=pl.BlockSpec((tt, 2 * w), lambda i: (i, 0)),
        out_shape=jax.ShapeDtypeStruct((s, 2 * w), BF16),
        scratch_shapes=[pltpu.VMEM((tt + HALO, w), F32), pltpu.VMEM((tt + HALO, w), F32),
                        pltpu.VMEM((SUBLANES - 1, tt + HALO - SUBLANES, w), F32),
                        pltpu.VMEM((tt, w), F32), pltpu.VMEM((tt, w), F32), pltpu.VMEM((tt, w), F32),
                        pltpu.VMEM((1, w), F32)],
        compiler_params=_params(),
        name="seqmix",
    )(rest, lcw, row(lcb), _block_diag(wa).astype(BF16), row(ba), _block_diag(wx).astype(BF16),
      row(bx), row(lam), ccw, row(ccb), row(cg), row(cb))


def _outproj_kernel(att_ref, rc_ref, h_ref, wo_ref, g_ref, b_ref, rwh_ref, rwl_ref, rb_ref,
                    h1_ref, h1c_ref, idx_ref, gate_ref, rank_ref, cnt_ref, tri_ref, carry_ref,
                    *, tm, n_att):
    i = pl.program_id(0)
    ne = N_EXPERTS

    @pl.when(i == 0)
    def _init():
        r = lax.broadcasted_iota(jnp.int32, (tm, tm), 0)
        c = lax.broadcasted_iota(jnp.int32, (tm, tm), 1)
        tri_ref[...] = jnp.where(c < r, 1.0, 0.0).astype(BF16)
        carry_ref[...] = jnp.zeros(carry_ref.shape, F32)

    mix = (jnp.dot(att_ref[...], wo_ref[0:n_att, :], preferred_element_type=F32)
           + jnp.dot(rc_ref[...], wo_ref[n_att:, :], preferred_element_type=F32))
    h1 = _layer_norm(RESID_ALPHA * h_ref[...] + mix, g_ref[...], b_ref[...])
    h1_ref[...] = h1
    _store_rows(h1c_ref, h1)

    h1_hi = h1.astype(BF16)
    h1_lo = (h1 - h1_hi.astype(F32)).astype(BF16)
    logits = (jnp.dot(h1_hi, rwh_ref[...], preferred_element_type=F32)
              + jnp.dot(h1_hi, rwl_ref[...], preferred_element_type=F32)
              + jnp.dot(h1_lo, rwh_ref[...], preferred_element_type=F32)) + rb_ref[...]
    lane = lax.broadcasted_iota(jnp.int32, (tm, ne), 1).astype(F32)
    onehot = jnp.zeros((tm, ne), F32)
    vals, sels = [], []
    for _ in range(TOP_K):
        mx = jnp.max(logits, axis=-1, keepdims=True)
        sel = jnp.min(jnp.where(logits == mx, lane, float(ne)), axis=-1, keepdims=True)
        hit = lane == sel
        onehot = onehot + jnp.where(hit, 1.0, 0.0)
        logits = jnp.where(hit, -jnp.inf, logits)
        vals.append(mx)
        sels.append(sel)
    ex = [jnp.exp(v - vals[0]) for v in vals]
    den = ex[0] + ex[1] + ex[2] + ex[3]
    before = jnp.dot(tri_ref[...], onehot.astype(BF16), preferred_element_type=F32) + carry_ref[...]
    for k in range(TOP_K):
        idx_ref[:, k:k + 1] = sels[k].astype(jnp.int32)
        gate_ref[:, k:k + 1] = ex[k] / den
        rank_ref[:, k:k + 1] = jnp.sum(jnp.where(lane == sels[k], before, 0.0), axis=-1,
                                       keepdims=True).astype(jnp.int32)
    carry_ref[...] = carry_ref[...] + jnp.sum(onehot, axis=0, keepdims=True)
    cnt_ref[...] = carry_ref[...].astype(jnp.int32)


def _outproj_router(att, rc, h, wo_bf16, g, b, rw, rb, tm):
    s, d = h.shape
    n_att = att.shape[1]
    ne = N_EXPERTS
    row = lambda a: a.reshape(1, -1).astype(F32)
    full = lambda r, c: pl.BlockSpec((r, c), lambda i: (0, 0))
    tile = lambda c: pl.BlockSpec((tm, c), lambda i: (i, 0))
    kern = functools.partial(_outproj_kernel, tm=tm, n_att=n_att)
    rw_hi = rw.astype(BF16)
    rw_lo = (rw - rw_hi.astype(F32)).astype(BF16)
    return pl.pallas_call(
        kern,
        grid=(s // tm,),
        in_specs=[tile(n_att), tile(rc.shape[1]), tile(d), full(wo_bf16.shape[0], d),
                  full(1, d), full(1, d), full(d, ne), full(d, ne), full(1, ne)],
        out_specs=[tile(d), pl.BlockSpec((tm * (d // LANES), LANES), lambda i: (i, 0)),
                   tile(TOP_K), tile(TOP_K), tile(TOP_K), full(1, ne)],
        out_shape=[jax.ShapeDtypeStruct((s, d), F32),
                   jax.ShapeDtypeStruct((s * (d // LANES), LANES), F32),
                   jax.ShapeDtypeStruct((s, TOP_K), jnp.int32),
                   jax.ShapeDtypeStruct((s, TOP_K), F32),
                   jax.ShapeDtypeStruct((s, TOP_K), jnp.int32),
                   jax.ShapeDtypeStruct((1, ne), jnp.int32)],
        scratch_shapes=[pltpu.VMEM((tm, tm), BF16), pltpu.VMEM((1, ne), F32)],
        compiler_params=_params(),
        name="outproj_router",
    )(att, rc, h, wo_bf16, row(g), row(b), rw_hi, rw_lo, row(rb))


def _row_copy(src_ref, src_row, dst_ref, dst_row, sem, nc):
    src = src_ref.at[pl.ds(pl.multiple_of(src_row * nc, nc), nc)]
    dst = dst_ref.at[pl.ds(pl.multiple_of(dst_row * nc, nc), nc)]
    return pltpu.make_async_copy(src, dst, sem)


def _dest_kernel(idx_ref, rank_ref, pstart_ref, dest_ref):
    tm, ne = idx_ref.shape[0], pstart_ref.shape[1]
    lane = lax.broadcasted_iota(jnp.int32, (tm, ne), 1)
    pstart = pstart_ref[...].astype(F32)
    for k in range(TOP_K):
        hit = lane == idx_ref[:, k:k + 1]
        start = jnp.sum(jnp.where(hit, pstart, 0.0), axis=-1, keepdims=True)
        dest_ref[:, k:k + 1] = start.astype(jnp.int32) + rank_ref[:, k:k + 1]


def _dest_rows(idx, rank, pstart, tm):
    s = idx.shape[0]
    ne = pstart.shape[0]
    tile = pl.BlockSpec((tm, TOP_K), lambda i: (i, 0))
    return pl.pallas_call(
        _dest_kernel,
        grid=(s // tm,),
        in_specs=[tile, tile, pl.BlockSpec((1, ne), lambda i: (0, 0))],
        out_specs=tile,
        out_shape=jax.ShapeDtypeStruct((s, TOP_K), jnp.int32),
        compiler_params=_params(),
        name="moe_dest_rows",
    )(idx, rank, pstart.reshape(1, ne))


def _dispatch_kernel(pstart_ref, pend_ref, dest_ref, h_ref, xd_ref, zero_ref, sem, zsem,
                     *, tt, tm, nc, n_blocks):
    i = pl.program_id(0)

    def zero_copy(e):
        start = pl.multiple_of((pend_ref[e] - tm) * nc, tm * nc)
        return pltpu.make_async_copy(zero_ref, xd_ref.at[pl.ds(start, tm * nc)], zsem)

    def tail_copy(b):
        start = pl.multiple_of(b * (tm * nc), tm * nc)
        return pltpu.make_async_copy(zero_ref, xd_ref.at[pl.ds(start, tm * nc)], zsem)

    @pl.when(i == 0)
    def _clear_padding():
        zero_ref[...] = jnp.zeros(zero_ref.shape, zero_ref.dtype)
        n_used = pend_ref[N_EXPERTS - 1] // tm

        def start_tail(b, carry):
            tail_copy(b).start()
            return carry

        def wait_tail(b, carry):
            tail_copy(b).wait()
            return carry

        lax.fori_loop(n_used, n_blocks, start_tail, 0)
        lax.fori_loop(n_used, n_blocks, wait_tail, 0)
        for e in range(N_EXPERTS):
            @pl.when(pend_ref[e] > pstart_ref[e])
            def _():
                zero_copy(e).start()
        for e in range(N_EXPERTS):
            @pl.when(pend_ref[e] > pstart_ref[e])
            def _():
                zero_copy(e).wait()

    def issue(t, carry):
        for k in range(TOP_K):
            _row_copy(h_ref, t, xd_ref, dest_ref[t * TOP_K + k], sem, nc).start(priority=k % 2)
        return carry

    lax.fori_loop(0, tt, issue, 0)

    for k in range(TOP_K):
        pltpu.make_async_copy(h_ref, xd_ref.at[pl.ds(0, tt * nc)], sem).wait()


def _dispatch(h1c, dest_flat, pstart, pend, rows, s, tt, tm):
    nc = h1c.shape[0] // s
    smem = lambda n: pl.BlockSpec((n,), lambda i, *_: (i,), memory_space=pltpu.SMEM)
    kern = functools.partial(_dispatch_kernel, tt=tt, tm=tm, nc=nc, n_blocks=rows // tm)
    return pl.pallas_call(
        kern,
        grid_spec=pltpu.PrefetchScalarGridSpec(
            num_scalar_prefetch=2,
            grid=(s // tt,),
            in_specs=[smem(tt * TOP_K),
                      pl.BlockSpec((tt * nc, LANES), lambda i, *_: (i, 0))],
            out_specs=pl.BlockSpec(memory_space=pl.ANY),
            scratch_shapes=[pltpu.VMEM((tm * nc, LANES), F32), pltpu.SemaphoreType.DMA(()),
                            pltpu.SemaphoreType.DMA(())]),
        out_shape=jax.ShapeDtypeStruct((rows * nc, LANES), F32),
        compiler_params=_params(),
        name="moe_dispatch",
    )(pstart, pend, dest_flat, h1c)


def _expert_kernel(bexp_ref, nused_ref, x_ref, w1_ref, b1_ref, w2_ref, b2_ref, y_ref,
                   w1b_ref, w2b_ref, *, f, chunk, tm):
    b = pl.program_id(0)
    last = nused_ref[0] - 1
    e = bexp_ref[jnp.minimum(b, last)]
    e_prev = bexp_ref[jnp.maximum(jnp.minimum(b, last) - 1, 0)]

    @pl.when((b == 0) | (e != e_prev))
    def _cast_weights():
        def cast(c, carry):
            r = pl.multiple_of(c * chunk, chunk)
            w1b_ref[pl.ds(r, chunk), :] = w1_ref[pl.ds(r, chunk), :].astype(BF16)
            w2b_ref[pl.ds(r, chunk), :] = w2_ref[pl.ds(r, chunk), :].astype(BF16)
            return carry
        lax.fori_loop(0, w1_ref.shape[0] // chunk, cast, 0)

    @pl.when(b > last)
    def _unused_block():
        y_ref[...] = jnp.zeros(y_ref.shape, y_ref.dtype)

    @pl.when(b <= last)
    def _mlp():
        x = _load_rows(x_ref, tm, w1_ref.shape[0]).astype(BF16)
        hdn = jnp.dot(x, w1b_ref[...], preferred_element_type=F32) + b1_ref[...]
        glu = jnp.minimum(hdn[:, 0:f], SWIGLU_LIMIT)
        lin = jnp.clip(hdn[:, f:2 * f], -SWIGLU_LIMIT, SWIGLU_LIMIT)
        act = glu * _sigmoid(SWIGLU_ALPHA * glu) * (lin + 1.0)
        _store_rows(y_ref, jnp.dot(act.astype(BF16), w2b_ref[...], preferred_element_type=F32)
                    + b2_ref[...])


def _experts(x_disp, bexp, nused, w1, b1, w2, b2, layer, tm):
    d = w1.shape[2]
    f = w2.shape[2]
    nc = d // LANES
    rows = x_disp.shape[0] // nc
    assert f == d, "the weight-cast loop assumes d_expert == d_model"

    def blk(b, bexp_ref, nused_ref):
        return jnp.minimum(b, nused_ref[0] - 1)

    def wmap(b, bexp_ref, nused_ref):
        return (layer, bexp_ref[blk(b, bexp_ref, nused_ref)], 0, 0)

    def bmap(b, bexp_ref, nused_ref):
        return (layer, bexp_ref[blk(b, bexp_ref, nused_ref)], 0, 0)

    kern = functools.partial(_expert_kernel, f=f, chunk=128, tm=tm)
    return pl.pallas_call(
        kern,
        grid_spec=pltpu.PrefetchScalarGridSpec(
            num_scalar_prefetch=2,
            grid=(rows // tm,),
            in_specs=[pl.BlockSpec((tm * nc, LANES), lambda b, be, nu: (b, 0)),
                      pl.BlockSpec((None, None, d, 2 * f), wmap),
                      pl.BlockSpec((None, None, 1, 2 * f), bmap),
                      pl.BlockSpec((None, None, f, d), wmap),
                      pl.BlockSpec((None, None, 1, d), bmap)],
            out_specs=pl.BlockSpec((tm * nc, LANES), lambda b, be, nu: (b, 0)),
            scratch_shapes=[pltpu.VMEM((d, 2 * f), BF16), pltpu.VMEM((f, d), BF16)]),
        out_shape=jax.ShapeDtypeStruct((rows * nc, LANES), F32),
        compiler_params=_params(),
        name="moe_experts",
    )(bexp, nused, x_disp, w1, b1.reshape(b1.shape[0], b1.shape[1], 1, -1), w2,
      b2.reshape(b2.shape[0], b2.shape[1], 1, -1))


def _combine_kernel(dest_ref, h_ref, gate_ref, g_ref, b_ref, yd_ref, o_ref,
                    ybuf_ref, sems, *, tt, nc):
    th = tt // COMBINE_PARTS
    d = h_ref.shape[1]

    def issue(part):
        def one(t, carry):
            for k in range(TOP_K):
                _row_copy(yd_ref, dest_ref[t * TOP_K + k], ybuf_ref.at[k], t, sems.at[part],
                          nc).start(priority=k % 2)
            return carry
        lax.fori_loop(part * th, (part + 1) * th, one, 0)

    def finish(part):
        for k in range(TOP_K):
            pltpu.make_async_copy(yd_ref.at[pl.ds(0, th * nc)],
                                  ybuf_ref.at[k, pl.ds(part * th * nc, th * nc)],
                                  sems.at[part]).wait()
        rows = slice(part * th, (part + 1) * th)
        ff = gate_ref[rows, 0:1] * _load_rows(ybuf_ref, th, d, (0,), part * th)
        for k in range(1, TOP_K):
            ff = ff + gate_ref[rows, k:k + 1] * _load_rows(ybuf_ref, th, d, (k,), part * th)
        o_ref[rows, :] = _layer_norm(RESID_ALPHA * h_ref[rows, :] + ff, g_ref[...], b_ref[...])

    issue(0)
    for part in range(COMBINE_PARTS):
        if part + 1 < COMBINE_PARTS:
            issue(part + 1)
        finish(part)


def _combine(y_disp, h1, gates, dest_flat, g, b, tt):
    s, d = h1.shape
    nc = d // LANES
    row = lambda a: a.reshape(1, -1).astype(F32)
    kern = functools.partial(_combine_kernel, tt=tt, nc=nc)
    return pl.pallas_call(
        kern,
        grid=(s // tt,),
        in_specs=[pl.BlockSpec((tt * TOP_K,), lambda i: (i,), memory_space=pltpu.SMEM),
                  pl.BlockSpec((tt, d), lambda i: (i, 0)),
                  pl.BlockSpec((tt, TOP_K), lambda i: (i, 0)),
                  pl.BlockSpec((1, d), lambda i: (0, 0)),
                  pl.BlockSpec((1, d), lambda i: (0, 0)),
                  pl.BlockSpec(memory_space=pl.ANY)],
        out_specs=pl.BlockSpec((tt, d), lambda i: (i, 0)),
        scratch_shapes=[pltpu.VMEM((TOP_K, tt * nc, LANES), F32),
                        pltpu.SemaphoreType.DMA((COMBINE_PARTS,))],
        out_shape=jax.ShapeDtypeStruct((s, d), F32),
        compiler_params=_params(),
        name="moe_combine",
    )(dest_flat, h1, gates, row(g), row(b), y_disp)


def _tiles(s):
    t = lambda want: math.gcd(s, want)
    return dict(attn=t(512), seq=t(512), route=t(512), dest=t(2048), dispatch=t(1024),
                expert=t(512), combine=t(512))


def kernel(x, ln_in_g, ln_in_b, w_in, lam_q1, lam_k1, lam_q2, lam_k2, subln_g, lru_conv_w, lru_conv_b, lru_wa, lru_ba, lru_wx, lru_bx, lru_lambda, cf_conv_w, cf_conv_b, cf_ln_g, cf_ln_b, w_out, ln1_g, ln1_b, router_w, router_b, moe_w1, moe_b1, moe_w2, moe_b2, ln2_g, ln2_b):
    bsz, s, d = x.shape
    assert bsz == 1
    tl = _tiles(s)
    tm = tl["expert"]
    n_blocks = (s * TOP_K) // tm + N_EXPERTS
    rows = n_blocks * tm

    h = x.reshape(s, d)
    for l in range(DEPTH):
        lam_init = 0.8 - 0.6 * math.exp(-0.3 * l)
        proj = _inproj(h, ln_in_g, ln_in_b, w_in[l].astype(BF16), tl["attn"], input_ln=(l == 0))
        qt, k, vt, rest = proj[:4]
        if l == 0:
            h = proj[4]
        att = _attention(qt, k, vt, lam_q1[l], lam_k1[l], lam_q2[l], lam_k2[l], subln_g[l],
                         lam_init)
        rc = _seqmix(rest, lru_conv_w[l], lru_conv_b[l], lru_wa[l], lru_ba[l], lru_wx[l],
                     lru_bx[l], lru_lambda[l], cf_conv_w[l], cf_conv_b[l], cf_ln_g[l], cf_ln_b[l],
                     tl["seq"])
        h1, h1c, idx, gates, rank, counts = _outproj_router(
            att, rc, h, w_out[l].astype(BF16), ln1_g[l], ln1_b[l], router_w[l], router_b[l],
            tl["route"])
        counts = counts.reshape(N_EXPERTS)
        padded = (counts + tm - 1) // tm * tm
        e_ids = jnp.arange(N_EXPERTS, dtype=jnp.int32)
        pend = jnp.sum(jnp.where(e_ids[None, :] <= e_ids[:, None], padded[None, :], 0),
                       axis=1).astype(jnp.int32)
        pstart = pend - padded
        block_row = jnp.arange(n_blocks, dtype=jnp.int32) * tm
        bexp = jnp.minimum(jnp.sum((pend[None, :] <= block_row[:, None]).astype(jnp.int32), axis=1),
                           N_EXPERTS - 1).astype(jnp.int32)
        nused = (pend[-1:] // tm).astype(jnp.int32)
        dest_flat = _dest_rows(idx, rank, pstart, tl["dest"]).reshape(s * TOP_K)
        x_disp = _dispatch(h1c, dest_flat, pstart, pend, rows, s, tl["dispatch"], tm)
        y_disp = _experts(x_disp, bexp, nused, moe_w1, moe_b1, moe_w2, moe_b2, l, tm)
        h = _combine(y_disp, h1, gates, dest_flat, ln2_g[l], ln2_b[l], tl["combine"])
    return h.reshape(bsz, s, d)
```

```python
import functools
import math

import jax
import jax.numpy as jnp
from jax import lax
from jax.experimental import pallas as pl
from jax.experimental.pallas import tpu as pltpu

F32 = jnp.float32
BF16 = jnp.bfloat16

DEPTH = 2
ATT_HEADS = 4
ATT_QK_DIM = 64
ATT_V_DIM = 128
LRU_BLOCKS = 4
LRU_CONV = 4
LRU_C = 8.0
CONV_KERNEL = 31
N_EXPERTS = 32
TOP_K = 4
SWIGLU_LIMIT = 7.0
SWIGLU_ALPHA = 1.702
LN_EPS = 1e-5
RESID_ALPHA = (2.0 * DEPTH) ** 0.25

VMEM_LIMIT_BYTES = 56 * 1024 * 1024
HALO = 32
LANES = 128
SUBLANES = 8
MXU_DIM = 256
LOG2E = math.log2(math.e)
V_PAD_ROWS = 16
Q_GROUP = 256
COMBINE_PARTS = 4

def _load_rows(ref, n_rows, d, lead=(), first_row=0):
    nc = d // LANES
    return jnp.concatenate([ref[lead + (pl.ds(first_row * nc + c, n_rows, stride=nc), slice(None))]
                            for c in range(nc)], axis=-1)


def _store_rows(ref, val):
    n_rows, d = val.shape
    nc = d // LANES
    for c in range(nc):
        ref[pl.ds(c, n_rows, stride=nc), :] = val[:, c * LANES:(c + 1) * LANES]


def _params(n_axes=1):
    return pltpu.CompilerParams(dimension_semantics=("arbitrary",) * n_axes,
                                vmem_limit_bytes=VMEM_LIMIT_BYTES)


def _layer_norm(x, g, b):
    mu = jnp.mean(x, axis=-1, keepdims=True)
    xc = x - mu
    var = jnp.mean(xc * xc, axis=-1, keepdims=True)
    return xc * lax.rsqrt(var + LN_EPS) * g + b


def _sigmoid(x):
    return 1.0 / (1.0 + jnp.exp(-x))


def _inproj_kernel(h_ref, g_ref, b_ref, w_ref, qt_ref, k_ref, vt_ref, rest_ref, *maybe_h_out,
                   scale):
    nh, hd = ATT_HEADS, ATT_V_DIM
    w_att = nh * hd
    tm = h_ref.shape[0]
    h = h_ref[...]
    if maybe_h_out:
        h = _layer_norm(h, g_ref[...], b_ref[...])
        maybe_h_out[0][...] = h
    hb = h.astype(BF16)
    q = jnp.dot(hb, w_ref[:, 0:w_att], preferred_element_type=F32) * scale
    k_ref[...] = jnp.dot(hb, w_ref[:, w_att:2 * w_att], preferred_element_type=F32).astype(BF16)
    v = jnp.dot(hb, w_ref[:, 2 * w_att:3 * w_att], preferred_element_type=F32)
    ones = jnp.ones((V_PAD_ROWS, tm), BF16)
    for h in range(nh):
        qt_ref[h] = q[:, h * hd:(h + 1) * hd].T.astype(BF16)
        vt_ref[h, 0:hd, :] = v[:, h * hd:(h + 1) * hd].T.astype(BF16)
        vt_ref[h, hd:hd + V_PAD_ROWS, :] = ones
    rest_ref[...] = jnp.dot(hb, w_ref[:, 3 * w_att:], preferred_element_type=F32)


def _inproj(h, ln_g, ln_b, w_bf16, tm, input_ln):
    s, d = h.shape
    n = w_bf16.shape[1]
    nh, hd = ATT_HEADS, ATT_V_DIM
    n_att = 3 * nh * hd
    kern = functools.partial(_inproj_kernel, scale=ATT_QK_DIM ** -0.5 * LOG2E)
    tposed = lambda r: pl.BlockSpec((nh, None, r, tm), lambda i: (0, i, 0, 0))
    rows = pl.BlockSpec((tm, d), lambda i: (i, 0))
    vec = pl.BlockSpec((1, d), lambda i: (0, 0))
    out_specs = [tposed(hd),
                 pl.BlockSpec((tm, nh * hd), lambda i: (i, 0)),
                 tposed(hd + V_PAD_ROWS),
                 pl.BlockSpec((tm, n - n_att), lambda i: (i, 0))]
    out_shape = [jax.ShapeDtypeStruct((nh, s // tm, hd, tm), BF16),
                 jax.ShapeDtypeStruct((s, nh * hd), BF16),
                 jax.ShapeDtypeStruct((nh, s // tm, hd + V_PAD_ROWS, tm), BF16),
                 jax.ShapeDtypeStruct((s, n - n_att), F32)]
    if input_ln:
        out_specs.append(rows)
        out_shape.append(jax.ShapeDtypeStruct((s, d), F32))
    return pl.pallas_call(
        kern,
        grid=(s // tm,),
        in_specs=[rows, vec, vec, pl.BlockSpec((d, n), lambda i: (0, 0))],
        out_specs=out_specs,
        out_shape=out_shape,
        compiler_params=_params(),
        name="inproj",
    )(h, ln_g.reshape(1, d).astype(F32), ln_b.reshape(1, d).astype(F32), w_bf16)


def _split3(x):
    hi = x.astype(BF16)
    rem = x - hi.astype(F32)
    mid = rem.astype(BF16)
    lo = (rem - mid.astype(F32)).astype(BF16)
    return hi, mid, lo


def _attn_kernel(slopes_ref, qt_ref, k_ref, vt_ref, lq1_ref, lk1_ref, lq2_ref, lk2_ref, g_ref,
                 o_ref, qs_ref, aug_ref, kc_ref, mask_ref, *stat_refs, tq, lam_init):
    h = pl.program_id(0)
    qi = pl.program_id(1)
    slope2 = slopes_ref[h]
    dk, hd = ATT_QK_DIM, ATT_V_DIM
    n_groups = (2 * tq) // Q_GROUP
    m_refs, acc_refs = stat_refs[:n_groups], stat_refs[n_groups:2 * n_groups]
    st_refs = stat_refs[2 * n_groups:3 * n_groups]
    mx_refs = stat_refs[3 * n_groups:]

    @pl.when(qi == 0)
    def _build_constants():
        c = lax.broadcasted_iota(jnp.int32, (tq, hd), 0)
        col = lax.broadcasted_iota(jnp.int32, (tq, hd), 1)
        c_lo = jnp.bitwise_and(c, MXU_DIM - 1)
        c_hi = c - c_lo
        for m, first in enumerate((dk, 0)):
            a = col - first
            kc = jnp.where((a >= 0) & (a < 3), c_hi,
                           jnp.where((a >= 3) & (a < 6), c_lo, jnp.where((a >= 6) & (a < 9), 1, 0)))
            kc_ref[m] = kc.astype(F32).astype(BF16)
        r = lax.broadcasted_iota(jnp.int32, (hd, 2 * tq), 1)
        first_row = jnp.where(r >= tq, 0, dk)
        r = jnp.where(r >= tq, r - tq, r)
        row = lax.broadcasted_iota(jnp.int32, (hd, 2 * tq), 0) - first_row
        sl = jnp.full((hd, 2 * tq), slope2, F32)
        s_hi, s_mid, s_lo = _split3(sl)
        t_hi, t_mid, t_lo = _split3(-(sl * r.astype(F32)))
        aug = jnp.zeros((hd, 2 * tq), F32)
        for i, piece in enumerate((s_hi, s_mid, s_lo, s_hi, s_mid, s_lo, t_hi, t_mid, t_lo)):
            aug = jnp.where(row == i, piece.astype(F32), aug)
        aug_ref[...] = aug.astype(BF16)
        ck = lax.broadcasted_iota(jnp.int32, (tq, 2 * tq), 0)
        rq = lax.broadcasted_iota(jnp.int32, (tq, 2 * tq), 1)
        rq = jnp.where(rq >= tq, rq - tq, rq)
        mask_ref[...] = jnp.where(ck <= rq, 0.0, -jnp.inf)

    qt = qt_ref[...]
    dim = lax.broadcasted_iota(jnp.int32, qt.shape, 0)
    qs_ref[:, 0:tq] = jnp.where(dim < dk, qt, aug_ref[:, 0:tq])
    qs_ref[:, tq:2 * tq] = jnp.where(dim >= dk, qt, aug_ref[:, tq:2 * tq])
    for m_ref, acc_ref in zip(m_refs, acc_refs):
        m_ref[...] = jnp.full(m_ref.shape, -jnp.inf, F32)
        acc_ref[...] = jnp.zeros(acc_ref.shape, F32)

    cols = [slice(g * Q_GROUP, (g + 1) * Q_GROUP) for g in range(n_groups)]

    def keys_of(j):
        start = pl.multiple_of(j * tq, tq)
        kb = k_ref[pl.ds(start, tq), :]
        lane = lax.broadcasted_iota(jnp.int32, kb.shape, 1)
        return jnp.where(lane < dk, kb, kc_ref[0]), jnp.where(lane >= dk, kb, kc_ref[1])

    def qk(ka, g):
        ka_map = ka[0] if g < n_groups // 2 else ka[1]
        st = jnp.dot(ka_map, qs_ref[:, cols[g]], preferred_element_type=F32)
        return st, jnp.max(st, axis=0, keepdims=True)

    def prefetch(ka, g):
        st_refs[g][...], mx_refs[g][...] = qk(ka, g)

    def softmax_pv(st, mx, j, vt, g, diagonal):
        if diagonal:
            st = st + mask_ref[:, cols[g]]
            mx = jnp.max(st, axis=0, keepdims=True)
        off = slope2 * ((j - qi) * tq).astype(F32)
        m_old = m_refs[g][...]
        m_new = jnp.maximum(m_old, mx + off)
        p = jnp.exp2(st - (m_new - off)).astype(BF16)
        alpha = jnp.exp2(m_old - m_new)
        acc_refs[g][...] = alpha * acc_refs[g][...] + jnp.dot(vt, p, preferred_element_type=F32)
        m_refs[g][...] = m_new

    def one_block(j_a, j_next, diagonal):
        vt_a = vt_ref[j_a]
        ka_n = keys_of(j_next) if j_next is not None else None
        for g in range(n_groups):
            softmax_pv(st_refs[g][...], mx_refs[g][...], j_a, vt_a, g, diagonal)
            if ka_n is not None:
                prefetch(ka_n, g)

    def two_blocks(j_a, j_b, j_next):
        vt_a, vt_b = vt_ref[j_a], vt_ref[j_b]
        ka_b, ka_n = keys_of(j_b), keys_of(j_next)
        s_b = {}
        for g in range(n_groups):
            softmax_pv(st_refs[g][...], mx_refs[g][...], j_a, vt_a, g, False)
            s_b[g] = qk(ka_b, g)
        for g in range(n_groups):
            softmax_pv(*s_b.pop(g), j_b, vt_b, g, False)
            prefetch(ka_n, g)

    ka_0 = keys_of(0)
    for g in range(n_groups):
        prefetch(ka_0, g)

    def body(pair, carry):
        two_blocks(2 * pair, 2 * pair + 1, 2 * pair + 2)
        return carry

    lax.fori_loop(0, lax.shift_right_logical(qi, 1), body, 0)

    @pl.when(jnp.bitwise_and(qi, 1) == 1)
    def _odd_block():
        one_block(qi - 1, qi, False)

    one_block(qi, None, True)

    lam = (jnp.exp(jnp.sum(lq1_ref[...] * lk1_ref[...], axis=-1, keepdims=True))
           - jnp.exp(jnp.sum(lq2_ref[...] * lk2_ref[...], axis=-1, keepdims=True)) + lam_init)
    ot = jnp.concatenate([a[0:hd, :] / a[hd:hd + 1, :] for a in acc_refs], axis=1)
    o = ot[:, 0:tq] - lam * ot[:, tq:2 * tq]
    o = o * lax.rsqrt(jnp.mean(o * o, axis=0, keepdims=True) + LN_EPS)
    o_ref[...] = (o * g_ref[...] * (1.0 - lam_init)).T.astype(o_ref.dtype)


def _attention(qt, k, vt, lq1, lk1, lq2, lk2, subln_g, lam_init):
    nh, nblk, hd, tq = qt.shape
    vrows = vt.shape[2]
    s = k.shape[0]
    assert (2 * tq) % Q_GROUP == 0 and tq % Q_GROUP == 0 and ATT_QK_DIM + 9 <= hd
    n_groups = (2 * tq) // Q_GROUP
    slopes = jnp.exp2(-8.0 * (jnp.arange(nh, dtype=F32) + 1.0) / nh) * LOG2E
    vec = lambda a: a.reshape(1, -1).astype(F32)
    small = lambda n: pl.BlockSpec((1, n), lambda h, i, *_: (0, 0))
    kern = functools.partial(_attn_kernel, tq=tq, lam_init=lam_init)
    return pl.pallas_call(
        kern,
        grid_spec=pltpu.PrefetchScalarGridSpec(
            num_scalar_prefetch=1,
            grid=(nh, nblk),
            in_specs=[pl.BlockSpec((None, None, hd, tq), lambda h, i, *_: (h, i, 0, 0)),
                      pl.BlockSpec((s, hd), lambda h, i, *_: (0, h)),
                      pl.BlockSpec((None, nblk, vrows, tq), lambda h, i, *_: (h, 0, 0, 0)),
                      small(ATT_QK_DIM), small(ATT_QK_DIM), small(ATT_QK_DIM), small(ATT_QK_DIM),
                      pl.BlockSpec((hd, 1), lambda h, i, *_: (0, 0))],
            out_specs=pl.BlockSpec((tq, hd), lambda h, i, *_: (i, h)),
            scratch_shapes=[pltpu.VMEM((hd, 2 * tq), BF16),
                            pltpu.VMEM((hd, 2 * tq), BF16),
                            pltpu.VMEM((2, tq, hd), BF16),
                            pltpu.VMEM((tq, 2 * tq), F32)]
            + [pltpu.VMEM((1, Q_GROUP), F32)] * n_groups
            + [pltpu.VMEM((vrows, Q_GROUP), F32)] * n_groups
            + [pltpu.VMEM((tq, Q_GROUP), F32)] * n_groups
            + [pltpu.VMEM((1, Q_GROUP), F32)] * n_groups),
        out_shape=jax.ShapeDtypeStruct((s, nh * hd), BF16),
        compiler_params=_params(2),
        name="diff_attention",
    )(slopes, qt, k, vt, vec(lq1), vec(lk1), vec(lq2), vec(lk2),
      subln_g.reshape(hd, 1).astype(F32))


def _seqmix_kernel(u_ref, lcw_ref, lcb_ref, wa_ref, ba_ref, wx_ref, bx_ref, lam_ref,
                   ccw_ref, ccb_ref, cg_ref, cb_ref, o_ref,
                   xbuf_ref, cbuf_ref, shift_ref, a_ref, b_ref, hs_ref, hc_ref, *, tt, w):
    i = pl.program_id(0)

    @pl.when(i == 0)
    def _init():
        xbuf_ref[0:HALO, :] = jnp.zeros((HALO, w), F32)
        cbuf_ref[0:HALO, :] = jnp.zeros((HALO, w), F32)
        hc_ref[...] = jnp.zeros(hc_ref.shape, F32)

    xbuf_ref[HALO:HALO + tt, :] = u_ref[:, 0:w]
    xc = jnp.zeros((tt, w), F32)
    for j in range(LRU_CONV):
        off = HALO - (LRU_CONV - 1) + j
        xc = xc + lcw_ref[j:j + 1, :] * xbuf_ref[off:off + tt, :]
    xc = xc + lcb_ref[...]
    xcb = xc.astype(BF16)
    gate_a = _sigmoid(jnp.dot(xcb, wa_ref[...], preferred_element_type=F32) + ba_ref[...])
    gate_x = _sigmoid(jnp.dot(xcb, wx_ref[...], preferred_element_type=F32) + bx_ref[...])
    nl = -lam_ref[...]
    softplus = jnp.maximum(nl, 0.0) + jnp.log(1.0 + jnp.exp(-jnp.abs(nl)))
    log_a = -LRU_C * gate_a * softplus
    a_ref[...] = jnp.exp(log_a)
    b_ref[...] = jnp.sqrt(1.0 - jnp.exp(2.0 * log_a)) * gate_x * xc

    def step(t, hprev):
        hnew = a_ref[pl.ds(t, 1), :] * hprev + b_ref[pl.ds(t, 1), :]
        hs_ref[pl.ds(t, 1), :] = hnew
        return hnew

    hc_ref[...] = lax.fori_loop(0, tt, step, hc_ref[...], unroll=8)
    o_ref[:, 0:w] = (hs_ref[...] * jax.nn.gelu(u_ref[:, w:2 * w], approximate=True)).astype(o_ref.dtype)
    xbuf_ref[0:HALO, :] = xbuf_ref[tt:tt + HALO, :]

    cbuf_ref[HALO:HALO + tt, :] = u_ref[:, 2 * w:3 * w] * _sigmoid(u_ref[:, 3 * w:4 * w])
    span = shift_ref.shape[1]
    for p in range(1, SUBLANES):
        shift_ref[p - 1] = cbuf_ref[p:p + span, :]
    y = jnp.zeros((tt, w), F32)
    for j in range(CONV_KERNEL):
        off = HALO - (CONV_KERNEL - 1) + j
        p, base = off % SUBLANES, off - off % SUBLANES
        win = cbuf_ref[base:base + tt, :] if p == 0 else shift_ref[p - 1, base:base + tt, :]
        y = y + ccw_ref[j:j + 1, :] * win
    y = _layer_norm(y + ccb_ref[...], cg_ref[...], cb_ref[...])
    o_ref[:, w:2 * w] = (y * _sigmoid(y)).astype(o_ref.dtype)
    cbuf_ref[0:HALO, :] = cbuf_ref[tt:tt + HALO, :]


def _block_diag(wb):
    nb, bd, _ = wb.shape
    eye = jnp.eye(nb, dtype=jnp.bool_)
    return jnp.where(eye[:, None, :, None], wb[:, :, None, :], 0.0).reshape(nb * bd, nb * bd)


def _seqmix(rest, lcw, lcb, wa, ba, wx, bx, lam, ccw, ccb, cg, cb, tt):
    s, n = rest.shape
    w = n // 4
    row = lambda a: a.reshape(1, w).astype(F32)
    full = lambda r, c: pl.BlockSpec((r, c), lambda i: (0, 0))
    kern = functools.partial(_seqmix_kernel, tt=tt, w=w)
    return pl.pallas_call(
        kern,
        grid=(s // tt,),
        in_specs=[pl.BlockSpec((tt, n), lambda i: (i, 0)),
                  full(LRU_CONV, w), full(1, w), full(w, w), full(1, w), full(w, w), full(1, w),
                  full(1, w), full(CONV_KERNEL, w), full(1, w), full(1, w), full(1, w)],
        out_specs=pl.BlockSpec((tt, 2 * w), lambda i: (i, 0)),
        out_shape=jax.ShapeDtypeStruct((s, 2 * w), BF16),
        scratch_shapes=[pltpu.VMEM((tt + HALO, w), F32), pltpu.VMEM((tt + HALO, w), F32),
                        pltpu.VMEM((SUBLANES - 1, tt + HALO - SUBLANES, w), F32),
                        pltpu.VMEM((tt, w), F32), pltpu.VMEM((tt, w), F32), pltpu.VMEM((tt, w), F32),
                        pltpu.VMEM((1, w), F32)],
        compiler_params=_params(),
        name="seqmix",
    )(rest, lcw, row(lcb), _block_diag(wa).astype(BF16), row(ba), _block_diag(wx).astype(BF16),
      row(bx), row(lam), ccw, row(ccb), row(cg), row(cb))


def _outproj_kernel(att_ref, rc_ref, h_ref, wo_ref, g_ref, b_ref, rwh_ref, rwl_ref, rb_ref,
                    h1_ref, h1c_ref, idx_ref, gate_ref, rank_ref, cnt_ref, tri_ref, carry_ref,
                    *, tm, n_att):
    i = pl.program_id(0)
    ne = N_EXPERTS

    @pl.when(i == 0)
    def _init():
        r = lax.broadcasted_iota(jnp.int32, (tm, tm), 0)
        c = lax.broadcasted_iota(jnp.int32, (tm, tm), 1)
        tri_ref[...] = jnp.where(c < r, 1.0, 0.0).astype(BF16)
        carry_ref[...] = jnp.zeros(carry_ref.shape, F32)

    mix = (jnp.dot(att_ref[...], wo_ref[0:n_att, :], preferred_element_type=F32)
           + jnp.dot(rc_ref[...], wo_ref[n_att:, :], preferred_element_type=F32))
    h1 = _layer_norm(RESID_ALPHA * h_ref[...] + mix, g_ref[...], b_ref[...])
    h1_ref[...] = h1
    _store_rows(h1c_ref, h1)

    h1_hi = h1.astype(BF16)
    h1_lo = (h1 - h1_hi.astype(F32)).astype(BF16)
    logits = (jnp.dot(h1_hi, rwh_ref[...], preferred_element_type=F32)
              + jnp.dot(h1_hi, rwl_ref[...], preferred_element_type=F32)
              + jnp.dot(h1_lo, rwh_ref[...], preferred_element_type=F32)) + rb_ref[...]
    lane = lax.broadcasted_iota(jnp.int32, (tm, ne), 1).astype(F32)
    onehot = jnp.zeros((tm, ne), F32)
    vals, sels = [], []
    for _ in range(TOP_K):
        mx = jnp.max(logits, axis=-1, keepdims=True)
        sel = jnp.min(jnp.where(logits == mx, lane, float(ne)), axis=-1, keepdims=True)
        hit = lane == sel
        onehot = onehot + jnp.where(hit, 1.0, 0.0)
        logits = jnp.where(hit, -jnp.inf, logits)
        vals.append(mx)
        sels.append(sel)
    ex = [jnp.exp(v - vals[0]) for v in vals]
    den = ex[0] + ex[1] + ex[2] + ex[3]
    before = jnp.dot(tri_ref[...], onehot.astype(BF16), preferred_element_type=F32) + carry_ref[...]
    for k in range(TOP_K):
        idx_ref[:, k:k + 1] = sels[k].astype(jnp.int32)
        gate_ref[:, k:k + 1] = ex[k] / den
        rank_ref[:, k:k + 1] = jnp.sum(jnp.where(lane == sels[k], before, 0.0), axis=-1,
                                       keepdims=True).astype(jnp.int32)
    carry_ref[...] = carry_ref[...] + jnp.sum(onehot, axis=0, keepdims=True)
    cnt_ref[...] = carry_ref[...].astype(jnp.int32)


def _outproj_router(att, rc, h, wo_bf16, g, b, rw, rb, tm):
    s, d = h.shape
    n_att = att.shape[1]
    ne = N_EXPERTS
    row = lambda a: a.reshape(1, -1).astype(F32)
    full = lambda r, c: pl.BlockSpec((r, c), lambda i: (0, 0))
    tile = lambda c: pl.BlockSpec((tm, c), lambda i: (i, 0))
    kern = functools.partial(_outproj_kernel, tm=tm, n_att=n_att)
    rw_hi = rw.astype(BF16)
    rw_lo = (rw - rw_hi.astype(F32)).astype(BF16)
    return pl.pallas_call(
        kern,
        grid=(s // tm,),
        in_specs=[tile(n_att), tile(rc.shape[1]), tile(d), full(wo_bf16.shape[0], d),
                  full(1, d), full(1, d), full(d, ne), full(d, ne), full(1, ne)],
        out_specs=[tile(d), pl.BlockSpec((tm * (d // LANES), LANES), lambda i: (i, 0)),
                   tile(TOP_K), tile(TOP_K), tile(TOP_K), full(1, ne)],
        out_shape=[jax.ShapeDtypeStruct((s, d), F32),
                   jax.ShapeDtypeStruct((s * (d // LANES), LANES), F32),
                   jax.ShapeDtypeStruct((s, TOP_K), jnp.int32),
                   jax.ShapeDtypeStruct((s, TOP_K), F32),
                   jax.ShapeDtypeStruct((s, TOP_K), jnp.int32),
                   jax.ShapeDtypeStruct((1, ne), jnp.int32)],
        scratch_shapes=[pltpu.VMEM((tm, tm), BF16), pltpu.VMEM((1, ne), F32)],
        compiler_params=_params(),
        name="outproj_router",
    )(att, rc, h, wo_bf16, row(g), row(b), rw_hi, rw_lo, row(rb))


def _row_copy(src_ref, src_row, dst_ref, dst_row, sem, nc):
    src = src_ref.at[pl.ds(pl.multiple_of(src_row * nc, nc), nc)]
    dst = dst_ref.at[pl.ds(pl.multiple_of(dst_row * nc, nc), nc)]
    return pltpu.make_async_copy(src, dst, sem)


def _dest_kernel(idx_ref, rank_ref, pstart_ref, dest_ref):
    tm, ne = idx_ref.shape[0], pstart_ref.shape[1]
    lane = lax.broadcasted_iota(jnp.int32, (tm, ne), 1)
    pstart = pstart_ref[...].astype(F32)
    for k in range(TOP_K):
        hit = lane == idx_ref[:, k:k + 1]
        start = jnp.sum(jnp.where(hit, pstart, 0.0), axis=-1, keepdims=True)
        dest_ref[:, k:k + 1] = start.astype(jnp.int32) + rank_ref[:, k:k + 1]


def _dest_rows(idx, rank, pstart, tm):
    s = idx.shape[0]
    ne = pstart.shape[0]
    tile = pl.BlockSpec((tm, TOP_K), lambda i: (i, 0))
    return pl.pallas_call(
        _dest_kernel,
        grid=(s // tm,),
        in_specs=[tile, tile, pl.BlockSpec((1, ne), lambda i: (0, 0))],
        out_specs=tile,
        out_shape=jax.ShapeDtypeStruct((s, TOP_K), jnp.int32),
        compiler_params=_params(),
        name="moe_dest_rows",
    )(idx, rank, pstart.reshape(1, ne))


def _dispatch_kernel(pstart_ref, pend_ref, dest_ref, h_ref, xd_ref, zero_ref, sem, zsem,
                     *, tt, tm, nc, n_blocks):
    i = pl.program_id(0)

    def zero_copy(e):
        start = pl.multiple_of((pend_ref[e] - tm) * nc, tm * nc)
        return pltpu.make_async_copy(zero_ref, xd_ref.at[pl.ds(start, tm * nc)], zsem)

    def tail_copy(b):
        start = pl.multiple_of(b * (tm * nc), tm * nc)
        return pltpu.make_async_copy(zero_ref, xd_ref.at[pl.ds(start, tm * nc)], zsem)

    @pl.when(i == 0)
    def _clear_padding():
        zero_ref[...] = jnp.zeros(zero_ref.shape, zero_ref.dtype)
        n_used = pend_ref[N_EXPERTS - 1] // tm

        def start_tail(b, carry):
            tail_copy(b).start()
            return carry

        def wait_tail(b, carry):
            tail_copy(b).wait()
            return carry

        lax.fori_loop(n_used, n_blocks, start_tail, 0)
        lax.fori_loop(n_used, n_blocks, wait_tail, 0)
        for e in range(N_EXPERTS):
            @pl.when(pend_ref[e] > pstart_ref[e])
            def _():
                zero_copy(e).start()
        for e in range(N_EXPERTS):
            @pl.when(pend_ref[e] > pstart_ref[e])
            def _():
                zero_copy(e).wait()

    def issue(t, carry):
        for k in range(TOP_K):
            _row_copy(h_ref, t, xd_ref, dest_ref[t * TOP_K + k], sem, nc).start(priority=k % 2)
        return carry

    lax.fori_loop(0, tt, issue, 0)

    for k in range(TOP_K):
        pltpu.make_async_copy(h_ref, xd_ref.at[pl.ds(0, tt * nc)], sem).wait()


def _dispatch(h1c, dest_flat, pstart, pend, rows, s, tt, tm):
    nc = h1c.shape[0] // s
    smem = lambda n: pl.BlockSpec((n,), lambda i, *_: (i,), memory_space=pltpu.SMEM)
    kern = functools.partial(_dispatch_kernel, tt=tt, tm=tm, nc=nc, n_blocks=rows // tm)
    return pl.pallas_call(
        kern,
        grid_spec=pltpu.PrefetchScalarGridSpec(
            num_scalar_prefetch=2,
            grid=(s // tt,),
            in_specs=[smem(tt * TOP_K),
                      pl.BlockSpec((tt * nc, LANES), lambda i, *_: (i, 0))],
            out_specs=pl.BlockSpec(memory_space=pl.ANY),
            scratch_shapes=[pltpu.VMEM((tm * nc, LANES), F32), pltpu.SemaphoreType.DMA(()),
                            pltpu.SemaphoreType.DMA(())]),
        out_shape=jax.ShapeDtypeStruct((rows * nc, LANES), F32),
        compiler_params=_params(),
        name="moe_dispatch",
    )(pstart, pend, dest_flat, h1c)


def _expert_kernel(bexp_ref, nused_ref, x_ref, w1_ref, b1_ref, w2_ref, b2_ref, y_ref,
                   w1b_ref, w2b_ref, *, f, chunk, tm):
    b = pl.program_id(0)
    last = nused_ref[0] - 1
    e = bexp_ref[jnp.minimum(b, last)]
    e_prev = bexp_ref[jnp.maximum(jnp.minimum(b, last) - 1, 0)]

    @pl.when((b == 0) | (e != e_prev))
    def _cast_weights():
        def cast(c, carry):
            r = pl.multiple_of(c * chunk, chunk)
            w1b_ref[pl.ds(r, chunk), :] = w1_ref[pl.ds(r, chunk), :].astype(BF16)
            w2b_ref[pl.ds(r, chunk), :] = w2_ref[pl.ds(r, chunk), :].astype(BF16)
            return carry
        lax.fori_loop(0, w1_ref.shape[0] // chunk, cast, 0)

    @pl.when(b > last)
    def _unused_block():
        y_ref[...] = jnp.zeros(y_ref.shape, y_ref.dtype)

    @pl.when(b <= last)
    def _mlp():
        x = _load_rows(x_ref, tm, w1_ref.shape[0]).astype(BF16)
        hdn = jnp.dot(x, w1b_ref[...], preferred_element_type=F32) + b1_ref[...]
        glu = jnp.minimum(hdn[:, 0:f], SWIGLU_LIMIT)
        lin = jnp.clip(hdn[:, f:2 * f], -SWIGLU_LIMIT, SWIGLU_LIMIT)
        act = glu * _sigmoid(SWIGLU_ALPHA * glu) * (lin + 1.0)
        _store_rows(y_ref, jnp.dot(act.astype(BF16), w2b_ref[...], preferred_element_type=F32)
                    + b2_ref[...])


def _experts(x_disp, bexp, nused, w1, b1, w2, b2, layer, tm):
    d = w1.shape[2]
    f = w2.shape[2]
    nc = d // LANES
    rows = x_disp.shape[0] // nc
    assert f == d, "the weight-cast loop assumes d_expert == d_model"

    def blk(b, bexp_ref, nused_ref):
        return jnp.minimum(b, nused_ref[0] - 1)

    def wmap(b, bexp_ref, nused_ref):
        return (layer, bexp_ref[blk(b, bexp_ref, nused_ref)], 0, 0)

    def bmap(b, bexp_ref, nused_ref):
        return (layer, bexp_ref[blk(b, bexp_ref, nused_ref)], 0, 0)

    kern = functools.partial(_expert_kernel, f=f, chunk=128, tm=tm)
    return pl.pallas_call(
        kern,
        grid_spec=pltpu.PrefetchScalarGridSpec(
            num_scalar_prefetch=2,
            grid=(rows // tm,),
            in_specs=[pl.BlockSpec((tm * nc, LANES), lambda b, be, nu: (b, 0)),
                      pl.BlockSpec((None, None, d, 2 * f), wmap),
                      pl.BlockSpec((None, None, 1, 2 * f), bmap),
                      pl.BlockSpec((None, None, f, d), wmap),
                      pl.BlockSpec((None, None, 1, d), bmap)],
            out_specs=pl.BlockSpec((tm * nc, LANES), lambda b, be, nu: (b, 0)),
            scratch_shapes=[pltpu.VMEM((d, 2 * f), BF16), pltpu.VMEM((f, d), BF16)]),
        out_shape=jax.ShapeDtypeStruct((rows * nc, LANES), F32),
        compiler_params=_params(),
        name="moe_experts",
    )(bexp, nused, x_disp, w1, b1.reshape(b1.shape[0], b1.shape[1], 1, -1), w2,
      b2.reshape(b2.shape[0], b2.shape[1], 1, -1))


def _combine_kernel(dest_ref, h_ref, gate_ref, g_ref, b_ref, yd_ref, o_ref,
                    ybuf_ref, sems, *, tt, nc):
    th = tt // COMBINE_PARTS
    d = h_ref.shape[1]

    def issue(part):
        def one(t, carry):
            for k in range(TOP_K):
                _row_copy(yd_ref, dest_ref[t * TOP_K + k], ybuf_ref.at[k], t, sems.at[part],
                          nc).start(priority=k % 2)
            return carry
        lax.fori_loop(part * th, (part + 1) * th, one, 0)

    def finish(part):
        for k in range(TOP_K):
            pltpu.make_async_copy(yd_ref.at[pl.ds(0, th * nc)],
                                  ybuf_ref.at[k, pl.ds(part * th * nc, th * nc)],
                                  sems.at[part]).wait()
        rows = slice(part * th, (part + 1) * th)
        ff = gate_ref[rows, 0:1] * _load_rows(ybuf_ref, th, d, (0,), part * th)
        for k in range(1, TOP_K):
            ff = ff + gate_ref[rows, k:k + 1] * _load_rows(ybuf_ref, th, d, (k,), part * th)
        o_ref[rows, :] = _layer_norm(RESID_ALPHA * h_ref[rows, :] + ff, g_ref[...], b_ref[...])

    issue(0)
    for part in range(COMBINE_PARTS):
        if part + 1 < COMBINE_PARTS:
            issue(part + 1)
        finish(part)


def _combine(y_disp, h1, gates, dest_flat, g, b, tt):
    s, d = h1.shape
    nc = d // LANES
    row = lambda a: a.reshape(1, -1).astype(F32)
    kern = functools.partial(_combine_kernel, tt=tt, nc=nc)
    return pl.pallas_call(
        kern,
        grid=(s // tt,),
        in_specs=[pl.BlockSpec((tt * TOP_K,), lambda i: (i,), memory_space=pltpu.SMEM),
                  pl.BlockSpec((tt, d), lambda i: (i, 0)),
                  pl.BlockSpec((tt, TOP_K), lambda i: (i, 0)),
                  pl.BlockSpec((1, d), lambda i: (0, 0)),
                  pl.BlockSpec((1, d), lambda i: (0, 0)),
                  pl.BlockSpec(memory_space=pl.ANY)],
        out_specs=pl.BlockSpec((tt, d), lambda i: (i, 0)),
        scratch_shapes=[pltpu.VMEM((TOP_K, tt * nc, LANES), F32),
                        pltpu.SemaphoreType.DMA((COMBINE_PARTS,))],
        out_shape=jax.ShapeDtypeStruct((s, d), F32),
        compiler_params=_params(),
        name="moe_combine",
    )(dest_flat, h1, gates, row(g), row(b), y_disp)


def _tiles(s):
    t = lambda want: math.gcd(s, want)
    return dict(attn=t(512), seq=t(512), route=t(512), dest=t(2048), dispatch=t(1024),
                expert=t(512), combine=t(512))


def kernel(x, ln_in_g, ln_in_b, w_in, lam_q1, lam_k1, lam_q2, lam_k2, subln_g, lru_conv_w, lru_conv_b, lru_wa, lru_ba, lru_wx, lru_bx, lru_lambda, cf_conv_w, cf_conv_b, cf_ln_g, cf_ln_b, w_out, ln1_g, ln1_b, router_w, router_b, moe_w1, moe_b1, moe_w2, moe_b2, ln2_g, ln2_b):
    bsz, s, d = x.shape
    assert bsz == 1
    tl = _tiles(s)
    tm = tl["expert"]
    n_blocks = (s * TOP_K) // tm + N_EXPERTS
    rows = n_blocks * tm

    h = x.reshape(s, d)
    for l in range(DEPTH):
        lam_init = 0.8 - 0.6 * math.exp(-0.3 * l)
        proj = _inproj(h, ln_in_g, ln_in_b, w_in[l].astype(BF16), tl["attn"], input_ln=(l == 0))
        qt, k, vt, rest = proj[:4]
        if l == 0:
            h = proj[4]
        att = _attention(qt, k, vt, lam_q1[l], lam_k1[l], lam_q2[l], lam_k2[l], subln_g[l],
                         lam_init)
        rc = _seqmix(rest, lru_conv_w[l], lru_conv_b[l], lru_wa[l], lru_ba[l], lru_wx[l],
                     lru_bx[l], lru_lambda[l], cf_conv_w[l], cf_conv_b[l], cf_ln_g[l], cf_ln_b[l],
                     tl["seq"])
        h1, h1c, idx, gates, rank, counts = _outproj_router(
            att, rc, h, w_out[l].astype(BF16), ln1_g[l], ln1_b[l], router_w[l], router_b[l],
            tl["route"])
        counts = counts.reshape(N_EXPERTS)
        padded = (counts + tm - 1) // tm * tm
        e_ids = jnp.arange(N_EXPERTS, dtype=jnp.int32)
        pend = jnp.sum(jnp.where(e_ids[None, :] <= e_ids[:, None], padded[None, :], 0),
                       axis=1).astype(jnp.int32)
        pstart = pend - padded
        block_row = jnp.arange(n_blocks, dtype=jnp.int32) * tm
        bexp = jnp.minimum(jnp.sum((pend[None, :] <= block_row[:, None]).astype(jnp.int32), axis=1),
                           N_EXPERTS - 1).astype(jnp.int32)
        nused = (pend[-1:] // tm).astype(jnp.int32)
        dest_flat = _dest_rows(idx, rank, pstart, tl["dest"]).reshape(s * TOP_K)
        x_disp = _dispatch(h1c, dest_flat, pstart, pend, rows, s, tl["dispatch"], tm)
        y_disp = _experts(x_disp, bexp, nused, moe_w1, moe_b1, moe_w2, moe_b2, l, tm)
        h = _combine(y_disp, h1, gates, dest_flat, ln2_g[l], ln2_b[l], tl["combine"])
    return h.reshape(bsz, s, d)
```

```python
import functools
import math

import jax
import jax.numpy as jnp
from jax import lax
from jax.experimental import pallas as pl
from jax.experimental.pallas import tpu as pltpu

F32 = jnp.float32
BF16 = jnp.bfloat16

DEPTH = 2
ATT_HEADS = 4
ATT_QK_DIM = 64
ATT_V_DIM = 128
LRU_BLOCKS = 4
LRU_CONV = 4
LRU_C = 8.0
CONV_KERNEL = 31
N_EXPERTS = 32
TOP_K = 4
SWIGLU_LIMIT = 7.0
SWIGLU_ALPHA = 1.702
LN_EPS = 1e-5
RESID_ALPHA = (2.0 * DEPTH) ** 0.25

VMEM_LIMIT_BYTES = 56 * 1024 * 1024
HALO = 32
LANES = 128
SUBLANES = 8
MXU_DIM = 256
LOG2E = math.log2(math.e)
V_PAD_ROWS = 16
Q_GROUP = 256
COMBINE_PARTS = 4

def _load_rows(ref, n_rows, d, lead=(), first_row=0):
    nc = d // LANES
    return jnp.concatenate([ref[lead + (pl.ds(first_row * nc + c, n_rows, stride=nc), slice(None))]
                            for c in range(nc)], axis=-1)


def _store_rows(ref, val):
    n_rows, d = val.shape
    nc = d // LANES
    for c in range(nc):
        ref[pl.ds(c, n_rows, stride=nc), :] = val[:, c * LANES:(c + 1) * LANES]


def _params(n_axes=1):
    return pltpu.CompilerParams(dimension_semantics=("arbitrary",) * n_axes,
                                vmem_limit_bytes=VMEM_LIMIT_BYTES)


def _layer_norm(x, g, b):
    mu = jnp.mean(x, axis=-1, keepdims=True)
    xc = x - mu
    var = jnp.mean(xc * xc, axis=-1, keepdims=True)
    return xc * lax.rsqrt(var + LN_EPS) * g + b


def _sigmoid(x):
    return 1.0 / (1.0 + jnp.exp(-x))


def _inproj_kernel(h_ref, g_ref, b_ref, w_ref, qt_ref, k_ref, vt_ref, rest_ref, *maybe_h_out,
                   scale):
    nh, hd = ATT_HEADS, ATT_V_DIM
    w_att = nh * hd
    tm = h_ref.shape[0]
    h = h_ref[...]
    if maybe_h_out:
        h = _layer_norm(h, g_ref[...], b_ref[...])
        maybe_h_out[0][...] = h
    hb = h.astype(BF16)
    q = jnp.dot(hb, w_ref[:, 0:w_att], preferred_element_type=F32) * scale
    k_ref[...] = jnp.dot(hb, w_ref[:, w_att:2 * w_att], preferred_element_type=F32).astype(BF16)
    v = jnp.dot(hb, w_ref[:, 2 * w_att:3 * w_att], preferred_element_type=F32)
    ones = jnp.ones((V_PAD_ROWS, tm), BF16)
    for h in range(nh):
        qt_ref[h] = q[:, h * hd:(h + 1) * hd].T.astype(BF16)
        vt_ref[h, 0:hd, :] = v[:, h * hd:(h + 1) * hd].T.astype(BF16)
        vt_ref[h, hd:hd + V_PAD_ROWS, :] = ones
    rest_ref[...] = jnp.dot(hb, w_ref[:, 3 * w_att:], preferred_element_type=F32)


def _inproj(h, ln_g, ln_b, w_bf16, tm, input_ln):
    s, d = h.shape
    n = w_bf16.shape[1]
    nh, hd = ATT_HEADS, ATT_V_DIM
    n_att = 3 * nh * hd
    kern = functools.partial(_inproj_kernel, scale=ATT_QK_DIM ** -0.5 * LOG2E)
    tposed = lambda r: pl.BlockSpec((nh, None, r, tm), lambda i: (0, i, 0, 0))
    rows = pl.BlockSpec((tm, d), lambda i: (i, 0))
    vec = pl.BlockSpec((1, d), lambda i: (0, 0))
    out_specs = [tposed(hd),
                 pl.BlockSpec((tm, nh * hd), lambda i: (i, 0)),
                 tposed(hd + V_PAD_ROWS),
                 pl.BlockSpec((tm, n - n_att), lambda i: (i, 0))]
    out_shape = [jax.ShapeDtypeStruct((nh, s // tm, hd, tm), BF16),
                 jax.ShapeDtypeStruct((s, nh * hd), BF16),
                 jax.ShapeDtypeStruct((nh, s // tm, hd + V_PAD_ROWS, tm), BF16),
                 jax.ShapeDtypeStruct((s, n - n_att), F32)]
    if input_ln:
        out_specs.append(rows)
        out_shape.append(jax.ShapeDtypeStruct((s, d), F32))
    return pl.pallas_call(
        kern,
        grid=(s // tm,),
        in_specs=[rows, vec, vec, pl.BlockSpec((d, n), lambda i: (0, 0))],
        out_specs=out_specs,
        out_shape=out_shape,
        compiler_params=_params(),
        name="inproj",
    )(h, ln_g.reshape(1, d).astype(F32), ln_b.reshape(1, d).astype(F32), w_bf16)


def _split3(x):
    hi = x.astype(BF16)
    rem = x - hi.astype(F32)
    mid = rem.astype(BF16)
    lo = (rem - mid.astype(F32)).astype(BF16)
    return hi, mid, lo


def _attn_kernel(slopes_ref, qt_ref, k_ref, vt_ref, lq1_ref, lk1_ref, lq2_ref, lk2_ref, g_ref,
                 o_ref, qs_ref, aug_ref, kc_ref, ka_ref, mask_ref, *stat_refs, tq, lam_init):
    h = pl.program_id(0)
    qi = pl.program_id(1)
    slope2 = slopes_ref[h]
    dk, hd = ATT_QK_DIM, ATT_V_DIM
    n_groups = (2 * tq) // Q_GROUP
    m_refs, acc_refs = stat_refs[:n_groups], stat_refs[n_groups:2 * n_groups]
    st_refs = stat_refs[2 * n_groups:3 * n_groups]
    mx_refs = stat_refs[3 * n_groups:]

    @pl.when(qi == 0)
    def _build_constants():
        c = lax.broadcasted_iota(jnp.int32, (tq, hd), 0)
        col = lax.broadcasted_iota(jnp.int32, (tq, hd), 1)
        c_lo = jnp.bitwise_and(c, MXU_DIM - 1)
        c_hi = c - c_lo
        for m, first in enumerate((dk, 0)):
            a = col - first
            kc = jnp.where((a >= 0) & (a < 3), c_hi,
                           jnp.where((a >= 3) & (a < 6), c_lo, jnp.where((a >= 6) & (a < 9), 1, 0)))
            kc_ref[m] = kc.astype(F32).astype(BF16)
        r = lax.broadcasted_iota(jnp.int32, (hd, 2 * tq), 1)
        first_row = jnp.where(r >= tq, 0, dk)
        r = jnp.where(r >= tq, r - tq, r)
        row = lax.broadcasted_iota(jnp.int32, (hd, 2 * tq), 0) - first_row
        sl = jnp.full((hd, 2 * tq), slope2, F32)
        s_hi, s_mid, s_lo = _split3(sl)
        t_hi, t_mid, t_lo = _split3(-(sl * r.astype(F32)))
        aug = jnp.zeros((hd, 2 * tq), F32)
        for i, piece in enumerate((s_hi, s_mid, s_lo, s_hi, s_mid, s_lo, t_hi, t_mid, t_lo)):
            aug = jnp.where(row == i, piece.astype(F32), aug)
        aug_ref[...] = aug.astype(BF16)
        ck = lax.broadcasted_iota(jnp.int32, (tq, 2 * tq), 0)
        rq = lax.broadcasted_iota(jnp.int32, (tq, 2 * tq), 1)
        rq = jnp.where(rq >= tq, rq - tq, rq)
        mask_ref[...] = jnp.where(ck <= rq, 0.0, -jnp.inf)

    qt = qt_ref[...]
    dim = lax.broadcasted_iota(jnp.int32, qt.shape, 0)
    qs_ref[:, 0:tq] = jnp.where(dim < dk, qt, aug_ref[:, 0:tq])
    qs_ref[:, tq:2 * tq] = jnp.where(dim >= dk, qt, aug_ref[:, tq:2 * tq])
    for m_ref, acc_ref in zip(m_refs, acc_refs):
        m_ref[...] = jnp.full(m_ref.shape, -jnp.inf, F32)
        acc_ref[...] = jnp.zeros(acc_ref.shape, F32)

    cols = [slice(g * Q_GROUP, (g + 1) * Q_GROUP) for g in range(n_groups)]

    @pl.when(qi == 0)
    def _build_key_operands():
        def build(j, carry):
            start = pl.multiple_of(j * tq, tq)
            kb = k_ref[pl.ds(start, tq), :]
            lane = lax.broadcasted_iota(jnp.int32, kb.shape, 1)
            ka_ref[0, pl.ds(start, tq), :] = jnp.where(lane < dk, kb, kc_ref[0])
            ka_ref[1, pl.ds(start, tq), :] = jnp.where(lane >= dk, kb, kc_ref[1])
            return carry
        lax.fori_loop(0, k_ref.shape[0] // tq, build, 0)

    def keys_of(j):
        start = pl.multiple_of(j * tq, tq)
        return ka_ref[0, pl.ds(start, tq), :], ka_ref[1, pl.ds(start, tq), :]

    def qk(ka, g):
        ka_map = ka[0] if g < n_groups // 2 else ka[1]
        st = jnp.dot(ka_map, qs_ref[:, cols[g]], preferred_element_type=F32)
        return st, jnp.max(st, axis=0, keepdims=True)

    def prefetch(ka, g):
        st_refs[g][...], mx_refs[g][...] = qk(ka, g)

    def softmax_pv(st, mx, j, vt, g, diagonal):
        if diagonal:
            st = st + mask_ref[:, cols[g]]
            mx = jnp.max(st, axis=0, keepdims=True)
        off = slope2 * ((j - qi) * tq).astype(F32)
        m_old = m_refs[g][...]
        m_new = jnp.maximum(m_old, mx + off)
        p = jnp.exp2(st - (m_new - off)).astype(BF16)
        alpha = jnp.exp2(m_old - m_new)
        acc_refs[g][...] = alpha * acc_refs[g][...] + jnp.dot(vt, p, preferred_element_type=F32)
        m_refs[g][...] = m_new

    def one_block(j_a, j_next, diagonal):
        vt_a = vt_ref[j_a]
        ka_n = keys_of(j_next) if j_next is not None else None
        for g in range(n_groups):
            softmax_pv(st_refs[g][...], mx_refs[g][...], j_a, vt_a, g, diagonal)
            if ka_n is not None:
                prefetch(ka_n, g)

    def two_blocks(j_a, j_b, j_next):
        vt_a, vt_b = vt_ref[j_a], vt_ref[j_b]
        ka_b, ka_n = keys_of(j_b), keys_of(j_next)
        s_b = {}
        for g in range(n_groups):
            softmax_pv(st_refs[g][...], mx_refs[g][...], j_a, vt_a, g, False)
            s_b[g] = qk(ka_b, g)
        for g in range(n_groups):
            softmax_pv(*s_b.pop(g), j_b, vt_b, g, False)
            prefetch(ka_n, g)

    ka_0 = keys_of(0)
    for g in range(n_groups):
        prefetch(ka_0, g)

    def body(pair, carry):
        two_blocks(2 * pair, 2 * pair + 1, 2 * pair + 2)
        return carry

    lax.fori_loop(0, lax.shift_right_logical(qi, 1), body, 0)

    @pl.when(jnp.bitwise_and(qi, 1) == 1)
    def _odd_block():
        one_block(qi - 1, qi, False)

    one_block(qi, None, True)

    lam = (jnp.exp(jnp.sum(lq1_ref[...] * lk1_ref[...], axis=-1, keepdims=True))
           - jnp.exp(jnp.sum(lq2_ref[...] * lk2_ref[...], axis=-1, keepdims=True)) + lam_init)
    ot = jnp.concatenate([a[0:hd, :] / a[hd:hd + 1, :] for a in acc_refs], axis=1)
    o = ot[:, 0:tq] - lam * ot[:, tq:2 * tq]
    o = o * lax.rsqrt(jnp.mean(o * o, axis=0, keepdims=True) + LN_EPS)
    o_ref[...] = (o * g_ref[...] * (1.0 - lam_init)).T.astype(o_ref.dtype)


def _attention(qt, k, vt, lq1, lk1, lq2, lk2, subln_g, lam_init):
    nh, nblk, hd, tq = qt.shape
    vrows = vt.shape[2]
    s = k.shape[0]
    assert (2 * tq) % Q_GROUP == 0 and tq % Q_GROUP == 0 and ATT_QK_DIM + 9 <= hd
    n_groups = (2 * tq) // Q_GROUP
    slopes = jnp.exp2(-8.0 * (jnp.arange(nh, dtype=F32) + 1.0) / nh) * LOG2E
    vec = lambda a: a.reshape(1, -1).astype(F32)
    small = lambda n: pl.BlockSpec((1, n), lambda h, i, *_: (0, 0))
    kern = functools.partial(_attn_kernel, tq=tq, lam_init=lam_init)
    return pl.pallas_call(
        kern,
        grid_spec=pltpu.PrefetchScalarGridSpec(
            num_scalar_prefetch=1,
            grid=(nh, nblk),
            in_specs=[pl.BlockSpec((None, None, hd, tq), lambda h, i, *_: (h, i, 0, 0)),
                      pl.BlockSpec((s, hd), lambda h, i, *_: (0, h)),
                      pl.BlockSpec((None, nblk, vrows, tq), lambda h, i, *_: (h, 0, 0, 0)),
                      small(ATT_QK_DIM), small(ATT_QK_DIM), small(ATT_QK_DIM), small(ATT_QK_DIM),
                      pl.BlockSpec((hd, 1), lambda h, i, *_: (0, 0))],
            out_specs=pl.BlockSpec((tq, hd), lambda h, i, *_: (i, h)),
            scratch_shapes=[pltpu.VMEM((hd, 2 * tq), BF16),
                            pltpu.VMEM((hd, 2 * tq), BF16),
                            pltpu.VMEM((2, tq, hd), BF16),
                            pltpu.VMEM((2, s, hd), BF16),
                            pltpu.VMEM((tq, 2 * tq), F32)]
            + [pltpu.VMEM((1, Q_GROUP), F32)] * n_groups
            + [pltpu.VMEM((vrows, Q_GROUP), F32)] * n_groups
            + [pltpu.VMEM((tq, Q_GROUP), F32)] * n_groups
            + [pltpu.VMEM((1, Q_GROUP), F32)] * n_groups),
        out_shape=jax.ShapeDtypeStruct((s, nh * hd), BF16),
        compiler_params=_params(2),
        name="diff_attention",
    )(slopes, qt, k, vt, vec(lq1), vec(lk1), vec(lq2), vec(lk2),
      subln_g.reshape(hd, 1).astype(F32))


def _seqmix_kernel(u_ref, lcw_ref, lcb_ref, wa_ref, ba_ref, wx_ref, bx_ref, lam_ref,
                   ccw_ref, ccb_ref, cg_ref, cb_ref, o_ref,
                   xbuf_ref, cbuf_ref, shift_ref, a_ref, b_ref, hs_ref, hc_ref, *, tt, w):
    i = pl.program_id(0)

    @pl.when(i == 0)
    def _init():
        xbuf_ref[0:HALO, :] = jnp.zeros((HALO, w), F32)
        cbuf_ref[0:HALO, :] = jnp.zeros((HALO, w), F32)
        hc_ref[...] = jnp.zeros(hc_ref.shape, F32)

    xbuf_ref[HALO:HALO + tt, :] = u_ref[:, 0:w]
    xc = jnp.zeros((tt, w), F32)
    for j in range(LRU_CONV):
        off = HALO - (LRU_CONV - 1) + j
        xc = xc + lcw_ref[j:j + 1, :] * xbuf_ref[off:off + tt, :]
    xc = xc + lcb_ref[...]
    xcb = xc.astype(BF16)
    gate_a = _sigmoid(jnp.dot(xcb, wa_ref[...], preferred_element_type=F32) + ba_ref[...])
    gate_x = _sigmoid(jnp.dot(xcb, wx_ref[...], preferred_element_type=F32) + bx_ref[...])
    nl = -lam_ref[...]
    softplus = jnp.maximum(nl, 0.0) + jnp.log(1.0 + jnp.exp(-jnp.abs(nl)))
    log_a = -LRU_C * gate_a * softplus
    a_ref[...] = jnp.exp(log_a)
    b_ref[...] = jnp.sqrt(1.0 - jnp.exp(2.0 * log_a)) * gate_x * xc

    def step(t, hprev):
        hnew = a_ref[pl.ds(t, 1), :] * hprev + b_ref[pl.ds(t, 1), :]
        hs_ref[pl.ds(t, 1), :] = hnew
        return hnew

    hc_ref[...] = lax.fori_loop(0, tt, step, hc_ref[...], unroll=8)
    o_ref[:, 0:w] = (hs_ref[...] * jax.nn.gelu(u_ref[:, w:2 * w], approximate=True)).astype(o_ref.dtype)
    xbuf_ref[0:HALO, :] = xbuf_ref[tt:tt + HALO, :]

    cbuf_ref[HALO:HALO + tt, :] = u_ref[:, 2 * w:3 * w] * _sigmoid(u_ref[:, 3 * w:4 * w])
    span = shift_ref.shape[1]
    for p in range(1, SUBLANES):
        shift_ref[p - 1] = cbuf_ref[p:p + span, :]
    y = jnp.zeros((tt, w), F32)
    for j in range(CONV_KERNEL):
        off = HALO - (CONV_KERNEL - 1) + j
        p, base = off % SUBLANES, off - off % SUBLANES
        win = cbuf_ref[base:base + tt, :] if p == 0 else shift_ref[p - 1, base:base + tt, :]
        y = y + ccw_ref[j:j + 1, :] * win
    y = _layer_norm(y + ccb_ref[...], cg_ref[...], cb_ref[...])
    o_ref[:, w:2 * w] = (y * _sigmoid(y)).astype(o_ref.dtype)
    cbuf_ref[0:HALO, :] = cbuf_ref[tt:tt + HALO, :]


def _block_diag(wb):
    nb, bd, _ = wb.shape
    eye = jnp.eye(nb, dtype=jnp.bool_)
    return jnp.where(eye[:, None, :, None], wb[:, :, None, :], 0.0).reshape(nb * bd, nb * bd)


def _seqmix(rest, lcw, lcb, wa, ba, wx, bx, lam, ccw, ccb, cg, cb, tt):
    s, n = rest.shape
    w = n // 4
    row = lambda a: a.reshape(1, w).astype(F32)
    full = lambda r, c: pl.BlockSpec((r, c), lambda i: (0, 0))
    kern = functools.partial(_seqmix_kernel, tt=tt, w=w)
    return pl.pallas_call(
        kern,
        grid=(s // tt,),
        in_specs=[pl.BlockSpec((tt, n), lambda i: (i, 0)),
                  full(LRU_CONV, w), full(1, w), full(w, w), full(1, w), full(w, w), full(1, w),
                  full(1, w), full(CONV_KERNEL, w), full(1, w), full(1, w), full(1, w)],
        out_specs=pl.BlockSpec((tt, 2 * w), lambda i: (i, 0)),
        out_shape=jax.ShapeDtypeStruct((s, 2 * w), BF16),
        scratch_shapes=[pltpu.VMEM((tt + HALO, w), F32), pltpu.VMEM((tt + HALO, w), F32),
                        pltpu.VMEM((SUBLANES - 1, tt + HALO - SUBLANES, w), F32),
                        pltpu.VMEM((tt, w), F32), pltpu.VMEM((tt, w), F32), pltpu.VMEM((tt, w), F32),
                        pltpu.VMEM((1, w), F32)],
        compiler_params=_params(),
        name="seqmix",
    )(rest, lcw, row(lcb), _block_diag(wa).astype(BF16), row(ba), _block_diag(wx).astype(BF16),
      row(bx), row(lam), ccw, row(ccb), row(cg), row(cb))


def _outproj_kernel(att_ref, rc_ref, h_ref, wo_ref, g_ref, b_ref, rwh_ref, rwl_ref, rb_ref,
                    h1_ref, h1c_ref, idx_ref, gate_ref, rank_ref, cnt_ref, tri_ref, carry_ref,
                    *, tm, n_att):
    i = pl.program_id(0)
    ne = N_EXPERTS

    @pl.when(i == 0)
    def _init():
        r = lax.broadcasted_iota(jnp.int32, (tm, tm), 0)
        c = lax.broadcasted_iota(jnp.int32, (tm, tm), 1)
        tri_ref[...] = jnp.where(c < r, 1.0, 0.0).astype(BF16)
        carry_ref[...] = jnp.zeros(carry_ref.shape, F32)

    mix = (jnp.dot(att_ref[...], wo_ref[0:n_att, :], preferred_element_type=F32)
           + jnp.dot(rc_ref[...], wo_ref[n_att:, :], preferred_element_type=F32))
    h1 = _layer_norm(RESID_ALPHA * h_ref[...] + mix, g_ref[...], b_ref[...])
    h1_ref[...] = h1
    _store_rows(h1c_ref, h1)

    h1_hi = h1.astype(BF16)
    h1_lo = (h1 - h1_hi.astype(F32)).astype(BF16)
    logits = (jnp.dot(h1_hi, rwh_ref[...], preferred_element_type=F32)
              + jnp.dot(h1_hi, rwl_ref[...], preferred_element_type=F32)
              + jnp.dot(h1_lo, rwh_ref[...], preferred_element_type=F32)) + rb_ref[...]
    lane = lax.broadcasted_iota(jnp.int32, (tm, ne), 1).astype(F32)
    onehot = jnp.zeros((tm, ne), F32)
    vals, sels = [], []
    for _ in range(TOP_K):
        mx = jnp.max(logits, axis=-1, keepdims=True)
        sel = jnp.min(jnp.where(logits == mx, lane, float(ne)), axis=-1, keepdims=True)
        hit = lane == sel
        onehot = onehot + jnp.where(hit, 1.0, 0.0)
        logits = jnp.where(hit, -jnp.inf, logits)
        vals.append(mx)
        sels.append(sel)
    ex = [jnp.exp(v - vals[0]) for v in vals]
    den = ex[0] + ex[1] + ex[2] + ex[3]
    before = jnp.dot(tri_ref[...], onehot.astype(BF16), preferred_element_type=F32) + carry_ref[...]
    for k in range(TOP_K):
        idx_ref[:, k:k + 1] = sels[k].astype(jnp.int32)
        gate_ref[:, k:k + 1] = ex[k] / den
        rank_ref[:, k:k + 1] = jnp.sum(jnp.where(lane == sels[k], before, 0.0), axis=-1,
                                       keepdims=True).astype(jnp.int32)
    carry_ref[...] = carry_ref[...] + jnp.sum(onehot, axis=0, keepdims=True)
    cnt_ref[...] = carry_ref[...].astype(jnp.int32)


def _outproj_router(att, rc, h, wo_bf16, g, b, rw, rb, tm):
    s, d = h.shape
    n_att = att.shape[1]
    ne = N_EXPERTS
    row = lambda a: a.reshape(1, -1).astype(F32)
    full = lambda r, c: pl.BlockSpec((r, c), lambda i: (0, 0))
    tile = lambda c: pl.BlockSpec((tm, c), lambda i: (i, 0))
    kern = functools.partial(_outproj_kernel, tm=tm, n_att=n_att)
    rw_hi = rw.astype(BF16)
    rw_lo = (rw - rw_hi.astype(F32)).astype(BF16)
    return pl.pallas_call(
        kern,
        grid=(s // tm,),
        in_specs=[tile(n_att), tile(rc.shape[1]), tile(d), full(wo_bf16.shape[0], d),
                  full(1, d), full(1, d), full(d, ne), full(d, ne), full(1, ne)],
        out_specs=[tile(d), pl.BlockSpec((tm * (d // LANES), LANES), lambda i: (i, 0)),
                   tile(TOP_K), tile(TOP_K), tile(TOP_K), full(1, ne)],
        out_shape=[jax.ShapeDtypeStruct((s, d), F32),
                   jax.ShapeDtypeStruct((s * (d // LANES), LANES), F32),
                   jax.ShapeDtypeStruct((s, TOP_K), jnp.int32),
                   jax.ShapeDtypeStruct((s, TOP_K), F32),
                   jax.ShapeDtypeStruct((s, TOP_K), jnp.int32),
                   jax.ShapeDtypeStruct((1, ne), jnp.int32)],
        scratch_shapes=[pltpu.VMEM((tm, tm), BF16), pltpu.VMEM((1, ne), F32)],
        compiler_params=_params(),
        name="outproj_router",
    )(att, rc, h, wo_bf16, row(g), row(b), rw_hi, rw_lo, row(rb))


def _row_copy(src_ref, src_row, dst_ref, dst_row, sem, nc):
    src = src_ref.at[pl.ds(pl.multiple_of(src_row * nc, nc), nc)]
    dst = dst_ref.at[pl.ds(pl.multiple_of(dst_row * nc, nc), nc)]
    return pltpu.make_async_copy(src, dst, sem)


def _dest_kernel(idx_ref, rank_ref, pstart_ref, dest_ref):
    tm, ne = idx_ref.shape[0], pstart_ref.shape[1]
    lane = lax.broadcasted_iota(jnp.int32, (tm, ne), 1)
    pstart = pstart_ref[...].astype(F32)
    for k in range(TOP_K):
        hit = lane == idx_ref[:, k:k + 1]
        start = jnp.sum(jnp.where(hit, pstart, 0.0), axis=-1, keepdims=True)
        dest_ref[:, k:k + 1] = start.astype(jnp.int32) + rank_ref[:, k:k + 1]


def _dest_rows(idx, rank, pstart, tm):
    s = idx.shape[0]
    ne = pstart.shape[0]
    tile = pl.BlockSpec((tm, TOP_K), lambda i: (i, 0))
    return pl.pallas_call(
        _dest_kernel,
        grid=(s // tm,),
        in_specs=[tile, tile, pl.BlockSpec((1, ne), lambda i: (0, 0))],
        out_specs=tile,
        out_shape=jax.ShapeDtypeStruct((s, TOP_K), jnp.int32),
        compiler_params=_params(),
        name="moe_dest_rows",
    )(idx, rank, pstart.reshape(1, ne))


def _dispatch_kernel(pstart_ref, pend_ref, dest_ref, h_ref, xd_ref, zero_ref, sem, zsem,
                     *, tt, tm, nc, n_blocks):
    i = pl.program_id(0)

    def zero_copy(e):
        start = pl.multiple_of((pend_ref[e] - tm) * nc, tm * nc)
        return pltpu.make_async_copy(zero_ref, xd_ref.at[pl.ds(start, tm * nc)], zsem)

    def tail_copy(b):
        start = pl.multiple_of(b * (tm * nc), tm * nc)
        return pltpu.make_async_copy(zero_ref, xd_ref.at[pl.ds(start, tm * nc)], zsem)

    @pl.when(i == 0)
    def _clear_padding():
        zero_ref[...] = jnp.zeros(zero_ref.shape, zero_ref.dtype)
        n_used = pend_ref[N_EXPERTS - 1] // tm

        def start_tail(b, carry):
            tail_copy(b).start()
            return carry

        def wait_tail(b, carry):
            tail_copy(b).wait()
            return carry

        lax.fori_loop(n_used, n_blocks, start_tail, 0)
        lax.fori_loop(n_used, n_blocks, wait_tail, 0)
        for e in range(N_EXPERTS):
            @pl.when(pend_ref[e] > pstart_ref[e])
            def _():
                zero_copy(e).start()
        for e in range(N_EXPERTS):
            @pl.when(pend_ref[e] > pstart_ref[e])
            def _():
                zero_copy(e).wait()

    def issue(t, carry):
        for k in range(TOP_K):
            _row_copy(h_ref, t, xd_ref, dest_ref[t * TOP_K + k], sem, nc).start(priority=k % 2)
        return carry

    lax.fori_loop(0, tt, issue, 0)

    for k in range(TOP_K):
        pltpu.make_async_copy(h_ref, xd_ref.at[pl.ds(0, tt * nc)], sem).wait()


def _dispatch(h1c, dest_flat, pstart, pend, rows, s, tt, tm):
    nc = h1c.shape[0] // s
    smem = lambda n: pl.BlockSpec((n,), lambda i, *_: (i,), memory_space=pltpu.SMEM)
    kern = functools.partial(_dispatch_kernel, tt=tt, tm=tm, nc=nc, n_blocks=rows // tm)
    return pl.pallas_call(
        kern,
        grid_spec=pltpu.PrefetchScalarGridSpec(
            num_scalar_prefetch=2,
            grid=(s // tt,),
            in_specs=[smem(tt * TOP_K),
                      pl.BlockSpec((tt * nc, LANES), lambda i, *_: (i, 0))],
            out_specs=pl.BlockSpec(memory_space=pl.ANY),
            scratch_shapes=[pltpu.VMEM((tm * nc, LANES), F32), pltpu.SemaphoreType.DMA(()),
                            pltpu.SemaphoreType.DMA(())]),
        out_shape=jax.ShapeDtypeStruct((rows * nc, LANES), F32),
        compiler_params=_params(),
        name="moe_dispatch",
    )(pstart, pend, dest_flat, h1c)


def _expert_kernel(bexp_ref, nused_ref, x_ref, w1_ref, b1_ref, w2_ref, b2_ref, y_ref,
                   w1b_ref, w2b_ref, *, f, chunk, tm):
    b = pl.program_id(0)
    last = nused_ref[0] - 1
    e = bexp_ref[jnp.minimum(b, last)]
    e_prev = bexp_ref[jnp.maximum(jnp.minimum(b, last) - 1, 0)]

    @pl.when((b == 0) | (e != e_prev))
    def _cast_weights():
        def cast(c, carry):
            r = pl.multiple_of(c * chunk, chunk)
            w1b_ref[pl.ds(r, chunk), :] = w1_ref[pl.ds(r, chunk), :].astype(BF16)
            w2b_ref[pl.ds(r, chunk), :] = w2_ref[pl.ds(r, chunk), :].astype(BF16)
            return carry
        lax.fori_loop(0, w1_ref.shape[0] // chunk, cast, 0)

    @pl.when(b > last)
    def _unused_block():
        y_ref[...] = jnp.zeros(y_ref.shape, y_ref.dtype)

    @pl.when(b <= last)
    def _mlp():
        x = _load_rows(x_ref, tm, w1_ref.shape[0]).astype(BF16)
        hdn = jnp.dot(x, w1b_ref[...], preferred_element_type=F32) + b1_ref[...]
        glu = jnp.minimum(hdn[:, 0:f], SWIGLU_LIMIT)
        lin = jnp.clip(hdn[:, f:2 * f], -SWIGLU_LIMIT, SWIGLU_LIMIT)
        act = glu * _sigmoid(SWIGLU_ALPHA * glu) * (lin + 1.0)
        _store_rows(y_ref, jnp.dot(act.astype(BF16), w2b_ref[...], preferred_element_type=F32)
                    + b2_ref[...])


def _experts(x_disp, bexp, nused, w1, b1, w2, b2, layer, tm):
    d = w1.shape[2]
    f = w2.shape[2]
    nc = d // LANES
    rows = x_disp.shape[0] // nc
    assert f == d, "the weight-cast loop assumes d_expert == d_model"

    def blk(b, bexp_ref, nused_ref):
        return jnp.minimum(b, nused_ref[0] - 1)

    def wmap(b, bexp_ref, nused_ref):
        return (layer, bexp_ref[blk(b, bexp_ref, nused_ref)], 0, 0)

    def bmap(b, bexp_ref, nused_ref):
        return (layer, bexp_ref[blk(b, bexp_ref, nused_ref)], 0, 0)

    kern = functools.partial(_expert_kernel, f=f, chunk=128, tm=tm)
    return pl.pallas_call(
        kern,
        grid_spec=pltpu.PrefetchScalarGridSpec(
            num_scalar_prefetch=2,
            grid=(rows // tm,),
            in_specs=[pl.BlockSpec((tm * nc, LANES), lambda b, be, nu: (b, 0)),
                      pl.BlockSpec((None, None, d, 2 * f), wmap),
                      pl.BlockSpec((None, None, 1, 2 * f), bmap),
                      pl.BlockSpec((None, None, f, d), wmap),
                      pl.BlockSpec((None, None, 1, d), bmap)],
            out_specs=pl.BlockSpec((tm * nc, LANES), lambda b, be, nu: (b, 0)),
            scratch_shapes=[pltpu.VMEM((d, 2 * f), BF16), pltpu.VMEM((f, d), BF16)]),
        out_shape=jax.ShapeDtypeStruct((rows * nc, LANES), F32),
        compiler_params=_params(),
        name="moe_experts",
    )(bexp, nused, x_disp, w1, b1.reshape(b1.shape[0], b1.shape[1], 1, -1), w2,
      b2.reshape(b2.shape[0], b2.shape[1], 1, -1))


def _combine_kernel(dest_ref, h_ref, gate_ref, g_ref, b_ref, yd_ref, o_ref,
                    ybuf_ref, sems, *, tt, nc):
    th = tt // COMBINE_PARTS
    d = h_ref.shape[1]

    def issue(part):
        def one(t, carry):
            for k in range(TOP_K):
                _row_copy(yd_ref, dest_ref[t * TOP_K + k], ybuf_ref.at[k], t, sems.at[part],
                          nc).start(priority=k % 2)
            return carry
        lax.fori_loop(part * th, (part + 1) * th, one, 0)

    def finish(part):
        for k in range(TOP_K):
            pltpu.make_async_copy(yd_ref.at[pl.ds(0, th * nc)],
                                  ybuf_ref.at[k, pl.ds(part * th * nc, th * nc)],
                                  sems.at[part]).wait()
        rows = slice(part * th, (part + 1) * th)
        ff = gate_ref[rows, 0:1] * _load_rows(ybuf_ref, th, d, (0,), part * th)
        for k in range(1, TOP_K):
            ff = ff + gate_ref[rows, k:k + 1] * _load_rows(ybuf_ref, th, d, (k,), part * th)
        o_ref[rows, :] = _layer_norm(RESID_ALPHA * h_ref[rows, :] + ff, g_ref[...], b_ref[...])

    issue(0)
    for part in range(COMBINE_PARTS):
        if part + 1 < COMBINE_PARTS:
            issue(part + 1)
        finish(part)


def _combine(y_disp, h1, gates, dest_flat, g, b, tt):
    s, d = h1.shape
    nc = d // LANES
    row = lambda a: a.reshape(1, -1).astype(F32)
    kern = functools.partial(_combine_kernel, tt=tt, nc=nc)
    return pl.pallas_call(
        kern,
        grid=(s // tt,),
        in_specs=[pl.BlockSpec((tt * TOP_K,), lambda i: (i,), memory_space=pltpu.SMEM),
                  pl.BlockSpec((tt, d), lambda i: (i, 0)),
                  pl.BlockSpec((tt, TOP_K), lambda i: (i, 0)),
                  pl.BlockSpec((1, d), lambda i: (0, 0)),
                  pl.BlockSpec((1, d), lambda i: (0, 0)),
                  pl.BlockSpec(memory_space=pl.ANY)],
        out_specs=pl.BlockSpec((tt, d), lambda i: (i, 0)),
        scratch_shapes=[pltpu.VMEM((TOP_K, tt * nc, LANES), F32),
                        pltpu.SemaphoreType.DMA((COMBINE_PARTS,))],
        out_shape=jax.ShapeDtypeStruct((s, d), F32),
        compiler_params=_params(),
        name="moe_combine",
    )(dest_flat, h1, gates, row(g), row(b), y_disp)


def _tiles(s):
    t = lambda want: math.gcd(s, want)
    return dict(attn=t(512), seq=t(512), route=t(512), dest=t(2048), dispatch=t(1024),
                expert=t(512), combine=t(512))


def kernel(x, ln_in_g, ln_in_b, w_in, lam_q1, lam_k1, lam_q2, lam_k2, subln_g, lru_conv_w, lru_conv_b, lru_wa, lru_ba, lru_wx, lru_bx, lru_lambda, cf_conv_w, cf_conv_b, cf_ln_g, cf_ln_b, w_out, ln1_g, ln1_b, router_w, router_b, moe_w1, moe_b1, moe_w2, moe_b2, ln2_g, ln2_b):
    bsz, s, d = x.shape
    assert bsz == 1
    tl = _tiles(s)
    tm = tl["expert"]
    n_blocks = (s * TOP_K) // tm + N_EXPERTS
    rows = n_blocks * tm

    h = x.reshape(s, d)
    for l in range(DEPTH):
        lam_init = 0.8 - 0.6 * math.exp(-0.3 * l)
        proj = _inproj(h, ln_in_g, ln_in_b, w_in[l].astype(BF16), tl["attn"], input_ln=(l == 0))
        qt, k, vt, rest = proj[:4]
        if l == 0:
            h = proj[4]
        att = _attention(qt, k, vt, lam_q1[l], lam_k1[l], lam_q2[l], lam_k2[l], subln_g[l],
                         lam_init)
        rc = _seqmix(rest, lru_conv_w[l], lru_conv_b[l], lru_wa[l], lru_ba[l], lru_wx[l],
                     lru_bx[l], lru_lambda[l], cf_conv_w[l], cf_conv_b[l], cf_ln_g[l], cf_ln_b[l],
                     tl["seq"])
        h1, h1c, idx, gates, rank, counts = _outproj_router(
            att, rc, h, w_out[l].astype(BF16), ln1_g[l], ln1_b[l], router_w[l], router_b[l],
            tl["route"])
        counts = counts.reshape(N_EXPERTS)
        padded = (counts + tm - 1) // tm * tm
        e_ids = jnp.arange(N_EXPERTS, dtype=jnp.int32)
        pend = jnp.sum(jnp.where(e_ids[None, :] <= e_ids[:, None], padded[None, :], 0),
                       axis=1).astype(jnp.int32)
        pstart = pend - padded
        block_row = jnp.arange(n_blocks, dtype=jnp.int32) * tm
        bexp = jnp.minimum(jnp.sum((pend[None, :] <= block_row[:, None]).astype(jnp.int32), axis=1),
                           N_EXPERTS - 1).astype(jnp.int32)
        nused = (pend[-1:] // tm).astype(jnp.int32)
        dest_flat = _dest_rows(idx, rank, pstart, tl["dest"]).reshape(s * TOP_K)
        x_disp = _dispatch(h1c, dest_flat, pstart, pend, rows, s, tl["dispatch"], tm)
        y_disp = _experts(x_disp, bexp, nused, moe_w1, moe_b1, moe_w2, moe_b2, l, tm)
        h = _combine(y_disp, h1, gates, dest_flat, ln2_g[l], ln2_b[l], tl["combine"])
    return h.reshape(bsz, s, d)
```
